```python
import math
import jax, jax.numpy as jnp
from jax import lax
import numpy as np

D_MODEL = 1024
BATCH = 2
SEQ = 8192
DEPTH = 4

ATTN_HEAD_DIM = 64
ATTN_HEADS = D_MODEL // 256
ATTN_V_DIM = 2 * ATTN_HEAD_DIM
ATTN_QK_WIDTH = ATTN_HEADS * 2 * ATTN_HEAD_DIM
ATTN_V_WIDTH = ATTN_HEADS * ATTN_V_DIM
Q_BLOCK = 128
ROPE_THETA = 10000.0

SSM_GROUP_CH = 16
SSM_GROUPS = D_MODEL // 32
SSM_WIDTH = SSM_GROUPS * SSM_GROUP_CH
SSM_STATE = 64
SSM_DT_MIN = 1e-3
SSM_DT_MAX = 1e-1

N_BRANCHES = 2
OFF_K = ATTN_QK_WIDTH
OFF_V = OFF_K + ATTN_QK_WIDTH
OFF_U = OFF_V + ATTN_V_WIDTH
OFF_G = OFF_U + SSM_WIDTH
IN_WIDTH = OFF_G + N_BRANCHES * D_MODEL

D_FF = 256 * ((8 * D_MODEL // 3 + 255) // 256)
N_EXPERTS = 8
TOP_K = 2
MOE_BLOCK = 256
RMS_EPS = 1e-6

kernel_name = "hybrid_diffattn_s5_moe_encoder"


def rms_norm(x, g):
    xf = x.astype(jnp.float32)
    y = xf * lax.rsqrt(jnp.mean(xf * xf, axis=-1, keepdims=True) + RMS_EPS)
    return (y * g.astype(jnp.float32)).astype(x.dtype)


def rotary(x, positions):
    half = ATTN_HEAD_DIM // 2
    inv_freq = 1.0 / (ROPE_THETA ** (jnp.arange(half, dtype=jnp.float32) / half))
    ang = positions.astype(jnp.float32)[:, None] * inv_freq[None, :]
    cos = jnp.cos(ang)[:, None, None, :]
    sin = jnp.sin(ang)[:, None, None, :]
    xf = x.astype(jnp.float32)
    x1, x2 = xf[..., :half], xf[..., half:]
    return jnp.concatenate([x1 * cos - x2 * sin, x2 * cos + x1 * sin], axis=-1).astype(x.dtype)


def diff_attention(q, k, v, lam):
    bsz, seq = q.shape[0], q.shape[1]
    n_blocks = seq // Q_BLOCK
    scale = ATTN_HEAD_DIM ** -0.5
    qb = jnp.moveaxis(q.reshape(bsz, n_blocks, Q_BLOCK, ATTN_HEADS, 2, ATTN_HEAD_DIM), 1, 0)

    def one_block(q_blk):
        s = jnp.einsum('bqhcd,bkhcd->bhcqk', q_blk, k,
                       preferred_element_type=jnp.float32) * scale
        p = jax.nn.softmax(s, axis=-1)
        w = p[:, :, 0] - lam * p[:, :, 1]
        return jnp.einsum('bhqk,bkhe->bqhe', w.astype(v.dtype), v)

    o = lax.map(one_block, qb)
    return jnp.moveaxis(o, 0, 1).reshape(bsz, seq, ATTN_HEADS, ATTN_V_DIM)


def _linear_recurrence_combine(left, right):
    a_l, b_l = left
    a_r, b_r = right
    return a_r * a_l, a_r * b_l + b_r


def s5_bidirectional(u, lam_re, lam_im, log_dt, b_re, b_im, c_re, c_im, d_skip):
    bsz, seq, _ = u.shape
    uf = u.astype(jnp.float32).reshape(bsz, seq, SSM_GROUPS, SSM_GROUP_CH)
    uc = uf.astype(jnp.complex64)
    y = uf * d_skip.astype(jnp.float32).reshape(SSM_GROUPS, SSM_GROUP_CH)
    for direction in range(2):
        lam = lax.complex(lam_re[direction].astype(jnp.float32),
                          lam_im[direction].astype(jnp.float32))
        dt = jnp.exp(log_dt[direction].astype(jnp.float32))[:, None]
        a_bar = jnp.exp(lam * dt)
        b_mat = lax.complex(b_re[direction].astype(jnp.float32),
                            b_im[direction].astype(jnp.float32))
        b_bar = ((a_bar - 1.0) / lam)[..., None] * b_mat
        bu = jnp.einsum('gph,bsgh->bsgp', b_bar, uc)
        a_seq = jnp.broadcast_to(a_bar, bu.shape)
        _, states = lax.associative_scan(_linear_recurrence_combine, (a_seq, bu),
                                         reverse=(direction == 1), axis=1)
        c_mat = lax.complex(c_re[direction].astype(jnp.float32),
                            c_im[direction].astype(jnp.float32))
        y = y + jnp.real(jnp.einsum('ghp,bsgp->bsgh', c_mat, states))
    return y.reshape(bsz, seq, SSM_WIDTH)


def swiglu(h, w1, w3, w2):
    return (jax.nn.silu(h @ w1) * (h @ w3)) @ w2


def moe_swiglu(h, w_router, w1, w3, w2):
    bsz, seq, d = h.shape
    t = h.reshape(-1, d)
    n_tok = t.shape[0]
    logits = jnp.dot(t, w_router, preferred_element_type=jnp.float32)
    top_logit, top_idx = lax.top_k(logits, TOP_K)
    gate = jax.nn.softmax(top_logit, axis=-1)
    n_assign = n_tok * TOP_K
    flat_e = top_idx.reshape(-1)
    order = jnp.argsort(flat_e)
    sorted_e = flat_e[order]
    sorted_tok = order // TOP_K
    sorted_gate = gate.reshape(-1)[order]
    counts = jnp.bincount(flat_e, length=N_EXPERTS)
    padded = (counts + MOE_BLOCK - 1) // MOE_BLOCK * MOE_BLOCK
    pad_end = jnp.cumsum(padded)
    pad_start = pad_end - padded
    grp_start = jnp.cumsum(counts) - counts
    dest = pad_start[sorted_e] + jnp.arange(n_assign) - grp_start[sorted_e]
    n_slots = ((n_assign + MOE_BLOCK - 1) // MOE_BLOCK) * MOE_BLOCK + N_EXPERTS * MOE_BLOCK
    slot_tok = jnp.zeros((n_slots,), jnp.int32).at[dest].set(sorted_tok.astype(jnp.int32))
    slot_gate = jnp.zeros((n_slots,), jnp.float32).at[dest].set(sorted_gate)
    n_blocks = n_slots // MOE_BLOCK
    block_start = jnp.arange(n_blocks) * MOE_BLOCK
    block_exp = jnp.minimum(jnp.searchsorted(pad_end, block_start, side='right'), N_EXPERTS - 1)
    xb = t[slot_tok].reshape(n_blocks, MOE_BLOCK, d)

    def expert_block(args):
        x_blk, e = args
        return (jax.nn.silu(x_blk @ w1[e]) * (x_blk @ w3[e])) @ w2[e]

    yb = lax.map(expert_block, (xb, block_exp)).reshape(n_slots, d)
    y = jnp.zeros_like(t).at[slot_tok].add(yb * slot_gate[:, None].astype(yb.dtype))
    return y.reshape(bsz, seq, d)


def setup_inputs(seed: int = 0) -> dict:
    key = jax.random.key(seed)
    ks = iter(jax.random.split(key, 40))
    f32 = jnp.float32
    n_dense = (DEPTH + 1) // 2
    n_moe = DEPTH // 2

    def nrm(shape, scale):
        return jax.random.normal(next(ks), shape, f32) * scale

    def gain(shape):
        return 1.0 + nrm(shape, 0.02)

    ssm_p = (DEPTH, 2, SSM_GROUPS, SSM_STATE)
    n_idx = jnp.arange(SSM_STATE, dtype=f32)
    return {
        "x": nrm((BATCH, SEQ, D_MODEL), 1.0),
        "norm_mix": gain((DEPTH, D_MODEL)),
        "w_in": nrm((DEPTH, D_MODEL, IN_WIDTH), D_MODEL ** -0.5),
        "attn_lambda": nrm((DEPTH, 4, ATTN_HEAD_DIM), 0.1),
        "attn_subln": gain((DEPTH, ATTN_V_DIM)),
        "ssm_lam_re": -0.5 + nrm(ssm_p, 0.01),
        "ssm_lam_im": math.pi * n_idx + nrm(ssm_p, 0.01),
        "ssm_log_dt": jax.random.uniform(next(ks), (DEPTH, 2, SSM_GROUPS), f32,
                                         math.log(SSM_DT_MIN), math.log(SSM_DT_MAX)),
        "ssm_b_re": nrm((DEPTH, 2, SSM_GROUPS, SSM_STATE, SSM_GROUP_CH), (2 * SSM_GROUP_CH) ** -0.5),
        "ssm_b_im": nrm((DEPTH, 2, SSM_GROUPS, SSM_STATE, SSM_GROUP_CH), (2 * SSM_GROUP_CH) ** -0.5),
        "ssm_c_re": nrm((DEPTH, 2, SSM_GROUPS, SSM_GROUP_CH, SSM_STATE), (2 * SSM_STATE) ** -0.5),
        "ssm_c_im": nrm((DEPTH, 2, SSM_GROUPS, SSM_GROUP_CH, SSM_STATE), (2 * SSM_STATE) ** -0.5),
        "ssm_d": nrm((DEPTH, SSM_WIDTH), 1.0),
        "w_glu": nrm((DEPTH, SSM_WIDTH, SSM_WIDTH), SSM_WIDTH ** -0.5),
        "w_branch": nrm((DEPTH, N_BRANCHES, ATTN_V_WIDTH, D_MODEL), ATTN_V_WIDTH ** -0.5),
        "w_out": nrm((DEPTH, D_MODEL, D_MODEL), D_MODEL ** -0.5),
        "norm_ffn": gain((DEPTH, D_MODEL)),
        "ffn_w1": nrm((n_dense, D_MODEL, D_FF), D_MODEL ** -0.5),
        "ffn_w3": nrm((n_dense, D_MODEL, D_FF), D_MODEL ** -0.5),
        "ffn_w2": nrm((n_dense, D_FF, D_MODEL), D_FF ** -0.5),
        "moe_router": nrm((n_moe, D_MODEL, N_EXPERTS), D_MODEL ** -0.5),
        "moe_w1": nrm((n_moe, N_EXPERTS, D_MODEL, D_FF), D_MODEL ** -0.5),
        "moe_w3": nrm((n_moe, N_EXPERTS, D_MODEL, D_FF), D_MODEL ** -0.5),
        "moe_w2": nrm((n_moe, N_EXPERTS, D_FF, D_MODEL), D_FF ** -0.5),
        "norm_final": gain((D_MODEL,)),
    }


def reference(x, norm_mix, w_in, attn_lambda, attn_subln, ssm_lam_re, ssm_lam_im, ssm_log_dt,
              ssm_b_re, ssm_b_im, ssm_c_re, ssm_c_im, ssm_d, w_glu, w_branch, w_out, norm_ffn,
              ffn_w1, ffn_w3, ffn_w2, moe_router, moe_w1, moe_w3, moe_w2, norm_final):
    bsz, seq, _ = x.shape
    positions = jnp.arange(seq, dtype=jnp.int32)
    for layer in range(DEPTH):
        xn = rms_norm(x, norm_mix[layer])
        proj = xn @ w_in[layer]
        q = proj[..., :OFF_K].reshape(bsz, seq, ATTN_HEADS, 2, ATTN_HEAD_DIM)
        k = proj[..., OFF_K:OFF_V].reshape(bsz, seq, ATTN_HEADS, 2, ATTN_HEAD_DIM)
        v = proj[..., OFF_V:OFF_U].reshape(bsz, seq, ATTN_HEADS, ATTN_V_DIM)
        u = proj[..., OFF_U:OFF_G]
        gates = jax.nn.sigmoid(proj[..., OFF_G:].astype(jnp.float32)).reshape(
            bsz, seq, N_BRANCHES, D_MODEL)

        q = rotary(q, positions)
        k = rotary(k, positions)
        lam_init = 0.8 - 0.6 * math.exp(-0.3 * layer)
        lp = attn_lambda[layer].astype(jnp.float32)
        lam = jnp.exp(jnp.sum(lp[0] * lp[1])) - jnp.exp(jnp.sum(lp[2] * lp[3])) + lam_init
        a = diff_attention(q, k, v, lam)
        a = (rms_norm(a, attn_subln[layer]) * (1.0 - lam_init)).reshape(bsz, seq, ATTN_V_WIDTH)

        s = s5_bidirectional(u, ssm_lam_re[layer], ssm_lam_im[layer], ssm_log_dt[layer],
                             ssm_b_re[layer], ssm_b_im[layer], ssm_c_re[layer], ssm_c_im[layer],
                             ssm_d[layer]).astype(x.dtype)
        s = jax.nn.gelu(s, approximate=False)
        s = s * jax.nn.sigmoid(s @ w_glu[layer])

        merged = (gates[:, :, 0] * (a @ w_branch[layer, 0])
                  + gates[:, :, 1] * (s @ w_branch[layer, 1])).astype(x.dtype)
        x = x + merged @ w_out[layer]

        hn = rms_norm(x, norm_ffn[layer])
        if layer % 2 == 0:
            i = layer // 2
            f = swiglu(hn, ffn_w1[i], ffn_w3[i], ffn_w2[i])
        else:
            i = layer // 2
            f = moe_swiglu(hn, moe_router[i], moe_w1[i], moe_w3[i], moe_w2[i])
        x = x + f.astype(x.dtype)
    return rms_norm(x, norm_final)
```

```python
import functools
import math

import jax
import jax.numpy as jnp
from jax import lax
from jax.experimental import pallas as pl
from jax.experimental.pallas import tpu as pltpu

F32 = jnp.float32
BF16 = jnp.bfloat16
I32 = jnp.int32

RMS_EPS = 1e-6
ROPE_THETA = 10000.0
LOG2E = 1.4426950408889634

LANES = 128
SUBLANES = 8
VMEM_BYTES_V7X = 64 * 1024 * 1024

ATTN_HEADS = 4
ATTN_HEAD_DIM = 64
HEAD_SLAB = 2 * ATTN_HEAD_DIM
SSM_GROUP_CH = 16
SSM_STATE = 64
SSM_OCTET = LANES // SSM_GROUP_CH
SSM_CHUNK = 8
N_EXPERTS = 8
TOP_K = 2
SEG = 512


def _cparams(sem, vmem_mb):
    return pltpu.CompilerParams(dimension_semantics=sem, vmem_limit_bytes=vmem_mb * 1024 * 1024)


def _rms(x, g):
    return x * lax.rsqrt(jnp.mean(x * x, axis=-1, keepdims=True) + RMS_EPS) * g


def _inproj_kernel(x_ref, g_ref, w_ref, cos_ref, sin_ref, o_ref, xn_ref, *, qscale):
    j = pl.program_id(1)

    @pl.when(j == 0)
    def _():
        xn_ref[...] = _rms(x_ref[...], g_ref[...]).astype(BF16)

    acc = jnp.dot(xn_ref[...], w_ref[...], preferred_element_type=F32)

    def rot(a):
        cos = cos_ref[...]
        sin = sin_ref[...]
        parts = []
        for s in range(0, a.shape[1], HEAD_SLAB):
            blk = a[:, s:s + HEAD_SLAB]
            parts.append(blk * cos + pltpu.roll(blk, HEAD_SLAB // 2, 1) * sin)
        return jnp.concatenate(parts, axis=1)

    @pl.when(j == 0)
    def _():
        o_ref[...] = (rot(acc) * qscale).astype(BF16)

    @pl.when(j == 1)
    def _():
        o_ref[...] = rot(acc).astype(BF16)

    @pl.when((j == 2) | (j == 3))
    def _():
        o_ref[...] = acc.astype(BF16)

    @pl.when(j >= 4)
    def _():
        o_ref[...] = jax.nn.sigmoid(acc).astype(BF16)


def _inproj(x, g, w, cos_t, sin_t, seq, bm):
    t, d = x.shape
    n = w.shape[1]
    qscale = ATTN_HEAD_DIM ** -0.5 * LOG2E
    nseq = seq // bm
    return pl.pallas_call(
        functools.partial(_inproj_kernel, qscale=qscale),
        grid=(t // bm, n // SEG),
        in_specs=[
            pl.BlockSpec((bm, d), lambda i, j: (i, 0)),
            pl.BlockSpec((1, d), lambda i, j: (0, 0)),
            pl.BlockSpec((d, SEG), lambda i, j: (0, j)),
            pl.BlockSpec((bm, HEAD_SLAB), lambda i, j: (i % nseq, 0)),
            pl.BlockSpec((bm, HEAD_SLAB), lambda i, j: (i % nseq, 0)),
        ],
        out_specs=pl.BlockSpec((bm, SEG), lambda i, j: (i, j)),
        out_shape=jax.ShapeDtypeStruct((t, n), BF16),
        scratch_shapes=[pltpu.VMEM((bm, d), BF16)],
        compiler_params=_cparams(("arbitrary", "arbitrary"), 40),
        name="inproj",
    )(x, g, w, cos_t, sin_t)


def _attn_kernel(sc_ref, lp_ref, q_ref, k_ref, v_ref, g_ref, o_ref, vext_ref, acc_ref, m_ref, *, bk):
    qi = pl.program_id(2)
    seq = k_ref.shape[0]
    bq = q_ref.shape[0]

    @pl.when(qi == 0)
    def _():
        vext_ref[:, :HEAD_SLAB] = v_ref[...]
        vext_ref[:, HEAD_SLAB:] = jnp.ones((seq, HEAD_SLAB), BF16)

    q = q_ref[...]
    lane = lax.broadcasted_iota(I32, q.shape, 1)
    half0 = (lane % ATTN_HEAD_DIM) < (ATTN_HEAD_DIM // 2)
    zero = jnp.zeros_like(q)
    qs = jnp.concatenate([jnp.where(half0, q, zero), jnp.where(half0, zero, q)], axis=0)

    acc_ref[...] = jnp.zeros_like(acc_ref)
    m_ref[...] = jnp.full_like(m_ref, -1e30)

    def step(j, carry):
        off = pl.multiple_of(j * bk, bk)
        kb = k_ref[pl.ds(off, bk), :]
        s = lax.dot_general(qs, kb, (((1,), (1,)), ((), ())), preferred_element_type=F32)
        m_prev = m_ref[...]
        m_new = jnp.maximum(m_prev, jnp.max(s, axis=-1, keepdims=True))
        alpha = jnp.exp2(m_prev - m_new)
        p = jnp.exp2(s - m_new).astype(BF16)
        acc_ref[...] = acc_ref[...] * alpha + jnp.dot(p, vext_ref[pl.ds(off, bk), :],
                                                      preferred_element_type=F32)
        m_ref[...] = m_new
        return carry

    lax.fori_loop(0, seq // bk, step, 0)

    lam_init = sc_ref[0]
    lp = lp_ref[...]
    lam = (jnp.exp(jnp.sum(lp[0:1] * lp[1:2], axis=-1, keepdims=True))
           - jnp.exp(jnp.sum(lp[2:3] * lp[3:4], axis=-1, keepdims=True)) + lam_init)
    acc = acc_ref[...]
    o1 = acc[:bq, :HEAD_SLAB] / acc[:bq, HEAD_SLAB:]
    o2 = acc[bq:, :HEAD_SLAB] / acc[bq:, HEAD_SLAB:]
    a = o1 - lam * o2
    o_ref[...] = (_rms(a, g_ref[...]) * (1.0 - lam_init)).astype(BF16)


def _attention(proj, lam_init, lam_p, subln, bsz, seq, bq, bk):
    t = proj.shape[0]
    nq = seq // bq
    koff = SEG // HEAD_SLAB
    return pl.pallas_call(
        functools.partial(_attn_kernel, bk=bk),
        grid=(bsz, ATTN_HEADS, nq),
        in_specs=[
            pl.BlockSpec(memory_space=pltpu.SMEM),
            pl.BlockSpec((4, ATTN_HEAD_DIM), lambda b, h, i: (0, 0)),
            pl.BlockSpec((bq, HEAD_SLAB), lambda b, h, i: (b * nq + i, h)),
            pl.BlockSpec((seq, HEAD_SLAB), lambda b, h, i: (b, koff + h)),
            pl.BlockSpec((seq, HEAD_SLAB), lambda b, h, i: (b, 2 * koff + h)),
            pl.BlockSpec((1, HEAD_SLAB), lambda b, h, i: (0, 0)),
        ],
        out_specs=pl.BlockSpec((bq, HEAD_SLAB), lambda b, h, i: (b * nq + i, h)),
        out_shape=jax.ShapeDtypeStruct((t, ATTN_HEADS * HEAD_SLAB), BF16),
        scratch_shapes=[
            pltpu.VMEM((seq, 2 * HEAD_SLAB), BF16),
            pltpu.VMEM((2 * bq, 2 * HEAD_SLAB), F32),
            pltpu.VMEM((2 * bq, 1), F32),
        ],
        compiler_params=_cparams(("arbitrary", "arbitrary", "arbitrary"), 40),
        name="diff_attention",
    )(lam_init, lam_p, proj, proj, proj, subln)


def _s5_tables(lam_re, lam_im, log_dt, b_re, b_im, c_re, c_im, d_skip):
    hp = lax.Precision.HIGHEST
    ll = SSM_CHUNK
    g = lam_re.shape[1]
    no = g // SSM_OCTET
    lam = lax.complex(lam_re, lam_im)
    ldt = lam * jnp.exp(log_dt)[..., None]
    a = jnp.exp(ldt)
    bbar = ((a - 1.0) / lam)[..., None] * lax.complex(b_re, b_im)
    cmat = lax.complex(c_re, c_im)
    n = jnp.arange(ll + 1, dtype=F32)
    pw = jnp.exp(ldt[:, None] * n[None, :, None, None])
    eye8 = jnp.eye(SSM_OCTET, dtype=F32)

    def kern(d):
        return jnp.real(jnp.einsum('ghp,dgp,gpk->dghk', cmat[d], pw[d, :ll], bbar[d], precision=hp))

    kf, kb = kern(0), kern(1)
    tt = jnp.arange(ll)
    lag = tt[None, :] - tt[:, None]
    mf = jnp.where((lag >= 0)[:, :, None, None, None], kf[jnp.clip(lag, 0, ll - 1)], 0.0)
    mb = jnp.where((lag <= 0)[:, :, None, None, None], kb[jnp.clip(-lag, 0, ll - 1)], 0.0)
    skip = (jnp.eye(ll, dtype=F32)[:, :, None, None, None]
            * (d_skip.reshape(g, SSM_GROUP_CH)[None, None, :, :, None] * jnp.eye(SSM_GROUP_CH, dtype=F32)[None, None, None]))
    m = (mf + mb + skip).reshape(ll, ll, no, SSM_OCTET, SSM_GROUP_CH, SSM_GROUP_CH)
    toe = jnp.einsum('jtoghk,gf->ojgktfh', m, eye8).reshape(no, ll * LANES, ll * LANES)

    ein = jnp.stack([pw[0, ll - 1 - tt][..., None] * bbar[0][None],
                     pw[1, tt][..., None] * bbar[1][None]])
    ein = jnp.stack([jnp.real(ein), jnp.imag(ein)], axis=1)
    ein = ein.reshape(2, 2, ll, no, SSM_OCTET, SSM_STATE, SSM_GROUP_CH)
    sin_ = jnp.einsum('drjogpk,gf->ojgkdrfp', ein, eye8).reshape(no, ll * LANES, 4 * SSM_OCTET * SSM_STATE)

    eout = jnp.stack([cmat[0][None] * pw[0, tt + 1][:, :, None, :],
                      cmat[1][None] * pw[1, ll - tt][:, :, None, :]])
    eout = jnp.stack([jnp.real(eout), -jnp.imag(eout)], axis=1)
    eout = eout.reshape(2, 2, ll, no, SSM_OCTET, SSM_GROUP_CH, SSM_STATE)
    sout = jnp.einsum('drtoghp,gf->odrfptgh', eout, eye8).reshape(no, 4 * SSM_OCTET * SSM_STATE, ll * LANES)

    al = pw[:, ll]
    apow = jnp.stack([jnp.real(al), jnp.imag(al)], axis=1)
    apow = apow.reshape(2, 2, no, SSM_OCTET * SSM_STATE).transpose(2, 0, 1, 3)
    return toe.astype(BF16), sin_.astype(BF16), sout.astype(BF16), apow.astype(F32)


def _s5_local_kernel(u_ref, toe_ref, sin_ref, yi_ref, st_ref):
    u = u_ref[...]
    yi_ref[...] = jnp.dot(u, toe_ref[...], preferred_element_type=F32)
    st_ref[...] = jnp.dot(u, sin_ref[...], preferred_element_type=F32)


def _s5_scan_kernel(st_ref, ap_ref, x_ref):
    n = st_ref.shape[0]
    w = st_ref.shape[1] // 4
    ap = ap_ref[...]
    far, fai = ap[0, 0:1], ap[0, 1:2]
    bar, bai = ap[1, 0:1], ap[1, 1:2]
    zrow = jnp.zeros((1, 2 * w), F32)
    x_ref[pl.ds(0, 1), pl.ds(0, 2 * w)] = zrow
    x_ref[pl.ds(n - 1, 1), pl.ds(2 * w, 2 * w)] = zrow

    def step(c, carry):
        fr, fi, br, bi = carry
        sf = st_ref[pl.ds(c - 1, 1), pl.ds(0, 2 * w)]
        fr, fi = far * fr - fai * fi + sf[:, :w], far * fi + fai * fr + sf[:, w:]
        x_ref[pl.ds(c, 1), pl.ds(0, 2 * w)] = jnp.concatenate([fr, fi], axis=1)
        cb = n - 1 - c
        sb = st_ref[pl.ds(cb + 1, 1), pl.ds(2 * w, 2 * w)]
        br, bi = bar * br - bai * bi + sb[:, :w], bar * bi + bai * br + sb[:, w:]
        x_ref[pl.ds(cb, 1), pl.ds(2 * w, 2 * w)] = jnp.concatenate([br, bi], axis=1)
        return fr, fi, br, bi

    z = jnp.zeros((1, w), F32)
    lax.fori_loop(1, n, step, (z, z, z, z))


def _s5_out_kernel(yi_ref, x_ref, sout_ref, y_ref):
    y_ref[...] = yi_ref[...] + jnp.dot(x_ref[...].astype(BF16), sout_ref[...], preferred_element_type=F32)


def _s5(u8, toe, sin_, sout, apow, bsz, ncb):
    no, nc, cw = u8.shape
    sw = sin_.shape[2]
    ncseq = nc // bsz
    yi, st = pl.pallas_call(
        _s5_local_kernel,
        grid=(no, nc // ncb),
        in_specs=[
            pl.BlockSpec((None, ncb, cw), lambda o, i: (o, i, 0)),
            pl.BlockSpec((None, cw, cw), lambda o, i: (o, 0, 0)),
            pl.BlockSpec((None, cw, sw), lambda o, i: (o, 0, 0)),
        ],
        out_specs=[
            pl.BlockSpec((None, ncb, cw), lambda o, i: (o, i, 0)),
            pl.BlockSpec((None, ncb, sw), lambda o, i: (o, i, 0)),
        ],
        out_shape=[jax.ShapeDtypeStruct((no, nc, cw), F32), jax.ShapeDtypeStruct((no, nc, sw), F32)],
        compiler_params=_cparams(("arbitrary", "arbitrary"), 40),
        name="s5_local",
    )(u8, toe, sin_)
    xin = pl.pallas_call(
        _s5_scan_kernel,
        grid=(no, bsz),
        in_specs=[
            pl.BlockSpec((None, ncseq, sw), lambda o, b: (o, b, 0)),
            pl.BlockSpec((None, 2, 2, sw // 4), lambda o, b: (o, 0, 0, 0)),
        ],
        out_specs=pl.BlockSpec((None, ncseq, sw), lambda o, b: (o, b, 0)),
        out_shape=jax.ShapeDtypeStruct((no, nc, sw), F32),
        compiler_params=_cparams(("arbitrary", "arbitrary"), 48),
        name="s5_scan",
    )(st, apow)
    return pl.pallas_call(
        _s5_out_kernel,
        grid=(no, nc // ncb),
        in_specs=[
            pl.BlockSpec((None, ncb, cw), lambda o, i: (o, i, 0)),
            pl.BlockSpec((None, ncb, sw), lambda o, i: (o, i, 0)),
            pl.BlockSpec((None, sw, cw), lambda o, i: (o, 0, 0)),
        ],
        out_specs=pl.BlockSpec((None, ncb, cw), lambda o, i: (o, i, 0)),
        out_shape=jax.ShapeDtypeStruct((no, nc, cw), F32),
        compiler_params=_cparams(("arbitrary", "arbitrary"), 40),
        name="s5_out",
    )(yi, xin, sout)


def _merge_kernel(a_ref, y_ref, g0_ref, g1_ref, x_ref, wglu_ref, wb0_ref, wb1_ref, wout_ref, o_ref):
    y = y_ref[...]
    s = 0.5 * y * (1.0 + lax.erf(y * (2.0 ** -0.5)))
    glu = jnp.dot(s.astype(BF16), wglu_ref[...], preferred_element_type=F32)
    s = s * jax.nn.sigmoid(glu)
    pa = jnp.dot(a_ref[...], wb0_ref[...], preferred_element_type=F32)
    ps = jnp.dot(s.astype(BF16), wb1_ref[...], preferred_element_type=F32)
    merged = g0_ref[...].astype(F32) * pa + g1_ref[...].astype(F32) * ps
    o_ref[...] = x_ref[...] + jnp.dot(merged.astype(BF16), wout_ref[...], preferred_element_type=F32)


def _merge(a, y, proj, x, wglu, wb0, wb1, wout, bm):
    t, d = x.shape
    aw = a.shape[1]
    g0 = 4 * SEG // d
    full = lambda arr: pl.BlockSpec(arr.shape, lambda i: (0,) * arr.ndim)
    return pl.pallas_call(
        _merge_kernel,
        grid=(t // bm,),
        in_specs=[
            pl.BlockSpec((bm, aw), lambda i: (i, 0)),
            pl.BlockSpec((bm, aw), lambda i: (i, 0)),
            pl.BlockSpec((bm, d), lambda i: (i, g0)),
            pl.BlockSpec((bm, d), lambda i: (i, g0 + 1)),
            pl.BlockSpec((bm, d), lambda i: (i, 0)),
            full(wglu), full(wb0), full(wb1), full(wout),
        ],
        out_specs=pl.BlockSpec((bm, d), lambda i: (i, 0)),
        out_shape=jax.ShapeDtypeStruct((t, d), F32),
        compiler_params=_cparams(("arbitrary",), 48),
        name="merge",
    )(a, y, proj, proj, x, wglu, wb0, wb1, wout)


def _swiglu_partial(h, w1_ref, w3_ref, w2_ref, sub):
    ffc = w1_ref.shape[-1]
    acc = None
    for s in range(0, ffc, sub):
        e = min(s + sub, ffc)
        a = jnp.dot(h, w1_ref[:, s:e], preferred_element_type=F32)
        b = jnp.dot(h, w3_ref[:, s:e], preferred_element_type=F32)
        tt = (a * jax.nn.sigmoid(a) * b).astype(BF16)
        c = jnp.dot(tt, w2_ref[s:e, :], preferred_element_type=F32)
        acc = c if acc is None else acc + c
    return acc


def _ffn_kernel(x_ref, g_ref, w1_ref, w3_ref, w2_ref, o_ref, hn_ref, *, sub):
    f = pl.program_id(1)

    @pl.when(f == 0)
    def _():
        x = x_ref[...]
        hn_ref[...] = _rms(x, g_ref[...]).astype(BF16)
        o_ref[...] = x

    o_ref[...] += _swiglu_partial(hn_ref[...], w1_ref, w3_ref, w2_ref, sub)


def _ffn_tile(dff):
    return dff // 2 if (dff // 2) % LANES == 0 else dff


def _ffn(x, g, w1, w3, w2, bm):
    t, d = x.shape
    dff = w1.shape[1]
    ffc = _ffn_tile(dff)
    return pl.pallas_call(
        functools.partial(_ffn_kernel, sub=2 * LANES),
        grid=(t // bm, dff // ffc),
        in_specs=[
            pl.BlockSpec((bm, d), lambda i, f: (i, 0)),
            pl.BlockSpec((1, d), lambda i, f: (0, 0)),
            pl.BlockSpec((d, ffc), lambda i, f: (0, f)),
            pl.BlockSpec((d, ffc), lambda i, f: (0, f)),
            pl.BlockSpec((ffc, d), lambda i, f: (f, 0)),
        ],
        out_specs=pl.BlockSpec((bm, d), lambda i, f: (i, 0)),
        out_shape=jax.ShapeDtypeStruct((t, d), F32),
        scratch_shapes=[pltpu.VMEM((bm, d), BF16)],
        compiler_params=_cparams(("arbitrary", "arbitrary"), 56),
        name="ffn_dense",
    )(x, g, w1, w3, w2)


def _router_kernel(x_ref, g_ref, wr_ref, hn_ref, r_ref):
    hn = _rms(x_ref[...], g_ref[...])
    hn_ref[...] = hn
    logits = jnp.dot(hn, wr_ref[...], preferred_element_type=F32, precision=lax.Precision.HIGHEST)
    lane = lax.broadcasted_iota(I32, logits.shape, 1)
    neg = jnp.float32(-jnp.inf)
    logits = jnp.where(lane < N_EXPERTS, logits, neg)
    m1 = jnp.max(logits, axis=-1, keepdims=True)
    i1 = jnp.min(jnp.where(logits == m1, lane, LANES), axis=-1, keepdims=True)
    rest = jnp.where(lane == i1, neg, logits)
    m2 = jnp.max(rest, axis=-1, keepdims=True)
    i2 = jnp.min(jnp.where(rest == m2, lane, LANES), axis=-1, keepdims=True)
    e2 = jnp.exp(m2 - m1)
    g1 = 1.0 / (1.0 + e2)
    g2 = e2 / (1.0 + e2)
    r_ref[...] = jnp.where(lane == 0, i1.astype(F32),
                           jnp.where(lane == 1, i2.astype(F32),
                                     jnp.where(lane == 2, g1, jnp.where(lane == 3, g2, 0.0))))


def _router(x, g, wr_pad, bm):
    t, d = x.shape
    return pl.pallas_call(
        _router_kernel,
        grid=(t // bm,),
        in_specs=[
            pl.BlockSpec((bm, d), lambda i: (i, 0)),
            pl.BlockSpec((1, d), lambda i: (0, 0)),
            pl.BlockSpec((d, LANES), lambda i: (0, 0)),
        ],
        out_specs=[pl.BlockSpec((bm, d), lambda i: (i, 0)), pl.BlockSpec((bm, LANES), lambda i: (i, 0))],
        out_shape=[jax.ShapeDtypeStruct((t, d), F32), jax.ShapeDtypeStruct((t, LANES), F32)],
        compiler_params=_cparams(("arbitrary",), 40),
        name="moe_router",
    )(x, g, wr_pad)


def _row_copy(src_hbm, dst_ref, sem, src_row, dst_row):
    return pltpu.make_async_copy(src_hbm.at[pl.ds(src_row, 1)], dst_ref.at[pl.ds(dst_row, 1)], sem)


def _gather_kernel(idx_ref, src_hbm, o_ref, sem):
    bm = o_ref.shape[0]
    base = pl.program_id(0) * bm

    def issue(r, c):
        _row_copy(src_hbm, o_ref, sem, idx_ref[base + r], r).start()
        return c

    lax.fori_loop(0, bm, issue, 0)

    def drain(r, c):
        _row_copy(src_hbm, o_ref, sem, 0, r).wait()
        return c

    lax.fori_loop(0, bm, drain, 0)


def _gather_rows(idx, src, bm):
    n = idx.shape[0]
    d = src.shape[1]
    return pl.pallas_call(
        _gather_kernel,
        grid_spec=pltpu.PrefetchScalarGridSpec(
            num_scalar_prefetch=1,
            grid=(n // bm,),
            in_specs=[pl.BlockSpec(memory_space=pl.ANY)],
            out_specs=pl.BlockSpec((bm, d), lambda i, idx: (i, 0)),
            scratch_shapes=[pltpu.SemaphoreType.DMA(())],
        ),
        out_shape=jax.ShapeDtypeStruct((n, d), src.dtype),
        compiler_params=_cparams(("arbitrary",), 32),
        name="moe_gather",
    )(idx, src)


def _expert_kernel(be_ref, x_ref, w1_ref, w3_ref, w2_ref, o_ref, *, sub):
    f = pl.program_id(1)
    part = _swiglu_partial(x_ref[...].astype(BF16), w1_ref, w3_ref, w2_ref, sub)

    @pl.when(f == 0)
    def _():
        o_ref[...] = part

    @pl.when(f > 0)
    def _():
        o_ref[...] += part


def _experts(block_exp, xb, w1, w3, w2, bm):
    n, d = xb.shape
    dff = w1.shape[2]
    ffc = _ffn_tile(dff)
    return pl.pallas_call(
        functools.partial(_expert_kernel, sub=2 * LANES),
        grid_spec=pltpu.PrefetchScalarGridSpec(
            num_scalar_prefetch=1,
            grid=(n // bm, dff // ffc),
            in_specs=[
                pl.BlockSpec((bm, d), lambda i, f, be: (i, 0)),
                pl.BlockSpec((None, d, ffc), lambda i, f, be: (be[i], 0, f)),
                pl.BlockSpec((None, d, ffc), lambda i, f, be: (be[i], 0, f)),
                pl.BlockSpec((None, ffc, d), lambda i, f, be: (be[i], f, 0)),
            ],
            out_specs=pl.BlockSpec((bm, d), lambda i, f, be: (i, 0)),
        ),
        out_shape=jax.ShapeDtypeStruct((n, d), F32),
        compiler_params=_cparams(("arbitrary", "arbitrary"), 56),
        name="moe_experts",
    )(block_exp, xb, w1, w3, w2)


def _combine_kernel(d0_ref, d1_ref, yb_hbm, x_ref, r_ref, o_ref, r0_ref, r1_ref, sem):
    bm = o_ref.shape[0]
    base = pl.program_id(0) * bm

    def issue(r, c):
        _row_copy(yb_hbm, r0_ref, sem, d0_ref[base + r], r).start()
        _row_copy(yb_hbm, r1_ref, sem, d1_ref[base + r], r).start()
        return c

    lax.fori_loop(0, bm, issue, 0)

    def drain(r, c):
        _row_copy(yb_hbm, r0_ref, sem, 0, r).wait()
        _row_copy(yb_hbm, r1_ref, sem, 0, r).wait()
        return c

    lax.fori_loop(0, bm, drain, 0)
    route = r_ref[...]
    o_ref[...] = x_ref[...] + route[:, 2:3] * r0_ref[...] + route[:, 3:4] * r1_ref[...]


def _combine(d0, d1, yb, x, route, bm):
    t, d = x.shape
    return pl.pallas_call(
        _combine_kernel,
        grid_spec=pltpu.PrefetchScalarGridSpec(
            num_scalar_prefetch=2,
            grid=(t // bm,),
            in_specs=[
                pl.BlockSpec(memory_space=pl.ANY),
                pl.BlockSpec((bm, d), lambda i, a, b: (i, 0)),
                pl.BlockSpec((bm, LANES), lambda i, a, b: (i, 0)),
            ],
            out_specs=pl.BlockSpec((bm, d), lambda i, a, b: (i, 0)),
            scratch_shapes=[pltpu.VMEM((bm, d), F32), pltpu.VMEM((bm, d), F32), pltpu.SemaphoreType.DMA(())],
        ),
        out_shape=jax.ShapeDtypeStruct((t, d), F32),
        compiler_params=_cparams(("arbitrary",), 32),
        name="moe_combine",
    )(d0, d1, yb, x, route)


def _dispatch_tables(idx, bm):
    t = idx.shape[0]
    flat_e = idx.reshape(-1)
    onehot = (flat_e[:, None] == jnp.arange(N_EXPERTS, dtype=I32)[None, :]).astype(I32)
    csum = jnp.cumsum(onehot, axis=0)
    rank = jnp.take_along_axis(csum, flat_e[:, None], axis=1)[:, 0] - 1
    counts = csum[-1]
    padded = (counts + bm - 1) // bm * bm
    pad_end = jnp.cumsum(padded)
    pad_start = pad_end - padded
    dest = pad_start[flat_e] + rank
    n_slots = t * TOP_K + N_EXPERTS * bm
    slot_tok = jnp.zeros((n_slots,), I32).at[dest].set(jnp.arange(t * TOP_K, dtype=I32) // TOP_K)
    block_start = jnp.arange(n_slots // bm, dtype=I32) * bm
    block_exp = jnp.minimum(jnp.searchsorted(pad_end, block_start, side='right'), N_EXPERTS - 1).astype(I32)
    dest = dest.reshape(t, TOP_K).astype(I32)
    return dest[:, 0], dest[:, 1], slot_tok, block_exp


def _moe(x, g, w_router, w1, w3, w2, bm_tok, bm_slot):
    d = x.shape[1]
    wr_pad = jnp.zeros((d, LANES), F32).at[:, :N_EXPERTS].set(w_router)
    hn, route = _router(x, g, wr_pad, bm_tok)
    idx = route[:, :TOP_K].astype(I32)
    d0, d1, slot_tok, block_exp = _dispatch_tables(idx, bm_slot)
    xb = _gather_rows(slot_tok, hn, bm_slot)
    yb = _experts(block_exp, xb, w1, w3, w2, bm_slot)
    return _combine(d0, d1, yb, x, route, bm_tok)


def _norm_kernel(x_ref, g_ref, o_ref):
    o_ref[...] = _rms(x_ref[...], g_ref[...])


def _final_norm(x, g, bm):
    t, d = x.shape
    return pl.pallas_call(
        _norm_kernel,
        grid=(t // bm,),
        in_specs=[pl.BlockSpec((bm, d), lambda i: (i, 0)), pl.BlockSpec((1, d), lambda i: (0, 0))],
        out_specs=pl.BlockSpec((bm, d), lambda i: (i, 0)),
        out_shape=jax.ShapeDtypeStruct((t, d), F32),
        compiler_params=_cparams(("arbitrary",), 32),
        name="final_norm",
    )(x, g)


def _rope_tables(seq):
    half = ATTN_HEAD_DIM // 2
    inv_freq = 1.0 / (ROPE_THETA ** (jnp.arange(half, dtype=F32) / half))
    ang = jnp.arange(seq, dtype=F32)[:, None] * inv_freq[None, :]
    cos = jnp.tile(jnp.cos(ang), (1, 4))
    sin = jnp.sin(ang)
    return cos, jnp.concatenate([-sin, -sin, sin, sin], axis=1)


def _permute_qk_columns(w):
    d = w.shape[0]
    half = ATTN_HEAD_DIM // 2
    qk = w[:, :2 * SEG].reshape(d, 2, ATTN_HEADS, 2, 2, half).transpose(0, 1, 2, 4, 3, 5).reshape(d, 2 * SEG)
    return jnp.concatenate([qk, w[:, 2 * SEG:]], axis=1)


def _block(n, pref):
    return pref if n % pref == 0 else n


def kernel(x, norm_mix, w_in, attn_lambda, attn_subln, ssm_lam_re, ssm_lam_im, ssm_log_dt, ssm_b_re, ssm_b_im, ssm_c_re, ssm_c_im, ssm_d, w_glu, w_branch, w_out, norm_ffn, ffn_w1, ffn_w3, ffn_w2, moe_router, moe_w1, moe_w3, moe_w2, norm_final):
    bsz, seq, d = x.shape
    depth = w_in.shape[0]
    t = bsz * seq
    ll = SSM_CHUNK
    nc = t // ll
    no = ssm_lam_re.shape[2] // SSM_OCTET
    cos_t, sin_t = _rope_tables(seq)
    xf = x.reshape(t, d)
    bm = _block(seq, 1024)
    for layer in range(depth):
        w = _permute_qk_columns(w_in[layer]).astype(BF16)
        proj = _inproj(xf, norm_mix[layer][None], w, cos_t, sin_t, seq, bm)

        lam_init = jnp.full((1,), 0.8 - 0.6 * math.exp(-0.3 * layer), F32)
        a = _attention(proj, lam_init, attn_lambda[layer], attn_subln[layer][None], bsz, seq,
                       _block(seq, 256), _block(seq, 512))

        toe, sin_, sout, apow = _s5_tables(ssm_lam_re[layer], ssm_lam_im[layer], ssm_log_dt[layer],
                                          ssm_b_re[layer], ssm_b_im[layer], ssm_c_re[layer],
                                          ssm_c_im[layer], ssm_d[layer])
        u = proj[:, 3 * SEG:4 * SEG]
        u8 = u.reshape(nc, ll, no, LANES).transpose(2, 0, 1, 3).reshape(no, nc, ll * LANES)
        y8 = _s5(u8, toe, sin_, sout, apow, bsz, _block(nc // bsz, 512))
        y = y8.reshape(no, nc, ll, LANES).transpose(1, 2, 0, 3).reshape(t, no * LANES)

        xf = _merge(a, y, proj, xf, w_glu[layer].astype(BF16), w_branch[layer, 0].astype(BF16),
                    w_branch[layer, 1].astype(BF16), w_out[layer].astype(BF16), _block(seq, 512))

        i = layer // 2
        if layer % 2 == 0:
            xf = _ffn(xf, norm_ffn[layer][None], ffn_w1[i].astype(BF16), ffn_w3[i].astype(BF16),
                      ffn_w2[i].astype(BF16), bm)
        else:
            xf = _moe(xf, norm_ffn[layer][None], moe_router[i], moe_w1[i].astype(BF16),
                      moe_w3[i].astype(BF16), moe_w2[i].astype(BF16), _block(seq, 256), 512)
    return _final_norm(xf, norm_final[None], bm).reshape(bsz, seq, d)
```

```python
import functools
import math

import jax
import jax.numpy as jnp
from jax import lax
from jax.experimental import pallas as pl
from jax.experimental.pallas import tpu as pltpu

F32 = jnp.float32
BF16 = jnp.bfloat16
I32 = jnp.int32

RMS_EPS = 1e-6
ROPE_THETA = 10000.0
LOG2E = 1.4426950408889634

LANES = 128
SUBLANES = 8
VMEM_BYTES_V7X = 64 * 1024 * 1024

ATTN_HEADS = 4
ATTN_HEAD_DIM = 64
HEAD_SLAB = 2 * ATTN_HEAD_DIM
SSM_GROUP_CH = 16
SSM_STATE = 64
SSM_OCTET = LANES // SSM_GROUP_CH
SSM_CHUNK = 8
N_EXPERTS = 8
TOP_K = 2
SEG = 512


def _cparams(sem, vmem_mb):
    return pltpu.CompilerParams(dimension_semantics=sem, vmem_limit_bytes=vmem_mb * 1024 * 1024)


def _rms(x, g):
    return x * lax.rsqrt(jnp.mean(x * x, axis=-1, keepdims=True) + RMS_EPS) * g


def _inproj_kernel(x_ref, g_ref, w_ref, cos_ref, sin_ref, o_ref, xn_ref, *, qscale):
    j = pl.program_id(1)

    @pl.when(j == 0)
    def _():
        xn_ref[...] = _rms(x_ref[...], g_ref[...]).astype(BF16)

    acc = jnp.dot(xn_ref[...], w_ref[...], preferred_element_type=F32)

    def rot(a):
        cos = cos_ref[...]
        sin = sin_ref[...]
        parts = []
        for s in range(0, a.shape[1], HEAD_SLAB):
            blk = a[:, s:s + HEAD_SLAB]
            parts.append(blk * cos + pltpu.roll(blk, HEAD_SLAB // 2, 1) * sin)
        return jnp.concatenate(parts, axis=1)

    @pl.when(j == 0)
    def _():
        o_ref[...] = (rot(acc) * qscale).astype(BF16)

    @pl.when(j == 1)
    def _():
        o_ref[...] = rot(acc).astype(BF16)

    @pl.when((j == 2) | (j == 3))
    def _():
        o_ref[...] = acc.astype(BF16)

    @pl.when(j >= 4)
    def _():
        o_ref[...] = jax.nn.sigmoid(acc).astype(BF16)


def _inproj(x, g, w, cos_t, sin_t, seq, bm):
    t, d = x.shape
    n = w.shape[1]
    qscale = ATTN_HEAD_DIM ** -0.5 * LOG2E
    nseq = seq // bm
    return pl.pallas_call(
        functools.partial(_inproj_kernel, qscale=qscale),
        grid=(t // bm, n // SEG),
        in_specs=[
            pl.BlockSpec((bm, d), lambda i, j: (i, 0)),
            pl.BlockSpec((1, d), lambda i, j: (0, 0)),
            pl.BlockSpec((d, SEG), lambda i, j: (0, j)),
            pl.BlockSpec((bm, HEAD_SLAB), lambda i, j: (i % nseq, 0)),
            pl.BlockSpec((bm, HEAD_SLAB), lambda i, j: (i % nseq, 0)),
        ],
        out_specs=pl.BlockSpec((bm, SEG), lambda i, j: (i, j)),
        out_shape=jax.ShapeDtypeStruct((t, n), BF16),
        scratch_shapes=[pltpu.VMEM((bm, d), BF16)],
        compiler_params=_cparams(("arbitrary", "arbitrary"), 40),
        name="inproj",
    )(x, g, w, cos_t, sin_t)


ONES_ROWS = 16


def _attn_kernel(sc_ref, lp_ref, q_ref, k_ref, v_ref, g_ref, o_ref, vt_ref, acc_ref, m_ref,
                 s0_ref, s1_ref, p0_ref, p1_ref, x0_ref, x1_ref, *, bk):
    qi = pl.program_id(2)
    seq = k_ref.shape[0]
    bq = q_ref.shape[0]
    nkv = seq // bk
    assert nkv >= 2 and nkv % 2 == 0

    @pl.when(qi == 0)
    def _():
        for c in range(nkv):
            vt_ref[c, :HEAD_SLAB, :] = v_ref[c * bk:(c + 1) * bk, :].astype(F32).T.astype(BF16)
            vt_ref[c, HEAD_SLAB:, :] = jnp.ones((ONES_ROWS, bk), BF16)

    qt = q_ref[...].astype(F32).T
    row = lax.broadcasted_iota(I32, qt.shape, 0)
    half0 = (row % ATTN_HEAD_DIM) < (ATTN_HEAD_DIM // 2)
    qst = jnp.concatenate([jnp.where(half0, qt, 0.0), jnp.where(half0, 0.0, qt)], axis=1).astype(BF16)

    acc_ref[...] = jnp.zeros_like(acc_ref)
    m_ref[...] = jnp.full_like(m_ref, -1e30)
    s_refs = (s0_ref, s1_ref)
    p_refs = (p0_ref, p1_ref)
    x_refs = (x0_ref, x1_ref)

    def qk(j, cur):
        off = pl.multiple_of(j * bk, bk)
        st = jnp.dot(k_ref[pl.ds(off, bk), :], qst, preferred_element_type=F32)
        s_refs[cur][...] = st
        x_refs[cur][...] = jnp.max(st, axis=0, keepdims=True)

    def pv(j, cur, alpha):
        acc_ref[...] = acc_ref[...] * alpha + jnp.dot(vt_ref[j], p_refs[cur][...],
                                                      preferred_element_type=F32)

    def softmax(cur):
        m_prev = m_ref[...]
        m_new = jnp.maximum(m_prev, x_refs[cur][...])
        p_refs[cur][...] = jnp.exp2((s_refs[cur][...] - m_new).astype(BF16))
        m_ref[...] = m_new
        return jnp.exp2(m_prev - m_new)

    def stage(j, cur, alpha):
        qk(j + 1, 1 - cur)
        pv(j - 1, 1 - cur, alpha)
        return softmax(cur)

    qk(0, 0)
    qk(1, 1)
    alpha = softmax(0)

    def pair(jj, alpha):
        j = 1 + 2 * jj
        return stage(j + 1, 0, stage(j, 1, alpha))

    alpha = lax.fori_loop(0, (nkv - 2) // 2, pair, alpha)
    pv(nkv - 2, 0, alpha)
    alpha = softmax(1)
    pv(nkv - 1, 1, alpha)

    lam_init = sc_ref[0]
    lp = lp_ref[...]
    lam = (jnp.exp(jnp.sum(lp[0:1] * lp[1:2], axis=-1, keepdims=True))
           - jnp.exp(jnp.sum(lp[2:3] * lp[3:4], axis=-1, keepdims=True)) + lam_init)
    acc = acc_ref[...]
    o = acc[:HEAD_SLAB] / acc[HEAD_SLAB:HEAD_SLAB + 1]
    a = o[:, :bq] - lam * o[:, bq:]
    a = a * lax.rsqrt(jnp.mean(a * a, axis=0, keepdims=True) + RMS_EPS) * (1.0 - lam_init)
    o_ref[...] = (a.T * g_ref[...]).astype(BF16)


def _attention(proj, lam_init, lam_p, subln, bsz, seq, bq, bk):
    t = proj.shape[0]
    nq = seq // bq
    koff = SEG // HEAD_SLAB
    return pl.pallas_call(
        functools.partial(_attn_kernel, bk=bk),
        grid=(bsz, ATTN_HEADS, nq),
        in_specs=[
            pl.BlockSpec(memory_space=pltpu.SMEM),
            pl.BlockSpec((4, ATTN_HEAD_DIM), lambda b, h, i: (0, 0)),
            pl.BlockSpec((bq, HEAD_SLAB), lambda b, h, i: (b * nq + i, h)),
            pl.BlockSpec((seq, HEAD_SLAB), lambda b, h, i: (b, koff + h)),
            pl.BlockSpec((seq, HEAD_SLAB), lambda b, h, i: (b, 2 * koff + h)),
            pl.BlockSpec((1, HEAD_SLAB), lambda b, h, i: (0, 0)),
        ],
        out_specs=pl.BlockSpec((bq, HEAD_SLAB), lambda b, h, i: (b * nq + i, h)),
        out_shape=jax.ShapeDtypeStruct((t, ATTN_HEADS * HEAD_SLAB), BF16),
        scratch_shapes=[
            pltpu.VMEM((seq // bk, HEAD_SLAB + ONES_ROWS, bk), BF16),
            pltpu.VMEM((HEAD_SLAB + ONES_ROWS, 2 * bq), F32),
            pltpu.VMEM((1, 2 * bq), F32),
            pltpu.VMEM((bk, 2 * bq), F32),
            pltpu.VMEM((bk, 2 * bq), F32),
            pltpu.VMEM((bk, 2 * bq), BF16),
            pltpu.VMEM((bk, 2 * bq), BF16),
            pltpu.VMEM((1, 2 * bq), F32),
            pltpu.VMEM((1, 2 * bq), F32),
        ],
        compiler_params=_cparams(("arbitrary", "arbitrary", "arbitrary"), 40),
        name="diff_attention",
    )(lam_init, lam_p, proj, proj, proj, subln)


def _s5_tables(lam_re, lam_im, log_dt, b_re, b_im, c_re, c_im, d_skip):
    hp = lax.Precision.HIGHEST
    ll = SSM_CHUNK
    g = lam_re.shape[1]
    no = g // SSM_OCTET
    lam = lax.complex(lam_re, lam_im)
    ldt = lam * jnp.exp(log_dt)[..., None]
    a = jnp.exp(ldt)
    bbar = ((a - 1.0) / lam)[..., None] * lax.complex(b_re, b_im)
    cmat = lax.complex(c_re, c_im)
    n = jnp.arange(ll + 1, dtype=F32)
    pw = jnp.exp(ldt[:, None] * n[None, :, None, None])
    eye8 = jnp.eye(SSM_OCTET, dtype=F32)

    def kern(d):
        return jnp.real(jnp.einsum('ghp,dgp,gpk->dghk', cmat[d], pw[d, :ll], bbar[d], precision=hp))

    kf, kb = kern(0), kern(1)
    tt = jnp.arange(ll)
    lag = tt[None, :] - tt[:, None]
    mf = jnp.where((lag >= 0)[:, :, None, None, None], kf[jnp.clip(lag, 0, ll - 1)], 0.0)
    mb = jnp.where((lag <= 0)[:, :, None, None, None], kb[jnp.clip(-lag, 0, ll - 1)], 0.0)
    skip = (jnp.eye(ll, dtype=F32)[:, :, None, None, None]
            * (d_skip.reshape(g, SSM_GROUP_CH)[None, None, :, :, None] * jnp.eye(SSM_GROUP_CH, dtype=F32)[None, None, None]))
    m = (mf + mb + skip).reshape(ll, ll, no, SSM_OCTET, SSM_GROUP_CH, SSM_GROUP_CH)
    toe = jnp.einsum('jtoghk,gf->ojgktfh', m, eye8).reshape(no, ll * LANES, ll * LANES)

    ein = jnp.stack([pw[0, ll - 1 - tt][..., None] * bbar[0][None],
                     pw[1, tt][..., None] * bbar[1][None]])
    ein = jnp.stack([jnp.real(ein), jnp.imag(ein)], axis=1)
    ein = ein.reshape(2, 2, ll, no, SSM_OCTET, SSM_STATE, SSM_GROUP_CH)
    sin_ = jnp.einsum('drjogpk,gf->ojgkdrfp', ein, eye8).reshape(no, ll * LANES, 4 * SSM_OCTET * SSM_STATE)

    eout = jnp.stack([cmat[0][None] * pw[0, tt + 1][:, :, None, :],
                      cmat[1][None] * pw[1, ll - tt][:, :, None, :]])
    eout = jnp.stack([jnp.real(eout), -jnp.imag(eout)], axis=1)
    eout = eout.reshape(2, 2, ll, no, SSM_OCTET, SSM_GROUP_CH, SSM_STATE)
    sout = jnp.einsum('drtoghp,gf->odrfptgh', eout, eye8).reshape(no, 4 * SSM_OCTET * SSM_STATE, ll * LANES)

    al = pw[:, ll]
    apow = jnp.stack([jnp.real(al), jnp.imag(al)], axis=1)
    apow = apow.reshape(2, 2, no, SSM_OCTET * SSM_STATE).transpose(2, 0, 1, 3)
    return toe.astype(BF16), sin_.astype(BF16), sout.astype(BF16), apow.astype(F32)


def _s5_local_kernel(u_ref, toe_ref, sin_ref, yi_ref, st_ref):
    u = u_ref[...]
    yi_ref[...] = jnp.dot(u, toe_ref[...], preferred_element_type=F32)
    st_ref[...] = jnp.dot(u, sin_ref[...], preferred_element_type=F32)


def _s5_scan_kernel(st_ref, ap_ref, x_ref):
    n = st_ref.shape[0]
    w = st_ref.shape[1] // 4
    ap = ap_ref[...]
    far, fai = ap[0, 0:1], ap[0, 1:2]
    bar, bai = ap[1, 0:1], ap[1, 1:2]
    zrow = jnp.zeros((1, 2 * w), F32)
    x_ref[pl.ds(0, 1), pl.ds(0, 2 * w)] = zrow
    x_ref[pl.ds(n - 1, 1), pl.ds(2 * w, 2 * w)] = zrow

    def step(c, carry):
        fr, fi, br, bi = carry
        sf = st_ref[pl.ds(c - 1, 1), pl.ds(0, 2 * w)]
        fr, fi = far * fr - fai * fi + sf[:, :w], far * fi + fai * fr + sf[:, w:]
        x_ref[pl.ds(c, 1), pl.ds(0, 2 * w)] = jnp.concatenate([fr, fi], axis=1)
        cb = n - 1 - c
        sb = st_ref[pl.ds(cb + 1, 1), pl.ds(2 * w, 2 * w)]
        br, bi = bar * br - bai * bi + sb[:, :w], bar * bi + bai * br + sb[:, w:]
        x_ref[pl.ds(cb, 1), pl.ds(2 * w, 2 * w)] = jnp.concatenate([br, bi], axis=1)
        return fr, fi, br, bi

    z = jnp.zeros((1, w), F32)
    lax.fori_loop(1, n, step, (z, z, z, z))


def _s5_out_kernel(yi_ref, x_ref, sout_ref, y_ref):
    y_ref[...] = yi_ref[...] + jnp.dot(x_ref[...].astype(BF16), sout_ref[...], preferred_element_type=F32)


def _s5(u8, toe, sin_, sout, apow, bsz, ncb):
    no, nc, cw = u8.shape
    sw = sin_.shape[2]
    ncseq = nc // bsz
    yi, st = pl.pallas_call(
        _s5_local_kernel,
        grid=(no, nc // ncb),
        in_specs=[
            pl.BlockSpec((None, ncb, cw), lambda o, i: (o, i, 0)),
            pl.BlockSpec((None, cw, cw), lambda o, i: (o, 0, 0)),
            pl.BlockSpec((None, cw, sw), lambda o, i: (o, 0, 0)),
        ],
        out_specs=[
            pl.BlockSpec((None, ncb, cw), lambda o, i: (o, i, 0)),
            pl.BlockSpec((None, ncb, sw), lambda o, i: (o, i, 0)),
        ],
        out_shape=[jax.ShapeDtypeStruct((no, nc, cw), F32), jax.ShapeDtypeStruct((no, nc, sw), F32)],
        compiler_params=_cparams(("arbitrary", "arbitrary"), 40),
        name="s5_local",
    )(u8, toe, sin_)
    xin = pl.pallas_call(
        _s5_scan_kernel,
        grid=(no, bsz),
        in_specs=[
            pl.BlockSpec((None, ncseq, sw), lambda o, b: (o, b, 0)),
            pl.BlockSpec((None, 2, 2, sw // 4), lambda o, b: (o, 0, 0, 0)),
        ],
        out_specs=pl.BlockSpec((None, ncseq, sw), lambda o, b: (o, b, 0)),
        out_shape=jax.ShapeDtypeStruct((no, nc, sw), F32),
        compiler_params=_cparams(("arbitrary", "arbitrary"), 48),
        name="s5_scan",
    )(st, apow)
    return pl.pallas_call(
        _s5_out_kernel,
        grid=(no, nc // ncb),
        in_specs=[
            pl.BlockSpec((None, ncb, cw), lambda o, i: (o, i, 0)),
            pl.BlockSpec((None, ncb, sw), lambda o, i: (o, i, 0)),
            pl.BlockSpec((None, sw, cw), lambda o, i: (o, 0, 0)),
        ],
        out_specs=pl.BlockSpec((None, ncb, cw), lambda o, i: (o, i, 0)),
        out_shape=jax.ShapeDtypeStruct((no, nc, cw), F32),
        compiler_params=_cparams(("arbitrary", "arbitrary"), 40),
        name="s5_out",
    )(yi, xin, sout)


def _merge_kernel(a_ref, y_ref, g0_ref, g1_ref, x_ref, wglu_ref, wb0_ref, wb1_ref, wout_ref, o_ref):
    y = y_ref[...]
    s = 0.5 * y * (1.0 + lax.erf(y * (2.0 ** -0.5)))
    glu = jnp.dot(s.astype(BF16), wglu_ref[...], preferred_element_type=F32)
    s = s * jax.nn.sigmoid(glu)
    pa = jnp.dot(a_ref[...], wb0_ref[...], preferred_element_type=F32)
    ps = jnp.dot(s.astype(BF16), wb1_ref[...], preferred_element_type=F32)
    merged = g0_ref[...].astype(F32) * pa + g1_ref[...].astype(F32) * ps
    o_ref[...] = x_ref[...] + jnp.dot(merged.astype(BF16), wout_ref[...], preferred_element_type=F32)


def _merge(a, y, proj, x, wglu, wb0, wb1, wout, bm):
    t, d = x.shape
    aw = a.shape[1]
    g0 = 4 * SEG // d
    full = lambda arr: pl.BlockSpec(arr.shape, lambda i: (0,) * arr.ndim)
    return pl.pallas_call(
        _merge_kernel,
        grid=(t // bm,),
        in_specs=[
            pl.BlockSpec((bm, aw), lambda i: (i, 0)),
            pl.BlockSpec((bm, aw), lambda i: (i, 0)),
            pl.BlockSpec((bm, d), lambda i: (i, g0)),
            pl.BlockSpec((bm, d), lambda i: (i, g0 + 1)),
            pl.BlockSpec((bm, d), lambda i: (i, 0)),
            full(wglu), full(wb0), full(wb1), full(wout),
        ],
        out_specs=pl.BlockSpec((bm, d), lambda i: (i, 0)),
        out_shape=jax.ShapeDtypeStruct((t, d), F32),
        compiler_params=_cparams(("arbitrary",), 48),
        name="merge",
    )(a, y, proj, proj, x, wglu, wb0, wb1, wout)


def _swiglu_partial(h, w1_ref, w3_ref, w2_ref, sub):
    ffc = w1_ref.shape[-1]
    acc = None
    for s in range(0, ffc, sub):
        e = min(s + sub, ffc)
        a = jnp.dot(h, w1_ref[:, s:e], preferred_element_type=F32)
        b = jnp.dot(h, w3_ref[:, s:e], preferred_element_type=F32)
        tt = (a * jax.nn.sigmoid(a) * b).astype(BF16)
        c = jnp.dot(tt, w2_ref[s:e, :], preferred_element_type=F32)
        acc = c if acc is None else acc + c
    return acc


def _ffn_kernel(x_ref, g_ref, w1_ref, w3_ref, w2_ref, o_ref, hn_ref, *, sub):
    f = pl.program_id(1)

    @pl.when(f == 0)
    def _():
        x = x_ref[...]
        hn_ref[...] = _rms(x, g_ref[...]).astype(BF16)
        o_ref[...] = x

    o_ref[...] += _swiglu_partial(hn_ref[...], w1_ref, w3_ref, w2_ref, sub)


def _ffn_tile(dff):
    return dff // 2 if (dff // 2) % LANES == 0 else dff


def _ffn(x, g, w1, w3, w2, bm):
    t, d = x.shape
    dff = w1.shape[1]
    ffc = _ffn_tile(dff)
    return pl.pallas_call(
        functools.partial(_ffn_kernel, sub=2 * LANES),
        grid=(t // bm, dff // ffc),
        in_specs=[
            pl.BlockSpec((bm, d), lambda i, f: (i, 0)),
            pl.BlockSpec((1, d), lambda i, f: (0, 0)),
            pl.BlockSpec((d, ffc), lambda i, f: (0, f)),
            pl.BlockSpec((d, ffc), lambda i, f: (0, f)),
            pl.BlockSpec((ffc, d), lambda i, f: (f, 0)),
        ],
        out_specs=pl.BlockSpec((bm, d), lambda i, f: (i, 0)),
        out_shape=jax.ShapeDtypeStruct((t, d), F32),
        scratch_shapes=[pltpu.VMEM((bm, d), BF16)],
        compiler_params=_cparams(("arbitrary", "arbitrary"), 56),
        name="ffn_dense",
    )(x, g, w1, w3, w2)


def _router_kernel(x_ref, g_ref, wr_ref, hn_ref, r_ref):
    hn = _rms(x_ref[...], g_ref[...])
    hn_ref[...] = hn
    logits = jnp.dot(hn, wr_ref[...], preferred_element_type=F32, precision=lax.Precision.HIGHEST)
    lane = lax.broadcasted_iota(I32, logits.shape, 1)
    neg = jnp.float32(-jnp.inf)
    logits = jnp.where(lane < N_EXPERTS, logits, neg)
    m1 = jnp.max(logits, axis=-1, keepdims=True)
    i1 = jnp.min(jnp.where(logits == m1, lane, LANES), axis=-1, keepdims=True)
    rest = jnp.where(lane == i1, neg, logits)
    m2 = jnp.max(rest, axis=-1, keepdims=True)
    i2 = jnp.min(jnp.where(rest == m2, lane, LANES), axis=-1, keepdims=True)
    e2 = jnp.exp(m2 - m1)
    g1 = 1.0 / (1.0 + e2)
    g2 = e2 / (1.0 + e2)
    r_ref[...] = jnp.where(lane == 0, i1.astype(F32),
                           jnp.where(lane == 1, i2.astype(F32),
                                     jnp.where(lane == 2, g1, jnp.where(lane == 3, g2, 0.0))))


def _router(x, g, wr_pad, bm):
    t, d = x.shape
    return pl.pallas_call(
        _router_kernel,
        grid=(t // bm,),
        in_specs=[
            pl.BlockSpec((bm, d), lambda i: (i, 0)),
            pl.BlockSpec((1, d), lambda i: (0, 0)),
            pl.BlockSpec((d, LANES), lambda i: (0, 0)),
        ],
        out_specs=[pl.BlockSpec((bm, d), lambda i: (i, 0)), pl.BlockSpec((bm, LANES), lambda i: (i, 0))],
        out_shape=[jax.ShapeDtypeStruct((t, d), F32), jax.ShapeDtypeStruct((t, LANES), F32)],
        compiler_params=_cparams(("arbitrary",), 40),
        name="moe_router",
    )(x, g, wr_pad)


def _row_copy(src_hbm, dst_ref, sem, src_row, dst_row):
    return pltpu.make_async_copy(src_hbm.at[pl.ds(src_row, 1)], dst_ref.at[pl.ds(dst_row, 1)], sem)


def _gather_kernel(idx_ref, src_hbm, o_ref, sem):
    bm = o_ref.shape[0]
    base = pl.program_id(0) * bm

    def issue(r, c):
        _row_copy(src_hbm, o_ref, sem, idx_ref[base + r], r).start()
        return c

    lax.fori_loop(0, bm, issue, 0)

    def drain(r, c):
        _row_copy(src_hbm, o_ref, sem, 0, r).wait()
        return c

    lax.fori_loop(0, bm, drain, 0)


def _gather_rows(idx, src, bm):
    n = idx.shape[0]
    d = src.shape[1]
    return pl.pallas_call(
        _gather_kernel,
        grid_spec=pltpu.PrefetchScalarGridSpec(
            num_scalar_prefetch=1,
            grid=(n // bm,),
            in_specs=[pl.BlockSpec(memory_space=pl.ANY)],
            out_specs=pl.BlockSpec((bm, d), lambda i, idx: (i, 0)),
            scratch_shapes=[pltpu.SemaphoreType.DMA(())],
        ),
        out_shape=jax.ShapeDtypeStruct((n, d), src.dtype),
        compiler_params=_cparams(("arbitrary",), 32),
        name="moe_gather",
    )(idx, src)


def _expert_kernel(be_ref, x_ref, w1_ref, w3_ref, w2_ref, o_ref, *, sub):
    f = pl.program_id(1)
    part = _swiglu_partial(x_ref[...].astype(BF16), w1_ref, w3_ref, w2_ref, sub)

    @pl.when(f == 0)
    def _():
        o_ref[...] = part

    @pl.when(f > 0)
    def _():
        o_ref[...] += part


def _experts(block_exp, xb, w1, w3, w2, bm):
    n, d = xb.shape
    dff = w1.shape[2]
    ffc = _ffn_tile(dff)
    return pl.pallas_call(
        functools.partial(_expert_kernel, sub=2 * LANES),
        grid_spec=pltpu.PrefetchScalarGridSpec(
            num_scalar_prefetch=1,
            grid=(n // bm, dff // ffc),
            in_specs=[
                pl.BlockSpec((bm, d), lambda i, f, be: (i, 0)),
                pl.BlockSpec((None, d, ffc), lambda i, f, be: (be[i], 0, f)),
                pl.BlockSpec((None, d, ffc), lambda i, f, be: (be[i], 0, f)),
                pl.BlockSpec((None, ffc, d), lambda i, f, be: (be[i], f, 0)),
            ],
            out_specs=pl.BlockSpec((bm, d), lambda i, f, be: (i, 0)),
        ),
        out_shape=jax.ShapeDtypeStruct((n, d), F32),
        compiler_params=_cparams(("arbitrary", "arbitrary"), 56),
        name="moe_experts",
    )(block_exp, xb, w1, w3, w2)


def _combine_kernel(d0_ref, d1_ref, yb_hbm, x_ref, r_ref, o_ref, r0_ref, r1_ref, sem):
    bm = o_ref.shape[0]
    base = pl.program_id(0) * bm

    def issue(r, c):
        _row_copy(yb_hbm, r0_ref, sem, d0_ref[base + r], r).start()
        _row_copy(yb_hbm, r1_ref, sem, d1_ref[base + r], r).start()
        return c

    lax.fori_loop(0, bm, issue, 0)

    def drain(r, c):
        _row_copy(yb_hbm, r0_ref, sem, 0, r).wait()
        _row_copy(yb_hbm, r1_ref, sem, 0, r).wait()
        return c

    lax.fori_loop(0, bm, drain, 0)
    route = r_ref[...]
    o_ref[...] = x_ref[...] + route[:, 2:3] * r0_ref[...] + route[:, 3:4] * r1_ref[...]


def _combine(d0, d1, yb, x, route, bm):
    t, d = x.shape
    return pl.pallas_call(
        _combine_kernel,
        grid_spec=pltpu.PrefetchScalarGridSpec(
            num_scalar_prefetch=2,
            grid=(t // bm,),
            in_specs=[
                pl.BlockSpec(memory_space=pl.ANY),
                pl.BlockSpec((bm, d), lambda i, a, b: (i, 0)),
                pl.BlockSpec((bm, LANES), lambda i, a, b: (i, 0)),
            ],
            out_specs=pl.BlockSpec((bm, d), lambda i, a, b: (i, 0)),
            scratch_shapes=[pltpu.VMEM((bm, d), F32), pltpu.VMEM((bm, d), F32), pltpu.SemaphoreType.DMA(())],
        ),
        out_shape=jax.ShapeDtypeStruct((t, d), F32),
        compiler_params=_cparams(("arbitrary",), 32),
        name="moe_combine",
    )(d0, d1, yb, x, route)


def _dispatch_tables(idx, bm):
    t = idx.shape[0]
    flat_e = idx.reshape(-1)
    onehot = (flat_e[:, None] == jnp.arange(N_EXPERTS, dtype=I32)[None, :]).astype(I32)
    csum = jnp.cumsum(onehot, axis=0)
    rank = jnp.take_along_axis(csum, flat_e[:, None], axis=1)[:, 0] - 1
    counts = csum[-1]
    padded = (counts + bm - 1) // bm * bm
    pad_end = jnp.cumsum(padded)
    pad_start = pad_end - padded
    dest = pad_start[flat_e] + rank
    n_slots = t * TOP_K + N_EXPERTS * bm
    slot_tok = jnp.zeros((n_slots,), I32).at[dest].set(jnp.arange(t * TOP_K, dtype=I32) // TOP_K)
    block_start = jnp.arange(n_slots // bm, dtype=I32) * bm
    block_exp = jnp.minimum(jnp.searchsorted(pad_end, block_start, side='right'), N_EXPERTS - 1).astype(I32)
    dest = dest.reshape(t, TOP_K).astype(I32)
    return dest[:, 0], dest[:, 1], slot_tok, block_exp


def _moe(x, g, w_router, w1, w3, w2, bm_tok, bm_slot):
    d = x.shape[1]
    wr_pad = jnp.zeros((d, LANES), F32).at[:, :N_EXPERTS].set(w_router)
    hn, route = _router(x, g, wr_pad, bm_tok)
    idx = route[:, :TOP_K].astype(I32)
    d0, d1, slot_tok, block_exp = _dispatch_tables(idx, bm_slot)
    xb = _gather_rows(slot_tok, hn, bm_slot)
    yb = _experts(block_exp, xb, w1, w3, w2, bm_slot)
    return _combine(d0, d1, yb, x, route, bm_tok)


def _norm_kernel(x_ref, g_ref, o_ref):
    o_ref[...] = _rms(x_ref[...], g_ref[...])


def _final_norm(x, g, bm):
    t, d = x.shape
    return pl.pallas_call(
        _norm_kernel,
        grid=(t // bm,),
        in_specs=[pl.BlockSpec((bm, d), lambda i: (i, 0)), pl.BlockSpec((1, d), lambda i: (0, 0))],
        out_specs=pl.BlockSpec((bm, d), lambda i: (i, 0)),
        out_shape=jax.ShapeDtypeStruct((t, d), F32),
        compiler_params=_cparams(("arbitrary",), 32),
        name="final_norm",
    )(x, g)


def _rope_tables(seq):
    half = ATTN_HEAD_DIM // 2
    inv_freq = 1.0 / (ROPE_THETA ** (jnp.arange(half, dtype=F32) / half))
    ang = jnp.arange(seq, dtype=F32)[:, None] * inv_freq[None, :]
    cos = jnp.tile(jnp.cos(ang), (1, 4))
    sin = jnp.sin(ang)
    return cos, jnp.concatenate([-sin, -sin, sin, sin], axis=1)


def _permute_qk_columns(w):
    d = w.shape[0]
    half = ATTN_HEAD_DIM // 2
    qk = w[:, :2 * SEG].reshape(d, 2, ATTN_HEADS, 2, 2, half).transpose(0, 1, 2, 4, 3, 5).reshape(d, 2 * SEG)
    return jnp.concatenate([qk, w[:, 2 * SEG:]], axis=1)


def _block(n, pref):
    return pref if n % pref == 0 else n


def kernel(x, norm_mix, w_in, attn_lambda, attn_subln, ssm_lam_re, ssm_lam_im, ssm_log_dt, ssm_b_re, ssm_b_im, ssm_c_re, ssm_c_im, ssm_d, w_glu, w_branch, w_out, norm_ffn, ffn_w1, ffn_w3, ffn_w2, moe_router, moe_w1, moe_w3, moe_w2, norm_final):
    bsz, seq, d = x.shape
    depth = w_in.shape[0]
    t = bsz * seq
    ll = SSM_CHUNK
    nc = t // ll
    no = ssm_lam_re.shape[2] // SSM_OCTET
    cos_t, sin_t = _rope_tables(seq)
    xf = x.reshape(t, d)
    bm = _block(seq, 1024)
    for layer in range(depth):
        w = _permute_qk_columns(w_in[layer]).astype(BF16)
        proj = _inproj(xf, norm_mix[layer][None], w, cos_t, sin_t, seq, bm)

        lam_init = jnp.full((1,), 0.8 - 0.6 * math.exp(-0.3 * layer), F32)
        a = _attention(proj, lam_init, attn_lambda[layer], attn_subln[layer][None], bsz, seq,
                       _block(seq, 512), _block(seq // 2, 512))

        toe, sin_, sout, apow = _s5_tables(ssm_lam_re[layer], ssm_lam_im[layer], ssm_log_dt[layer],
                                          ssm_b_re[layer], ssm_b_im[layer], ssm_c_re[layer],
                                          ssm_c_im[layer], ssm_d[layer])
        u = proj[:, 3 * SEG:4 * SEG]
        u8 = u.reshape(nc, ll, no, LANES).transpose(2, 0, 1, 3).reshape(no, nc, ll * LANES)
        y8 = _s5(u8, toe, sin_, sout, apow, bsz, _block(nc // bsz, 512))
        y = y8.reshape(no, nc, ll, LANES).transpose(1, 2, 0, 3).reshape(t, no * LANES)

        xf = _merge(a, y, proj, xf, w_glu[layer].astype(BF16), w_branch[layer, 0].astype(BF16),
                    w_branch[layer, 1].astype(BF16), w_out[layer].astype(BF16), _block(seq, 512))

        i = layer // 2
        if layer % 2 == 0:
            xf = _ffn(xf, norm_ffn[layer][None], ffn_w1[i].astype(BF16), ffn_w3[i].astype(BF16),
                      ffn_w2[i].astype(BF16), bm)
        else:
            xf = _moe(xf, norm_ffn[layer][None], moe_router[i], moe_w1[i].astype(BF16),
                      moe_w3[i].astype(BF16), moe_w2[i].astype(BF16), _block(seq, 256), 512)
    return _final_norm(xf, norm_final[None], bm).reshape(bsz, seq, d)
```

```python
import functools
import math

import jax
import jax.numpy as jnp
from jax import lax
from jax.experimental import pallas as pl
from jax.experimental.pallas import tpu as pltpu

F32 = jnp.float32
BF16 = jnp.bfloat16
I32 = jnp.int32

RMS_EPS = 1e-6
ROPE_THETA = 10000.0
LOG2E = 1.4426950408889634

LANES = 128
SUBLANES = 8
VMEM_BYTES_V7X = 64 * 1024 * 1024

ATTN_HEADS = 4
ATTN_HEAD_DIM = 64
HEAD_SLAB = 2 * ATTN_HEAD_DIM
SSM_GROUP_CH = 16
SSM_STATE = 64
SSM_OCTET = LANES // SSM_GROUP_CH
SSM_CHUNK = 8
N_EXPERTS = 8
TOP_K = 2
SEG = 512


def _cparams(sem, vmem_mb):
    return pltpu.CompilerParams(dimension_semantics=sem, vmem_limit_bytes=vmem_mb * 1024 * 1024)


def _rms(x, g):
    return x * lax.rsqrt(jnp.mean(x * x, axis=-1, keepdims=True) + RMS_EPS) * g


def _sigmoid(x):
    return 0.5 * jnp.tanh(0.5 * x) + 0.5


def _inproj_kernel(x_ref, g_ref, w_ref, cos_ref, sin_ref, o_ref, u_ref, *, qscale):
    xn = _rms(x_ref[...], g_ref[...]).astype(BF16)
    cos = cos_ref[...]
    sin = sin_ref[...]

    def rot(a):
        parts = []
        for s in range(0, a.shape[1], HEAD_SLAB):
            blk = a[:, s:s + HEAD_SLAB]
            parts.append(blk * cos + pltpu.roll(blk, HEAD_SLAB // 2, 1) * sin)
        return jnp.concatenate(parts, axis=1)

    for j in range(w_ref.shape[1] // SEG):
        acc = jnp.dot(xn, w_ref[:, j * SEG:(j + 1) * SEG], preferred_element_type=F32)
        if j == 0:
            acc = rot(acc) * qscale
        elif j == 1:
            acc = rot(acc)
        elif j == 3:
            u_ref[...] = acc
        elif j >= 4:
            acc = _sigmoid(acc)
        o_ref[:, j * SEG:(j + 1) * SEG] = acc.astype(BF16)


def _inproj(x, g, w, cos_t, sin_t, seq, bm):
    t, d = x.shape
    n = w.shape[1]
    qscale = ATTN_HEAD_DIM ** -0.5 * LOG2E
    nseq = seq // bm
    return pl.pallas_call(
        functools.partial(_inproj_kernel, qscale=qscale),
        grid=(t // bm,),
        in_specs=[
            pl.BlockSpec((bm, d), lambda i: (i, 0)),
            pl.BlockSpec((1, d), lambda i: (0, 0)),
            pl.BlockSpec((d, n), lambda i: (0, 0)),
            pl.BlockSpec((bm, HEAD_SLAB), lambda i: (i % nseq, 0)),
            pl.BlockSpec((bm, HEAD_SLAB), lambda i: (i % nseq, 0)),
        ],
        out_specs=[pl.BlockSpec((bm, n), lambda i: (i, 0)), pl.BlockSpec((bm, SEG), lambda i: (i, 0))],
        out_shape=[jax.ShapeDtypeStruct((t, n), BF16), jax.ShapeDtypeStruct((t, SEG), F32)],
        compiler_params=_cparams(("arbitrary",), 52),
        name="inproj",
    )(x, g, w, cos_t, sin_t)


ONES_ROWS = 16


def _attn_kernel(sc_ref, lp_ref, q_ref, k_ref, v_ref, g_ref, o_ref, vt_ref, acc_ref, m_ref,
                 s0_ref, s1_ref, p0_ref, p1_ref, x0_ref, x1_ref, *, bk):
    qi = pl.program_id(2)
    seq = k_ref.shape[0]
    bq = q_ref.shape[0]
    nkv = seq // bk
    assert nkv >= 2 and nkv % 2 == 0

    @pl.when(qi == 0)
    def _():
        for c in range(nkv):
            vt_ref[c, :HEAD_SLAB, :] = v_ref[c * bk:(c + 1) * bk, :].astype(F32).T.astype(BF16)
            vt_ref[c, HEAD_SLAB:, :] = jnp.ones((ONES_ROWS, bk), BF16)

    qt = q_ref[...].astype(F32).T
    row = lax.broadcasted_iota(I32, qt.shape, 0)
    half0 = (row % ATTN_HEAD_DIM) < (ATTN_HEAD_DIM // 2)
    qst = jnp.concatenate([jnp.where(half0, qt, 0.0), jnp.where(half0, 0.0, qt)], axis=1).astype(BF16)

    acc_ref[...] = jnp.zeros_like(acc_ref)
    m_ref[...] = jnp.full_like(m_ref, -1e30)
    s_refs = (s0_ref, s1_ref)
    p_refs = (p0_ref, p1_ref)
    x_refs = (x0_ref, x1_ref)

    def qk(j, cur):
        off = pl.multiple_of(j * bk, bk)
        st = jnp.dot(k_ref[pl.ds(off, bk), :], qst, preferred_element_type=F32)
        s_refs[cur][...] = st
        x_refs[cur][...] = jnp.max(st, axis=0, keepdims=True)

    def pv(j, cur, alpha):
        acc_ref[...] = acc_ref[...] * alpha + jnp.dot(vt_ref[j], p_refs[cur][...],
                                                      preferred_element_type=F32)

    def softmax(cur):
        m_prev = m_ref[...]
        m_new = jnp.maximum(m_prev, x_refs[cur][...])
        p_refs[cur][...] = jnp.exp2((s_refs[cur][...] - m_new).astype(BF16))
        m_ref[...] = m_new
        return jnp.exp2(m_prev - m_new)

    def stage(j, cur, alpha):
        qk(j + 1, 1 - cur)
        pv(j - 1, 1 - cur, alpha)
        return softmax(cur)

    qk(0, 0)
    qk(1, 1)
    alpha = softmax(0)

    def pair(jj, alpha):
        j = 1 + 2 * jj
        return stage(j + 1, 0, stage(j, 1, alpha))

    alpha = lax.fori_loop(0, (nkv - 2) // 2, pair, alpha)
    pv(nkv - 2, 0, alpha)
    alpha = softmax(1)
    pv(nkv - 1, 1, alpha)

    lam_init = sc_ref[0]
    lp = lp_ref[...]
    lam = (jnp.exp(jnp.sum(lp[0:1] * lp[1:2], axis=-1, keepdims=True))
           - jnp.exp(jnp.sum(lp[2:3] * lp[3:4], axis=-1, keepdims=True)) + lam_init)
    acc = acc_ref[...]
    o = acc[:HEAD_SLAB] / acc[HEAD_SLAB:HEAD_SLAB + 1]
    a = o[:, :bq] - lam * o[:, bq:]
    a = a * lax.rsqrt(jnp.mean(a * a, axis=0, keepdims=True) + RMS_EPS) * (1.0 - lam_init)
    o_ref[...] = (a.T * g_ref[...]).astype(BF16)


def _attention(proj, lam_init, lam_p, subln, bsz, seq, bq, bk):
    t = proj.shape[0]
    nq = seq // bq
    koff = SEG // HEAD_SLAB
    return pl.pallas_call(
        functools.partial(_attn_kernel, bk=bk),
        grid=(bsz, ATTN_HEADS, nq),
        in_specs=[
            pl.BlockSpec(memory_space=pltpu.SMEM),
            pl.BlockSpec((4, ATTN_HEAD_DIM), lambda b, h, i: (0, 0)),
            pl.BlockSpec((bq, HEAD_SLAB), lambda b, h, i: (b * nq + i, h)),
            pl.BlockSpec((seq, HEAD_SLAB), lambda b, h, i: (b, koff + h)),
            pl.BlockSpec((seq, HEAD_SLAB), lambda b, h, i: (b, 2 * koff + h)),
            pl.BlockSpec((1, HEAD_SLAB), lambda b, h, i: (0, 0)),
        ],
        out_specs=pl.BlockSpec((bq, HEAD_SLAB), lambda b, h, i: (b * nq + i, h)),
        out_shape=jax.ShapeDtypeStruct((t, ATTN_HEADS * HEAD_SLAB), BF16),
        scratch_shapes=[
            pltpu.VMEM((seq // bk, HEAD_SLAB + ONES_ROWS, bk), BF16),
            pltpu.VMEM((HEAD_SLAB + ONES_ROWS, 2 * bq), F32),
            pltpu.VMEM((1, 2 * bq), F32),
            pltpu.VMEM((bk, 2 * bq), F32),
            pltpu.VMEM((bk, 2 * bq), F32),
            pltpu.VMEM((bk, 2 * bq), BF16),
            pltpu.VMEM((bk, 2 * bq), BF16),
            pltpu.VMEM((1, 2 * bq), F32),
            pltpu.VMEM((1, 2 * bq), F32),
        ],
        compiler_params=_cparams(("arbitrary", "arbitrary", "arbitrary"), 40),
        name="diff_attention",
    )(lam_init, lam_p, proj, proj, proj, subln)


SSM_TILE = SSM_OCTET * SSM_GROUP_CH
SSM_CW = SSM_CHUNK * SSM_TILE
SSM_SW = 4 * SSM_OCTET * SSM_STATE
SSM_QPER = SSM_SW // LANES
SCAN_SEGS = SUBLANES


def _s5_tables(lam_re, lam_im, log_dt, b_re, b_im, c_re, c_im, d_skip):
    hp = lax.Precision.HIGHEST
    ll = SSM_CHUNK
    g = lam_re.shape[1]
    no = g // SSM_OCTET
    lam = lax.complex(lam_re, lam_im)
    ldt = lam * jnp.exp(log_dt)[..., None]
    a = jnp.exp(ldt)
    bbar = ((a - 1.0) / lam)[..., None] * lax.complex(b_re, b_im)
    cmat = lax.complex(c_re, c_im)
    n = jnp.arange(ll + 1, dtype=F32)
    pw = jnp.exp(ldt[:, None] * n[None, :, None, None])

    def kern(d):
        return jnp.real(jnp.einsum('ghp,dgp,gpk->dghk', cmat[d], pw[d, :ll], bbar[d], precision=hp))

    kf, kb = kern(0), kern(1)
    tt = jnp.arange(ll)
    lag = tt[None, :] - tt[:, None]
    mf = jnp.where((lag >= 0)[:, :, None, None, None], kf[jnp.clip(lag, 0, ll - 1)], 0.0)
    mb = jnp.where((lag <= 0)[:, :, None, None, None], kb[jnp.clip(-lag, 0, ll - 1)], 0.0)
    skip = (jnp.eye(ll, dtype=F32)[:, :, None, None, None]
            * (d_skip.reshape(g, SSM_GROUP_CH)[None, None, :, :, None] * jnp.eye(SSM_GROUP_CH, dtype=F32)[None, None, None]))
    m = (mf + mb + skip).reshape(ll, ll, no, SSM_OCTET, SSM_GROUP_CH, SSM_GROUP_CH)
    a2 = m.transpose(2, 0, 3, 5, 1, 4).reshape(no, SSM_CW, ll * SSM_GROUP_CH)

    ein = jnp.stack([pw[0, ll - 1 - tt][..., None] * bbar[0][None],
                     pw[1, tt][..., None] * bbar[1][None]])
    ein = jnp.stack([jnp.real(ein), jnp.imag(ein)], axis=1)
    ein = ein.reshape(2, 2, ll, no, SSM_OCTET, SSM_STATE, SSM_GROUP_CH)
    b2 = ein.transpose(3, 2, 4, 6, 0, 1, 5).reshape(no, SSM_CW, 4 * SSM_STATE)

    eout = jnp.stack([cmat[0][None] * pw[0, tt + 1][:, :, None, :],
                      cmat[1][None] * pw[1, ll - tt][:, :, None, :]])
    eout = jnp.stack([jnp.real(eout), -jnp.imag(eout)], axis=1)
    eout = eout.reshape(2, 2, ll, no, SSM_OCTET, SSM_GROUP_CH, SSM_STATE)
    c2 = eout.transpose(3, 0, 1, 4, 6, 2, 5).reshape(no, SSM_SW, ll * SSM_GROUP_CH)

    al = pw[:, ll]
    apow = jnp.stack([jnp.real(al), jnp.imag(al)], axis=1)
    apow = apow.reshape(2, 2, no, SSM_OCTET * SSM_STATE).transpose(2, 0, 1, 3)
    return a2.astype(BF16), b2.astype(BF16), c2.astype(BF16), apow.astype(F32)


def _to_row_tiles(ref, x):
    n = x.shape[0]
    s_per = x.shape[1] // LANES
    for s in range(s_per):
        ref[pl.ds(s, n, stride=s_per), :] = x[:, s * LANES:(s + 1) * LANES]


def _from_row_tiles(ref, s_per):
    n = ref.shape[0] // s_per
    return jnp.concatenate([ref[pl.ds(s, n, stride=s_per), :] for s in range(s_per)], axis=1)


def _shr(idx, n):
    assert n & (n - 1) == 0
    return idx >> (n.bit_length() - 1)


def _group_of(idx, width):
    return _shr(idx, width) & (SSM_OCTET - 1)


def _spread(compact, rep_rows, rep_cols, inner, row_w, col_w):
    q = lax.broadcasted_iota(I32, (rep_rows, rep_cols), 0)
    c = lax.broadcasted_iota(I32, (rep_rows, rep_cols), 1)
    rep = ((_shr(q, inner) == _shr(c, inner * SSM_OCTET)) & ((q & (inner - 1)) == (c & (inner - 1)))).astype(BF16)
    full = jnp.dot(compact, rep, preferred_element_type=F32)
    r = lax.broadcasted_iota(I32, full.shape, 0)
    cc = lax.broadcasted_iota(I32, full.shape, 1)
    return jnp.where(_group_of(r, row_w) == _group_of(cc, col_w), full, 0.0).astype(BF16)


def _seg_rows(q, k, seglen):
    return pl.ds(q * SCAN_SEGS + k, seglen, stride=SSM_QPER * SCAN_SEGS)


def _s5_local_kernel(u_ref, a2_ref, b2_ref, yi_ref, st_ref, toe_ref, sin_ref, *, parts):
    ncb = yi_ref.shape[0]
    part = pl.program_id(2)
    segs = SCAN_SEGS // parts
    seglen = ncb // segs

    @pl.when((pl.program_id(1) == 0) & (part == 0))
    def _():
        toe_ref[...] = _spread(a2_ref[...], a2_ref.shape[1], SSM_CW, SSM_GROUP_CH, SSM_GROUP_CH, SSM_GROUP_CH)
        sin_ref[...] = _spread(b2_ref[...], b2_ref.shape[1], SSM_SW, SSM_STATE, SSM_GROUP_CH, SSM_STATE)

    u = jnp.concatenate([u_ref[pl.ds(j, ncb, stride=SSM_CHUNK), :] for j in range(SSM_CHUNK)],
                        axis=1).astype(BF16)
    yi_ref[...] = jnp.dot(u, toe_ref[...], preferred_element_type=F32)
    st = jnp.dot(u, sin_ref[...], preferred_element_type=F32)
    for pv in range(parts):
        @pl.when(part == pv)
        def _():
            for kk in range(segs):
                for q in range(SSM_QPER):
                    st_ref[_seg_rows(q, pv * segs + kk, seglen), :] = (
                        st[kk * seglen:(kk + 1) * seglen, q * LANES:(q + 1) * LANES])


def _cmul(ar, ai, xr, xi):
    return ar * xr - ai * xi, ar * xi + ai * xr


def _s5_scan_kernel(st_ref, ap_ref, x_ref):
    wt = SSM_QPER // 4
    seglen = st_ref.shape[0] // SSM_QPER
    assert seglen & (seglen - 1) == 0
    ap = ap_ref[...]

    def coef(d, r):
        return [jnp.broadcast_to(ap[d, r:r + 1, t * LANES:(t + 1) * LANES], (SCAN_SEGS, LANES)) for t in range(wt)]

    far, fai, bar, bai = coef(0, 0), coef(0, 1), coef(1, 0), coef(1, 1)

    def scan(init, store):
        def step(i, carry):
            fr, fi, br, bi = [list(c) for c in carry]
            rf = i * SSM_QPER
            rb = (seglen - 1 - i) * SSM_QPER
            for t in range(wt):
                if store:
                    x_ref[rf + t] = fr[t]
                    x_ref[rf + wt + t] = fi[t]
                    x_ref[rb + 2 * wt + t] = br[t]
                    x_ref[rb + 3 * wt + t] = bi[t]
                pr, pi = _cmul(far[t], fai[t], fr[t], fi[t])
                fr[t] = pr + st_ref[rf + t]
                fi[t] = pi + st_ref[rf + wt + t]
                pr, pi = _cmul(bar[t], bai[t], br[t], bi[t])
                br[t] = pr + st_ref[rb + 2 * wt + t]
                bi[t] = pi + st_ref[rb + 3 * wt + t]
            return tuple(fr), tuple(fi), tuple(br), tuple(bi)

        return lax.fori_loop(0, seglen, step, init)

    z = tuple(jnp.zeros((SCAN_SEGS, LANES), F32) for _ in range(wt))
    fr, fi, br, bi = scan((z, z, z, z), store=False)

    z1 = jnp.zeros((1, LANES), F32)
    fcr, fci, bcr, bci = [], [], [], []
    for t in range(wt):
        fsr, fsi, bsr, bsi = far[t][0:1], fai[t][0:1], bar[t][0:1], bai[t][0:1]
        for _ in range(seglen.bit_length() - 1):
            fsr, fsi = _cmul(fsr, fsi, fsr, fsi)
            bsr, bsi = _cmul(bsr, bsi, bsr, bsi)
        cr, ci = z1, z1
        rs, is_ = [], []
        for k in range(SCAN_SEGS):
            rs.append(cr)
            is_.append(ci)
            cr, ci = _cmul(fsr, fsi, cr, ci)
            cr, ci = cr + fr[t][k:k + 1], ci + fi[t][k:k + 1]
        fcr.append(jnp.concatenate(rs, axis=0))
        fci.append(jnp.concatenate(is_, axis=0))
        cr, ci = z1, z1
        rs, is_ = [None] * SCAN_SEGS, [None] * SCAN_SEGS
        for k in reversed(range(SCAN_SEGS)):
            rs[k] = cr
            is_[k] = ci
            cr, ci = _cmul(bsr, bsi, cr, ci)
            cr, ci = cr + br[t][k:k + 1], ci + bi[t][k:k + 1]
        bcr.append(jnp.concatenate(rs, axis=0))
        bci.append(jnp.concatenate(is_, axis=0))

    scan((tuple(fcr), tuple(fci), tuple(bcr), tuple(bci)), store=True)


def _s5_out_kernel(yi_ref, x_ref, c2_ref, y_ref, sout_ref, *, parts):
    ncb = yi_ref.shape[0]
    part = pl.program_id(2)
    segs = SCAN_SEGS // parts
    seglen = ncb // segs

    @pl.when((pl.program_id(1) == 0) & (part == 0))
    def _():
        sout_ref[...] = _spread(c2_ref[...], c2_ref.shape[1], SSM_CW, SSM_GROUP_CH, SSM_STATE, SSM_GROUP_CH)

    for pv in range(parts):
        @pl.when(part == pv)
        def _():
            x = jnp.concatenate(
                [jnp.concatenate([x_ref[_seg_rows(q, pv * segs + kk, seglen), :] for q in range(SSM_QPER)], axis=1)
                 for kk in range(segs)], axis=0).astype(BF16)
            y = yi_ref[...] + jnp.dot(x, sout_ref[...], preferred_element_type=F32)
            for t in range(SSM_CHUNK):
                y_ref[pl.ds(t, ncb, stride=SSM_CHUNK), :] = y[:, t * SSM_TILE:(t + 1) * SSM_TILE]


def _s5(u, a2, b2, c2, apow, bsz, ncb):
    t, width = u.shape
    no = width // SSM_TILE
    nc = t // SSM_CHUNK
    ncseq = nc // bsz
    parts = ncseq // ncb
    assert SCAN_SEGS % parts == 0
    rows = ncb * SSM_CHUNK
    slab = ncseq * SSM_QPER
    yi, st = pl.pallas_call(
        functools.partial(_s5_local_kernel, parts=parts),
        grid=(no, bsz, parts),
        in_specs=[
            pl.BlockSpec((rows, SSM_TILE), lambda o, b, p: (b * parts + p, o)),
            pl.BlockSpec((None,) + a2.shape[1:], lambda o, b, p: (o, 0, 0)),
            pl.BlockSpec((None,) + b2.shape[1:], lambda o, b, p: (o, 0, 0)),
        ],
        out_specs=[
            pl.BlockSpec((None, ncb, SSM_CW), lambda o, b, p: (o, b * parts + p, 0)),
            pl.BlockSpec((None, None, slab, LANES), lambda o, b, p: (o, b, 0, 0)),
        ],
        out_shape=[jax.ShapeDtypeStruct((no, nc, SSM_CW), F32),
                   jax.ShapeDtypeStruct((no, bsz, slab, LANES), F32)],
        scratch_shapes=[pltpu.VMEM((SSM_CW, SSM_CW), BF16), pltpu.VMEM((SSM_CW, SSM_SW), BF16)],
        compiler_params=_cparams(("arbitrary", "arbitrary", "arbitrary"), 52),
        name="s5_local",
    )(u, a2, b2)
    tiles = slab // SCAN_SEGS
    xin = pl.pallas_call(
        _s5_scan_kernel,
        grid=(no, bsz),
        in_specs=[
            pl.BlockSpec((None, None, tiles, SCAN_SEGS, LANES), lambda o, b: (o, b, 0, 0, 0)),
            pl.BlockSpec((None, 2, 2, SSM_SW // 4), lambda o, b: (o, 0, 0, 0)),
        ],
        out_specs=pl.BlockSpec((None, None, tiles, SCAN_SEGS, LANES), lambda o, b: (o, b, 0, 0, 0)),
        out_shape=jax.ShapeDtypeStruct((no, bsz, tiles, SCAN_SEGS, LANES), F32),
        compiler_params=_cparams(("arbitrary", "arbitrary"), 48),
        name="s5_scan",
    )(st.reshape(no, bsz, tiles, SCAN_SEGS, LANES), apow)
    return pl.pallas_call(
        functools.partial(_s5_out_kernel, parts=parts),
        grid=(no, bsz, parts),
        in_specs=[
            pl.BlockSpec((None, ncb, SSM_CW), lambda o, b, p: (o, b * parts + p, 0)),
            pl.BlockSpec((None, None, slab, LANES), lambda o, b, p: (o, b, 0, 0)),
            pl.BlockSpec((None,) + c2.shape[1:], lambda o, b, p: (o, 0, 0)),
        ],
        out_specs=pl.BlockSpec((rows, SSM_TILE), lambda o, b, p: (b * parts + p, o)),
        out_shape=jax.ShapeDtypeStruct((t, width), F32),
        scratch_shapes=[pltpu.VMEM((SSM_SW, SSM_CW), BF16)],
        compiler_params=_cparams(("arbitrary", "arbitrary", "arbitrary"), 52),
        name="s5_out",
    )(yi, xin.reshape(no, bsz, slab, LANES), c2)


def _merge_kernel(a_ref, y_ref, g0_ref, g1_ref, x_ref, wglu_ref, wb0_ref, wb1_ref, wout_ref, o_ref):
    y = y_ref[...]
    s = 0.5 * y * (1.0 + lax.erf(y * (2.0 ** -0.5)))
    glu = jnp.dot(s.astype(BF16), wglu_ref[...], preferred_element_type=F32)
    s = s * jax.nn.sigmoid(glu)
    pa = jnp.dot(a_ref[...], wb0_ref[...], preferred_element_type=F32)
    ps = jnp.dot(s.astype(BF16), wb1_ref[...], preferred_element_type=F32)
    merged = g0_ref[...].astype(F32) * pa + g1_ref[...].astype(F32) * ps
    o_ref[...] = x_ref[...] + jnp.dot(merged.astype(BF16), wout_ref[...], preferred_element_type=F32)


def _merge(a, y, proj, x, wglu, wb0, wb1, wout, bm):
    t, d = x.shape
    aw = a.shape[1]
    g0 = 4 * SEG // d
    full = lambda arr: pl.BlockSpec(arr.shape, lambda i: (0,) * arr.ndim)
    return pl.pallas_call(
        _merge_kernel,
        grid=(t // bm,),
        in_specs=[
            pl.BlockSpec((bm, aw), lambda i: (i, 0)),
            pl.BlockSpec((bm, aw), lambda i: (i, 0)),
            pl.BlockSpec((bm, d), lambda i: (i, g0)),
            pl.BlockSpec((bm, d), lambda i: (i, g0 + 1)),
            pl.BlockSpec((bm, d), lambda i: (i, 0)),
            full(wglu), full(wb0), full(wb1), full(wout),
        ],
        out_specs=pl.BlockSpec((bm, d), lambda i: (i, 0)),
        out_shape=jax.ShapeDtypeStruct((t, d), F32),
        compiler_params=_cparams(("arbitrary",), 48),
        name="merge",
    )(a, y, proj, proj, x, wglu, wb0, wb1, wout)


def _swiglu_partial(h, w1_ref, w3_ref, w2_ref, sub):
    ffc = w1_ref.shape[-1]
    acc = None
    for s in range(0, ffc, sub):
        e = min(s + sub, ffc)
        a = jnp.dot(h, w1_ref[:, s:e], preferred_element_type=F32)
        b = jnp.dot(h, w3_ref[:, s:e], preferred_element_type=F32)
        tt = (a * jax.nn.sigmoid(a) * b).astype(BF16)
        c = jnp.dot(tt, w2_ref[s:e, :], preferred_element_type=F32)
        acc = c if acc is None else acc + c
    return acc


def _ffn_kernel(x_ref, g_ref, w1_ref, w3_ref, w2_ref, o_ref, hn_ref, *, sub):
    f = pl.program_id(1)

    @pl.when(f == 0)
    def _():
        x = x_ref[...]
        hn_ref[...] = _rms(x, g_ref[...]).astype(BF16)
        o_ref[...] = x

    o_ref[...] += _swiglu_partial(hn_ref[...], w1_ref, w3_ref, w2_ref, sub)


def _ffn_tile(dff):
    return dff // 2 if (dff // 2) % LANES == 0 else dff


def _ffn(x, g, w1, w3, w2, bm):
    t, d = x.shape
    dff = w1.shape[1]
    ffc = _ffn_tile(dff)
    return pl.pallas_call(
        functools.partial(_ffn_kernel, sub=2 * LANES),
        grid=(t // bm, dff // ffc),
        in_specs=[
            pl.BlockSpec((bm, d), lambda i, f: (i, 0)),
            pl.BlockSpec((1, d), lambda i, f: (0, 0)),
            pl.BlockSpec((d, ffc), lambda i, f: (0, f)),
            pl.BlockSpec((d, ffc), lambda i, f: (0, f)),
            pl.BlockSpec((ffc, d), lambda i, f: (f, 0)),
        ],
        out_specs=pl.BlockSpec((bm, d), lambda i, f: (i, 0)),
        out_shape=jax.ShapeDtypeStruct((t, d), F32),
        scratch_shapes=[pltpu.VMEM((bm, d), BF16)],
        compiler_params=_cparams(("arbitrary", "arbitrary"), 56),
        name="ffn_dense",
    )(x, g, w1, w3, w2)


def _router_kernel(x_ref, g_ref, wr_ref, hn_ref, r_ref):
    hn = _rms(x_ref[...], g_ref[...])
    _to_row_tiles(hn_ref, hn)
    logits = jnp.dot(hn, wr_ref[...], preferred_element_type=F32, precision=lax.Precision.HIGHEST)
    lane = lax.broadcasted_iota(I32, logits.shape, 1)
    neg = jnp.float32(-1e30)
    logits = jnp.where(lane < N_EXPERTS, logits, neg)
    m1 = jnp.max(logits, axis=-1, keepdims=True)
    i1 = jnp.min(jnp.where(logits == m1, lane, LANES), axis=-1, keepdims=True)
    rest = jnp.where(lane == i1, neg, logits)
    m2 = jnp.max(rest, axis=-1, keepdims=True)
    i2 = jnp.min(jnp.where(rest == m2, lane, LANES), axis=-1, keepdims=True)
    e2 = jnp.exp(m2 - m1)
    g1 = 1.0 / (1.0 + e2)
    g2 = e2 / (1.0 + e2)
    r_ref[...] = jnp.where(lane == 0, i1.astype(F32),
                           jnp.where(lane == 1, i2.astype(F32),
                                     jnp.where(lane == 2, g1, jnp.where(lane == 3, g2, 0.0))))


def _router(x, g, wr_pad, bm):
    t, d = x.shape
    s_per = d // LANES
    return pl.pallas_call(
        _router_kernel,
        grid=(t // bm,),
        in_specs=[
            pl.BlockSpec((bm, d), lambda i: (i, 0)),
            pl.BlockSpec((1, d), lambda i: (0, 0)),
            pl.BlockSpec((d, LANES), lambda i: (0, 0)),
        ],
        out_specs=[pl.BlockSpec((bm * s_per, LANES), lambda i: (i, 0)), pl.BlockSpec((bm, LANES), lambda i: (i, 0))],
        out_shape=[jax.ShapeDtypeStruct((t * s_per, LANES), F32), jax.ShapeDtypeStruct((t, LANES), F32)],
        compiler_params=_cparams(("arbitrary",), 40),
        name="moe_router",
    )(x, g, wr_pad)


def _row_copy(src_hbm, dst_ref, sem, src_row, dst_row, s_per):
    src = pl.multiple_of(src_row * s_per, s_per)
    dst = pl.multiple_of(dst_row * s_per, s_per)
    return pltpu.make_async_copy(src_hbm.at[pl.ds(src, s_per)], dst_ref.at[pl.ds(dst, s_per)], sem)


def _gather_kernel(idx_ref, src_hbm, o_ref, sem, *, s_per):
    bm = o_ref.shape[0] // s_per
    base = pl.program_id(0) * bm

    def issue(r, c):
        _row_copy(src_hbm, o_ref, sem, idx_ref[base + r], r, s_per).start()
        return c

    lax.fori_loop(0, bm, issue, 0, unroll=8)

    def drain(r, c):
        _row_copy(src_hbm, o_ref, sem, 0, r, s_per).wait()
        return c

    lax.fori_loop(0, bm, drain, 0, unroll=8)


def _gather_rows(idx, src, bm, s_per):
    n = idx.shape[0]
    return pl.pallas_call(
        functools.partial(_gather_kernel, s_per=s_per),
        grid_spec=pltpu.PrefetchScalarGridSpec(
            num_scalar_prefetch=1,
            grid=(n // bm,),
            in_specs=[pl.BlockSpec(memory_space=pl.ANY)],
            out_specs=pl.BlockSpec((bm * s_per, LANES), lambda i, idx: (i, 0)),
            scratch_shapes=[pltpu.SemaphoreType.DMA(())],
        ),
        out_shape=jax.ShapeDtypeStruct((n * s_per, LANES), src.dtype),
        compiler_params=_cparams(("arbitrary",), 32),
        name="moe_gather",
    )(idx, src)


def _expert_kernel(be_ref, x_ref, w1_ref, w3_ref, w2_ref, o_ref, xb_ref, acc_ref, *, sub, s_per):
    f = pl.program_id(1)

    @pl.when(f == 0)
    def _():
        xb_ref[...] = _from_row_tiles(x_ref, s_per).astype(BF16)

    part = _swiglu_partial(xb_ref[...], w1_ref, w3_ref, w2_ref, sub)

    @pl.when(f == 0)
    def _():
        acc_ref[...] = part

    @pl.when((f > 0) & (f < pl.num_programs(1) - 1))
    def _():
        acc_ref[...] += part

    @pl.when(f == pl.num_programs(1) - 1)
    def _():
        _to_row_tiles(o_ref, acc_ref[...] + part)


def _experts(block_exp, xb, w1, w3, w2, bm):
    d = w1.shape[1]
    dff = w1.shape[2]
    s_per = d // LANES
    n = xb.shape[0] // s_per
    ffc = _ffn_tile(dff)
    assert dff // ffc >= 2
    return pl.pallas_call(
        functools.partial(_expert_kernel, sub=2 * LANES, s_per=s_per),
        grid_spec=pltpu.PrefetchScalarGridSpec(
            num_scalar_prefetch=1,
            grid=(n // bm, dff // ffc),
            in_specs=[
                pl.BlockSpec((bm * s_per, LANES), lambda i, f, be: (i, 0)),
                pl.BlockSpec((None, d, ffc), lambda i, f, be: (be[i], 0, f)),
                pl.BlockSpec((None, d, ffc), lambda i, f, be: (be[i], 0, f)),
                pl.BlockSpec((None, ffc, d), lambda i, f, be: (be[i], f, 0)),
            ],
            out_specs=pl.BlockSpec((bm * s_per, LANES), lambda i, f, be: (i, 0)),
            scratch_shapes=[pltpu.VMEM((bm, d), BF16), pltpu.VMEM((bm, d), F32)],
        ),
        out_shape=jax.ShapeDtypeStruct((n * s_per, LANES), F32),
        compiler_params=_cparams(("arbitrary", "arbitrary"), 56),
        name="moe_experts",
    )(block_exp, xb, w1, w3, w2)


def _combine_kernel(d0_ref, d1_ref, yb_hbm, x_ref, r_ref, o_ref, r0_ref, r1_ref, sem, *, s_per):
    bm = o_ref.shape[0]
    base = pl.program_id(0) * bm

    def issue(r, c):
        _row_copy(yb_hbm, r0_ref, sem, d0_ref[base + r], r, s_per).start()
        _row_copy(yb_hbm, r1_ref, sem, d1_ref[base + r], r, s_per).start()
        return c

    lax.fori_loop(0, bm, issue, 0, unroll=8)

    def drain(r, c):
        _row_copy(yb_hbm, r0_ref, sem, 0, r, s_per).wait()
        _row_copy(yb_hbm, r1_ref, sem, 0, r, s_per).wait()
        return c

    lax.fori_loop(0, bm, drain, 0, unroll=8)
    route = r_ref[...]
    o_ref[...] = (x_ref[...] + route[:, 2:3] * _from_row_tiles(r0_ref, s_per)
                  + route[:, 3:4] * _from_row_tiles(r1_ref, s_per))


def _combine(d0, d1, yb, x, route, bm):
    t, d = x.shape
    s_per = d // LANES
    return pl.pallas_call(
        functools.partial(_combine_kernel, s_per=s_per),
        grid_spec=pltpu.PrefetchScalarGridSpec(
            num_scalar_prefetch=2,
            grid=(t // bm,),
            in_specs=[
                pl.BlockSpec(memory_space=pl.ANY),
                pl.BlockSpec((bm, d), lambda i, a, b: (i, 0)),
                pl.BlockSpec((bm, LANES), lambda i, a, b: (i, 0)),
            ],
            out_specs=pl.BlockSpec((bm, d), lambda i, a, b: (i, 0)),
            scratch_shapes=[pltpu.VMEM((bm * s_per, LANES), F32), pltpu.VMEM((bm * s_per, LANES), F32),
                            pltpu.SemaphoreType.DMA(())],
        ),
        out_shape=jax.ShapeDtypeStruct((t, d), F32),
        compiler_params=_cparams(("arbitrary",), 32),
        name="moe_combine",
    )(d0, d1, yb, x, route)


def _dispatch_tables(idx, bm):
    t = idx.shape[0]
    flat_e = idx.reshape(-1)
    onehot = (flat_e[:, None] == jnp.arange(N_EXPERTS, dtype=I32)[None, :]).astype(I32)
    csum = jnp.cumsum(onehot, axis=0)
    rank = jnp.take_along_axis(csum, flat_e[:, None], axis=1)[:, 0] - 1
    counts = csum[-1]
    padded = (counts + bm - 1) // bm * bm
    pad_end = jnp.cumsum(padded)
    pad_start = pad_end - padded
    dest = pad_start[flat_e] + rank
    n_slots = t * TOP_K + N_EXPERTS * bm
    slot_tok = jnp.zeros((n_slots,), I32).at[dest].set(jnp.arange(t * TOP_K, dtype=I32) // TOP_K)
    block_start = jnp.arange(n_slots // bm, dtype=I32) * bm
    block_exp = jnp.minimum(jnp.searchsorted(pad_end, block_start, side='right'), N_EXPERTS - 1).astype(I32)
    dest = dest.reshape(t, TOP_K).astype(I32)
    return dest[:, 0], dest[:, 1], slot_tok, block_exp


def _moe(x, g, w_router, w1, w3, w2, bm_tok, bm_slot):
    d = x.shape[1]
    wr_pad = jnp.zeros((d, LANES), F32).at[:, :N_EXPERTS].set(w_router)
    hn, route = _router(x, g, wr_pad, bm_tok)
    idx = route[:, :TOP_K].astype(I32)
    d0, d1, slot_tok, block_exp = _dispatch_tables(idx, bm_slot)
    xb = _gather_rows(slot_tok, hn, bm_slot, d // LANES)
    yb = _experts(block_exp, xb, w1, w3, w2, bm_slot)
    return _combine(d0, d1, yb, x, route, bm_tok)


def _norm_kernel(x_ref, g_ref, o_ref):
    o_ref[...] = _rms(x_ref[...], g_ref[...])


def _final_norm(x, g, bm):
    t, d = x.shape
    return pl.pallas_call(
        _norm_kernel,
        grid=(t // bm,),
        in_specs=[pl.BlockSpec((bm, d), lambda i: (i, 0)), pl.BlockSpec((1, d), lambda i: (0, 0))],
        out_specs=pl.BlockSpec((bm, d), lambda i: (i, 0)),
        out_shape=jax.ShapeDtypeStruct((t, d), F32),
        compiler_params=_cparams(("arbitrary",), 32),
        name="final_norm",
    )(x, g)


def _rope_tables(seq):
    half = ATTN_HEAD_DIM // 2
    inv_freq = 1.0 / (ROPE_THETA ** (jnp.arange(half, dtype=F32) / half))
    ang = jnp.arange(seq, dtype=F32)[:, None] * inv_freq[None, :]
    cos = jnp.tile(jnp.cos(ang), (1, 4))
    sin = jnp.sin(ang)
    return cos, jnp.concatenate([-sin, -sin, sin, sin], axis=1)


def _permute_qk_columns(w):
    d = w.shape[0]
    half = ATTN_HEAD_DIM // 2
    qk = w[:, :2 * SEG].reshape(d, 2, ATTN_HEADS, 2, 2, half).transpose(0, 1, 2, 4, 3, 5).reshape(d, 2 * SEG)
    return jnp.concatenate([qk, w[:, 2 * SEG:]], axis=1)


def _block(n, pref):
    return pref if n % pref == 0 else n


def kernel(x, norm_mix, w_in, attn_lambda, attn_subln, ssm_lam_re, ssm_lam_im, ssm_log_dt, ssm_b_re, ssm_b_im, ssm_c_re, ssm_c_im, ssm_d, w_glu, w_branch, w_out, norm_ffn, ffn_w1, ffn_w3, ffn_w2, moe_router, moe_w1, moe_w3, moe_w2, norm_final):
    bsz, seq, d = x.shape
    depth = w_in.shape[0]
    t = bsz * seq
    cos_t, sin_t = _rope_tables(seq)
    xf = x.reshape(t, d)
    bm = _block(seq, 1024)
    for layer in range(depth):
        w = _permute_qk_columns(w_in[layer]).astype(BF16)
        proj, u = _inproj(xf, norm_mix[layer][None], w, cos_t, sin_t, seq, _block(seq, 512))

        lam_init = jnp.full((1,), 0.8 - 0.6 * math.exp(-0.3 * layer), F32)
        a = _attention(proj, lam_init, attn_lambda[layer], attn_subln[layer][None], bsz, seq,
                       _block(seq, 512), _block(seq // 2, 512))

        a2, b2, c2, apow = _s5_tables(ssm_lam_re[layer], ssm_lam_im[layer], ssm_log_dt[layer],
                                      ssm_b_re[layer], ssm_b_im[layer], ssm_c_re[layer],
                                      ssm_c_im[layer], ssm_d[layer])
        y = _s5(u, a2, b2, c2, apow, bsz, _block(seq // SSM_CHUNK, 512))

        xf = _merge(a, y, proj, xf, w_glu[layer].astype(BF16), w_branch[layer, 0].astype(BF16),
                    w_branch[layer, 1].astype(BF16), w_out[layer].astype(BF16), _block(seq, 512))

        i = layer // 2
        if layer % 2 == 0:
            xf = _ffn(xf, norm_ffn[layer][None], ffn_w1[i].astype(BF16), ffn_w3[i].astype(BF16),
                      ffn_w2[i].astype(BF16), bm)
        else:
            xf = _moe(xf, norm_ffn[layer][None], moe_router[i], moe_w1[i].astype(BF16),
                      moe_w3[i].astype(BF16), moe_w2[i].astype(BF16), _block(seq, 256), 512)
    return _final_norm(xf, norm_final[None], bm).reshape(bsz, seq, d)
```

```python
import functools
import math

import jax
import jax.numpy as jnp
from jax import lax
from jax.experimental import pallas as pl
from jax.experimental.pallas import tpu as pltpu

F32 = jnp.float32
BF16 = jnp.bfloat16
I32 = jnp.int32

RMS_EPS = 1e-6
ROPE_THETA = 10000.0
LOG2E = 1.4426950408889634

LANES = 128
SUBLANES = 8
VMEM_BYTES_V7X = 64 * 1024 * 1024

ATTN_HEADS = 4
ATTN_HEAD_DIM = 64
HEAD_SLAB = 2 * ATTN_HEAD_DIM
SSM_GROUP_CH = 16
SSM_STATE = 64
SSM_OCTET = LANES // SSM_GROUP_CH
SSM_CHUNK = 8
N_EXPERTS = 8
TOP_K = 2
SEG = 512


def _cparams(sem, vmem_mb):
    return pltpu.CompilerParams(dimension_semantics=sem, vmem_limit_bytes=vmem_mb * 1024 * 1024)


def _rms(x, g):
    return x * lax.rsqrt(jnp.mean(x * x, axis=-1, keepdims=True) + RMS_EPS) * g


def _sigmoid(x):
    return 0.5 * jnp.tanh(0.5 * x) + 0.5


def _inproj_kernel(x_ref, g_ref, w_ref, cos_ref, sin_ref, o_ref, u_ref, *, qscale):
    xn = _rms(x_ref[...], g_ref[...]).astype(BF16)
    cos = cos_ref[...]
    sin = sin_ref[...]

    def rot(a):
        parts = []
        for s in range(0, a.shape[1], HEAD_SLAB):
            blk = a[:, s:s + HEAD_SLAB]
            parts.append(blk * cos + pltpu.roll(blk, HEAD_SLAB // 2, 1) * sin)
        return jnp.concatenate(parts, axis=1)

    for j in range(w_ref.shape[1] // SEG):
        acc = jnp.dot(xn, w_ref[:, j * SEG:(j + 1) * SEG], preferred_element_type=F32)
        if j == 0:
            acc = rot(acc) * qscale
        elif j == 1:
            acc = rot(acc)
        elif j == 3:
            u_ref[...] = acc
        elif j >= 4:
            acc = _sigmoid(acc)
        o_ref[:, j * SEG:(j + 1) * SEG] = acc.astype(BF16)


def _inproj(x, g, w, cos_t, sin_t, seq, bm):
    t, d = x.shape
    n = w.shape[1]
    qscale = ATTN_HEAD_DIM ** -0.5 * LOG2E
    nseq = seq // bm
    return pl.pallas_call(
        functools.partial(_inproj_kernel, qscale=qscale),
        grid=(t // bm,),
        in_specs=[
            pl.BlockSpec((bm, d), lambda i: (i, 0)),
            pl.BlockSpec((1, d), lambda i: (0, 0)),
            pl.BlockSpec((d, n), lambda i: (0, 0)),
            pl.BlockSpec((bm, HEAD_SLAB), lambda i: (i % nseq, 0)),
            pl.BlockSpec((bm, HEAD_SLAB), lambda i: (i % nseq, 0)),
        ],
        out_specs=[pl.BlockSpec((bm, n), lambda i: (i, 0)), pl.BlockSpec((bm, SEG), lambda i: (i, 0))],
        out_shape=[jax.ShapeDtypeStruct((t, n), BF16), jax.ShapeDtypeStruct((t, SEG), F32)],
        compiler_params=_cparams(("arbitrary",), 52),
        name="inproj",
    )(x, g, w, cos_t, sin_t)


ONES_ROWS = 16


def _attn_kernel(sc_ref, lp_ref, q_ref, k_ref, v_ref, g_ref, o_ref, vt_ref, acc_ref, m_ref,
                 s0_ref, s1_ref, p0_ref, p1_ref, x0_ref, x1_ref, *, bk):
    qi = pl.program_id(2)
    seq = k_ref.shape[0]
    bq = q_ref.shape[0]
    nkv = seq // bk
    assert nkv >= 2 and nkv % 2 == 0

    @pl.when(qi == 0)
    def _():
        for c in range(nkv):
            vt_ref[c, :HEAD_SLAB, :] = v_ref[c * bk:(c + 1) * bk, :].astype(F32).T.astype(BF16)
            vt_ref[c, HEAD_SLAB:, :] = jnp.ones((ONES_ROWS, bk), BF16)

    qt = q_ref[...].astype(F32).T
    row = lax.broadcasted_iota(I32, qt.shape, 0)
    half0 = (row % ATTN_HEAD_DIM) < (ATTN_HEAD_DIM // 2)
    qst = jnp.concatenate([jnp.where(half0, qt, 0.0), jnp.where(half0, 0.0, qt)], axis=1).astype(BF16)

    acc_ref[...] = jnp.zeros_like(acc_ref)
    m_ref[...] = jnp.full_like(m_ref, -1e30)
    s_refs = (s0_ref, s1_ref)
    p_refs = (p0_ref, p1_ref)
    x_refs = (x0_ref, x1_ref)

    def qk(j, cur):
        off = pl.multiple_of(j * bk, bk)
        st = jnp.dot(k_ref[pl.ds(off, bk), :], qst, preferred_element_type=F32)
        s_refs[cur][...] = st
        x_refs[cur][...] = jnp.max(st, axis=0, keepdims=True)

    def pv(j, cur, alpha):
        acc_ref[...] = acc_ref[...] * alpha + jnp.dot(vt_ref[j], p_refs[cur][...],
                                                      preferred_element_type=F32)

    def softmax(cur):
        m_prev = m_ref[...]
        m_new = jnp.maximum(m_prev, x_refs[cur][...])
        p_refs[cur][...] = jnp.exp2((s_refs[cur][...] - m_new).astype(BF16))
        m_ref[...] = m_new
        return jnp.exp2(m_prev - m_new)

    def stage(j, cur, alpha):
        qk(j + 1, 1 - cur)
        pv(j - 1, 1 - cur, alpha)
        return softmax(cur)

    qk(0, 0)
    qk(1, 1)
    alpha = softmax(0)

    def pair(jj, alpha):
        j = 1 + 2 * jj
        return stage(j + 1, 0, stage(j, 1, alpha))

    alpha = lax.fori_loop(0, (nkv - 2) // 2, pair, alpha)
    pv(nkv - 2, 0, alpha)
    alpha = softmax(1)
    pv(nkv - 1, 1, alpha)

    lam_init = sc_ref[0]
    lp = lp_ref[...]
    lam = (jnp.exp(jnp.sum(lp[0:1] * lp[1:2], axis=-1, keepdims=True))
           - jnp.exp(jnp.sum(lp[2:3] * lp[3:4], axis=-1, keepdims=True)) + lam_init)
    acc = acc_ref[...]
    o = acc[:HEAD_SLAB] / acc[HEAD_SLAB:HEAD_SLAB + 1]
    a = o[:, :bq] - lam * o[:, bq:]
    a = a * lax.rsqrt(jnp.mean(a * a, axis=0, keepdims=True) + RMS_EPS) * (1.0 - lam_init)
    o_ref[...] = (a.T * g_ref[...]).astype(BF16)


def _attention(proj, lam_init, lam_p, subln, bsz, seq, bq, bk):
    t = proj.shape[0]
    nq = seq // bq
    koff = SEG // HEAD_SLAB
    return pl.pallas_call(
        functools.partial(_attn_kernel, bk=bk),
        grid=(bsz, ATTN_HEADS, nq),
        in_specs=[
            pl.BlockSpec(memory_space=pltpu.SMEM),
            pl.BlockSpec((4, ATTN_HEAD_DIM), lambda b, h, i: (0, 0)),
            pl.BlockSpec((bq, HEAD_SLAB), lambda b, h, i: (b * nq + i, h)),
            pl.BlockSpec((seq, HEAD_SLAB), lambda b, h, i: (b, koff + h)),
            pl.BlockSpec((seq, HEAD_SLAB), lambda b, h, i: (b, 2 * koff + h)),
            pl.BlockSpec((1, HEAD_SLAB), lambda b, h, i: (0, 0)),
        ],
        out_specs=pl.BlockSpec((bq, HEAD_SLAB), lambda b, h, i: (b * nq + i, h)),
        out_shape=jax.ShapeDtypeStruct((t, ATTN_HEADS * HEAD_SLAB), BF16),
        scratch_shapes=[
            pltpu.VMEM((seq // bk, HEAD_SLAB + ONES_ROWS, bk), BF16),
            pltpu.VMEM((HEAD_SLAB + ONES_ROWS, 2 * bq), F32),
            pltpu.VMEM((1, 2 * bq), F32),
            pltpu.VMEM((bk, 2 * bq), F32),
            pltpu.VMEM((bk, 2 * bq), F32),
            pltpu.VMEM((bk, 2 * bq), BF16),
            pltpu.VMEM((bk, 2 * bq), BF16),
            pltpu.VMEM((1, 2 * bq), F32),
            pltpu.VMEM((1, 2 * bq), F32),
        ],
        compiler_params=_cparams(("arbitrary", "arbitrary", "arbitrary"), 56),
        name="diff_attention",
    )(lam_init, lam_p, proj, proj, proj, subln)


SSM_TILE = SSM_OCTET * SSM_GROUP_CH
SSM_CW = SSM_CHUNK * SSM_TILE
SSM_SW = 4 * SSM_OCTET * SSM_STATE
SSM_QPER = SSM_SW // LANES
SCAN_SEGS = SUBLANES


def _s5_tables(lam_re, lam_im, log_dt, b_re, b_im, c_re, c_im, d_skip):
    hp = lax.Precision.HIGHEST
    ll = SSM_CHUNK
    g = lam_re.shape[1]
    no = g // SSM_OCTET
    lam = lax.complex(lam_re, lam_im)
    ldt = lam * jnp.exp(log_dt)[..., None]
    a = jnp.exp(ldt)
    bbar = ((a - 1.0) / lam)[..., None] * lax.complex(b_re, b_im)
    cmat = lax.complex(c_re, c_im)
    n = jnp.arange(ll + 1, dtype=F32)
    pw = jnp.exp(ldt[:, None] * n[None, :, None, None])

    def kern(d):
        return jnp.real(jnp.einsum('ghp,dgp,gpk->dghk', cmat[d], pw[d, :ll], bbar[d], precision=hp))

    kf, kb = kern(0), kern(1)
    tt = jnp.arange(ll)
    lag = tt[None, :] - tt[:, None]
    mf = jnp.where((lag >= 0)[:, :, None, None, None], kf[jnp.clip(lag, 0, ll - 1)], 0.0)
    mb = jnp.where((lag <= 0)[:, :, None, None, None], kb[jnp.clip(-lag, 0, ll - 1)], 0.0)
    skip = (jnp.eye(ll, dtype=F32)[:, :, None, None, None]
            * (d_skip.reshape(g, SSM_GROUP_CH)[None, None, :, :, None] * jnp.eye(SSM_GROUP_CH, dtype=F32)[None, None, None]))
    m = (mf + mb + skip).reshape(ll, ll, no, SSM_OCTET, SSM_GROUP_CH, SSM_GROUP_CH)
    a2 = m.transpose(2, 0, 3, 5, 1, 4).reshape(no, SSM_CW, ll * SSM_GROUP_CH)

    ein = jnp.stack([pw[0, ll - 1 - tt][..., None] * bbar[0][None],
                     pw[1, tt][..., None] * bbar[1][None]])
    ein = jnp.stack([jnp.real(ein), jnp.imag(ein)], axis=1)
    ein = ein.reshape(2, 2, ll, no, SSM_OCTET, SSM_STATE, SSM_GROUP_CH)
    b2 = ein.transpose(3, 2, 4, 6, 0, 1, 5).reshape(no, SSM_CW, 4 * SSM_STATE)

    eout = jnp.stack([cmat[0][None] * pw[0, tt + 1][:, :, None, :],
                      cmat[1][None] * pw[1, ll - tt][:, :, None, :]])
    eout = jnp.stack([jnp.real(eout), -jnp.imag(eout)], axis=1)
    eout = eout.reshape(2, 2, ll, no, SSM_OCTET, SSM_GROUP_CH, SSM_STATE)
    c2 = eout.transpose(3, 0, 1, 4, 6, 2, 5).reshape(no, SSM_SW, ll * SSM_GROUP_CH)

    al = pw[:, ll]
    apow = jnp.stack([jnp.real(al), jnp.imag(al)], axis=1)
    apow = apow.reshape(2, 2, no, SSM_OCTET * SSM_STATE).transpose(2, 0, 1, 3)
    return a2.astype(BF16), b2.astype(BF16), c2.astype(BF16), apow.astype(F32)


def _to_row_tiles(ref, x):
    n = x.shape[0]
    s_per = x.shape[1] // LANES
    for s in range(s_per):
        ref[pl.ds(s, n, stride=s_per), :] = x[:, s * LANES:(s + 1) * LANES]


def _from_row_tiles(ref, s_per):
    n = ref.shape[0] // s_per
    return jnp.concatenate([ref[pl.ds(s, n, stride=s_per), :] for s in range(s_per)], axis=1)


def _shr(idx, n):
    assert n & (n - 1) == 0
    return idx >> (n.bit_length() - 1)


def _group_of(idx, width):
    return _shr(idx, width) & (SSM_OCTET - 1)


def _spread(compact, rep_rows, rep_cols, inner, row_w, col_w):
    q = lax.broadcasted_iota(I32, (rep_rows, rep_cols), 0)
    c = lax.broadcasted_iota(I32, (rep_rows, rep_cols), 1)
    rep = ((_shr(q, inner) == _shr(c, inner * SSM_OCTET)) & ((q & (inner - 1)) == (c & (inner - 1)))).astype(BF16)
    full = jnp.dot(compact, rep, preferred_element_type=F32)
    r = lax.broadcasted_iota(I32, full.shape, 0)
    cc = lax.broadcasted_iota(I32, full.shape, 1)
    return jnp.where(_group_of(r, row_w) == _group_of(cc, col_w), full, 0.0).astype(BF16)


def _seg_rows(q, k, seglen):
    return pl.ds(q * SCAN_SEGS + k, seglen, stride=SSM_QPER * SCAN_SEGS)


def _s5_local_kernel(u_ref, a2_ref, b2_ref, yi_ref, st_ref, toe_ref, sin_ref, *, parts):
    ncb = yi_ref.shape[0]
    part = pl.program_id(2)
    segs = SCAN_SEGS // parts
    seglen = ncb // segs

    @pl.when((pl.program_id(1) == 0) & (part == 0))
    def _():
        toe_ref[...] = _spread(a2_ref[...], a2_ref.shape[1], SSM_CW, SSM_GROUP_CH, SSM_GROUP_CH, SSM_GROUP_CH)
        sin_ref[...] = _spread(b2_ref[...], b2_ref.shape[1], SSM_SW, SSM_STATE, SSM_GROUP_CH, SSM_STATE)

    u = jnp.concatenate([u_ref[pl.ds(j, ncb, stride=SSM_CHUNK), :] for j in range(SSM_CHUNK)],
                        axis=1).astype(BF16)
    yi_ref[...] = jnp.dot(u, toe_ref[...], preferred_element_type=F32)
    st = jnp.dot(u, sin_ref[...], preferred_element_type=F32)
    for pv in range(parts):
        @pl.when(part == pv)
        def _():
            for kk in range(segs):
                for q in range(SSM_QPER):
                    st_ref[_seg_rows(q, pv * segs + kk, seglen), :] = (
                        st[kk * seglen:(kk + 1) * seglen, q * LANES:(q + 1) * LANES])


def _cmul(ar, ai, xr, xi):
    return ar * xr - ai * xi, ar * xi + ai * xr


def _s5_scan_kernel(st_ref, ap_ref, x_ref):
    wt = SSM_QPER // 4
    seglen = st_ref.shape[0] // SSM_QPER
    assert seglen & (seglen - 1) == 0
    ap = ap_ref[...]

    def coef(d, r):
        return [jnp.broadcast_to(ap[d, r:r + 1, t * LANES:(t + 1) * LANES], (SCAN_SEGS, LANES)) for t in range(wt)]

    far, fai, bar, bai = coef(0, 0), coef(0, 1), coef(1, 0), coef(1, 1)

    def scan(init, store):
        def step(i, carry):
            fr, fi, br, bi = [list(c) for c in carry]
            rf = i * SSM_QPER
            rb = (seglen - 1 - i) * SSM_QPER
            for t in range(wt):
                if store:
                    x_ref[rf + t] = fr[t]
                    x_ref[rf + wt + t] = fi[t]
                    x_ref[rb + 2 * wt + t] = br[t]
                    x_ref[rb + 3 * wt + t] = bi[t]
                pr, pi = _cmul(far[t], fai[t], fr[t], fi[t])
                fr[t] = pr + st_ref[rf + t]
                fi[t] = pi + st_ref[rf + wt + t]
                pr, pi = _cmul(bar[t], bai[t], br[t], bi[t])
                br[t] = pr + st_ref[rb + 2 * wt + t]
                bi[t] = pi + st_ref[rb + 3 * wt + t]
            return tuple(fr), tuple(fi), tuple(br), tuple(bi)

        return lax.fori_loop(0, seglen, step, init)

    z = tuple(jnp.zeros((SCAN_SEGS, LANES), F32) for _ in range(wt))
    fr, fi, br, bi = scan((z, z, z, z), store=False)

    z1 = jnp.zeros((1, LANES), F32)
    fcr, fci, bcr, bci = [], [], [], []
    for t in range(wt):
        fsr, fsi, bsr, bsi = far[t][0:1], fai[t][0:1], bar[t][0:1], bai[t][0:1]
        for _ in range(seglen.bit_length() - 1):
            fsr, fsi = _cmul(fsr, fsi, fsr, fsi)
            bsr, bsi = _cmul(bsr, bsi, bsr, bsi)
        cr, ci = z1, z1
        rs, is_ = [], []
        for k in range(SCAN_SEGS):
            rs.append(cr)
            is_.append(ci)
            cr, ci = _cmul(fsr, fsi, cr, ci)
            cr, ci = cr + fr[t][k:k + 1], ci + fi[t][k:k + 1]
        fcr.append(jnp.concatenate(rs, axis=0))
        fci.append(jnp.concatenate(is_, axis=0))
        cr, ci = z1, z1
        rs, is_ = [None] * SCAN_SEGS, [None] * SCAN_SEGS
        for k in reversed(range(SCAN_SEGS)):
            rs[k] = cr
            is_[k] = ci
            cr, ci = _cmul(bsr, bsi, cr, ci)
            cr, ci = cr + br[t][k:k + 1], ci + bi[t][k:k + 1]
        bcr.append(jnp.concatenate(rs, axis=0))
        bci.append(jnp.concatenate(is_, axis=0))

    scan((tuple(fcr), tuple(fci), tuple(bcr), tuple(bci)), store=True)


def _s5_out_kernel(yi_ref, x_ref, c2_ref, y_ref, sout_ref, *, parts):
    ncb = yi_ref.shape[0]
    part = pl.program_id(2)
    segs = SCAN_SEGS // parts
    seglen = ncb // segs

    @pl.when((pl.program_id(1) == 0) & (part == 0))
    def _():
        sout_ref[...] = _spread(c2_ref[...], c2_ref.shape[1], SSM_CW, SSM_GROUP_CH, SSM_STATE, SSM_GROUP_CH)

    for pv in range(parts):
        @pl.when(part == pv)
        def _():
            x = jnp.concatenate(
                [jnp.concatenate([x_ref[_seg_rows(q, pv * segs + kk, seglen), :] for q in range(SSM_QPER)], axis=1)
                 for kk in range(segs)], axis=0).astype(BF16)
            y = yi_ref[...] + jnp.dot(x, sout_ref[...], preferred_element_type=F32)
            for t in range(SSM_CHUNK):
                y_ref[pl.ds(t, ncb, stride=SSM_CHUNK), :] = y[:, t * SSM_TILE:(t + 1) * SSM_TILE]


def _s5(u, a2, b2, c2, apow, bsz, ncb):
    t, width = u.shape
    no = width // SSM_TILE
    nc = t // SSM_CHUNK
    ncseq = nc // bsz
    parts = ncseq // ncb
    assert SCAN_SEGS % parts == 0
    rows = ncb * SSM_CHUNK
    slab = ncseq * SSM_QPER
    yi, st = pl.pallas_call(
        functools.partial(_s5_local_kernel, parts=parts),
        grid=(no, bsz, parts),
        in_specs=[
            pl.BlockSpec((rows, SSM_TILE), lambda o, b, p: (b * parts + p, o)),
            pl.BlockSpec((None,) + a2.shape[1:], lambda o, b, p: (o, 0, 0)),
            pl.BlockSpec((None,) + b2.shape[1:], lambda o, b, p: (o, 0, 0)),
        ],
        out_specs=[
            pl.BlockSpec((None, ncb, SSM_CW), lambda o, b, p: (o, b * parts + p, 0)),
            pl.BlockSpec((None, None, slab, LANES), lambda o, b, p: (o, b, 0, 0)),
        ],
        out_shape=[jax.ShapeDtypeStruct((no, nc, SSM_CW), F32),
                   jax.ShapeDtypeStruct((no, bsz, slab, LANES), F32)],
        scratch_shapes=[pltpu.VMEM((SSM_CW, SSM_CW), BF16), pltpu.VMEM((SSM_CW, SSM_SW), BF16)],
        compiler_params=_cparams(("arbitrary", "arbitrary", "arbitrary"), 52),
        name="s5_local",
    )(u, a2, b2)
    tiles = slab // SCAN_SEGS
    xin = pl.pallas_call(
        _s5_scan_kernel,
        grid=(no, bsz),
        in_specs=[
            pl.BlockSpec((None, None, tiles, SCAN_SEGS, LANES), lambda o, b: (o, b, 0, 0, 0)),
            pl.BlockSpec((None, 2, 2, SSM_SW // 4), lambda o, b: (o, 0, 0, 0)),
        ],
        out_specs=pl.BlockSpec((None, None, tiles, SCAN_SEGS, LANES), lambda o, b: (o, b, 0, 0, 0)),
        out_shape=jax.ShapeDtypeStruct((no, bsz, tiles, SCAN_SEGS, LANES), F32),
        compiler_params=_cparams(("arbitrary", "arbitrary"), 48),
        name="s5_scan",
    )(st.reshape(no, bsz, tiles, SCAN_SEGS, LANES), apow)
    return pl.pallas_call(
        functools.partial(_s5_out_kernel, parts=parts),
        grid=(no, bsz, parts),
        in_specs=[
            pl.BlockSpec((None, ncb, SSM_CW), lambda o, b, p: (o, b * parts + p, 0)),
            pl.BlockSpec((None, None, slab, LANES), lambda o, b, p: (o, b, 0, 0)),
            pl.BlockSpec((None,) + c2.shape[1:], lambda o, b, p: (o, 0, 0)),
        ],
        out_specs=pl.BlockSpec((rows, SSM_TILE), lambda o, b, p: (b * parts + p, o)),
        out_shape=jax.ShapeDtypeStruct((t, width), F32),
        scratch_shapes=[pltpu.VMEM((SSM_SW, SSM_CW), BF16)],
        compiler_params=_cparams(("arbitrary", "arbitrary", "arbitrary"), 52),
        name="s5_out",
    )(yi, xin.reshape(no, bsz, slab, LANES), c2)


def _merge_kernel(a_ref, y_ref, g0_ref, g1_ref, x_ref, wglu_ref, wb0_ref, wb1_ref, wout_ref, o_ref):
    y = y_ref[...]
    s = 0.5 * y * (1.0 + lax.erf(y * (2.0 ** -0.5)))
    glu = jnp.dot(s.astype(BF16), wglu_ref[...], preferred_element_type=F32)
    s = s * jax.nn.sigmoid(glu)
    pa = jnp.dot(a_ref[...], wb0_ref[...], preferred_element_type=F32)
    ps = jnp.dot(s.astype(BF16), wb1_ref[...], preferred_element_type=F32)
    merged = g0_ref[...].astype(F32) * pa + g1_ref[...].astype(F32) * ps
    o_ref[...] = x_ref[...] + jnp.dot(merged.astype(BF16), wout_ref[...], preferred_element_type=F32)


def _merge(a, y, proj, x, wglu, wb0, wb1, wout, bm):
    t, d = x.shape
    aw = a.shape[1]
    g0 = 4 * SEG // d
    full = lambda arr: pl.BlockSpec(arr.shape, lambda i: (0,) * arr.ndim)
    return pl.pallas_call(
        _merge_kernel,
        grid=(t // bm,),
        in_specs=[
            pl.BlockSpec((bm, aw), lambda i: (i, 0)),
            pl.BlockSpec((bm, aw), lambda i: (i, 0)),
            pl.BlockSpec((bm, d), lambda i: (i, g0)),
            pl.BlockSpec((bm, d), lambda i: (i, g0 + 1)),
            pl.BlockSpec((bm, d), lambda i: (i, 0)),
            full(wglu), full(wb0), full(wb1), full(wout),
        ],
        out_specs=pl.BlockSpec((bm, d), lambda i: (i, 0)),
        out_shape=jax.ShapeDtypeStruct((t, d), F32),
        compiler_params=_cparams(("arbitrary",), 48),
        name="merge",
    )(a, y, proj, proj, x, wglu, wb0, wb1, wout)


def _swiglu_partial(h, w1_ref, w3_ref, w2_ref, sub):
    ffc = w1_ref.shape[-1]
    acc = None
    for s in range(0, ffc, sub):
        e = min(s + sub, ffc)
        a = jnp.dot(h, w1_ref[:, s:e], preferred_element_type=F32)
        b = jnp.dot(h, w3_ref[:, s:e], preferred_element_type=F32)
        tt = (a * jax.nn.sigmoid(a) * b).astype(BF16)
        c = jnp.dot(tt, w2_ref[s:e, :], preferred_element_type=F32)
        acc = c if acc is None else acc + c
    return acc


def _ffn_kernel(x_ref, g_ref, w1_ref, w3_ref, w2_ref, o_ref, hn_ref, *, sub):
    f = pl.program_id(1)

    @pl.when(f == 0)
    def _():
        x = x_ref[...]
        hn_ref[...] = _rms(x, g_ref[...]).astype(BF16)
        o_ref[...] = x

    o_ref[...] += _swiglu_partial(hn_ref[...], w1_ref, w3_ref, w2_ref, sub)


def _ffn_tile(dff):
    return dff // 2 if (dff // 2) % LANES == 0 else dff


def _ffn(x, g, w1, w3, w2, bm):
    t, d = x.shape
    dff = w1.shape[1]
    ffc = _ffn_tile(dff)
    return pl.pallas_call(
        functools.partial(_ffn_kernel, sub=2 * LANES),
        grid=(t // bm, dff // ffc),
        in_specs=[
            pl.BlockSpec((bm, d), lambda i, f: (i, 0)),
            pl.BlockSpec((1, d), lambda i, f: (0, 0)),
            pl.BlockSpec((d, ffc), lambda i, f: (0, f)),
            pl.BlockSpec((d, ffc), lambda i, f: (0, f)),
            pl.BlockSpec((ffc, d), lambda i, f: (f, 0)),
        ],
        out_specs=pl.BlockSpec((bm, d), lambda i, f: (i, 0)),
        out_shape=jax.ShapeDtypeStruct((t, d), F32),
        scratch_shapes=[pltpu.VMEM((bm, d), BF16)],
        compiler_params=_cparams(("arbitrary", "arbitrary"), 56),
        name="ffn_dense",
    )(x, g, w1, w3, w2)


def _router_kernel(x_ref, g_ref, wr_ref, hn_ref, r_ref):
    hn = _rms(x_ref[...], g_ref[...])
    _to_row_tiles(hn_ref, hn)
    logits = jnp.dot(hn, wr_ref[...], preferred_element_type=F32, precision=lax.Precision.HIGHEST)
    lane = lax.broadcasted_iota(I32, logits.shape, 1)
    neg = jnp.float32(-1e30)
    logits = jnp.where(lane < N_EXPERTS, logits, neg)
    m1 = jnp.max(logits, axis=-1, keepdims=True)
    i1 = jnp.min(jnp.where(logits == m1, lane, LANES), axis=-1, keepdims=True)
    rest = jnp.where(lane == i1, neg, logits)
    m2 = jnp.max(rest, axis=-1, keepdims=True)
    i2 = jnp.min(jnp.where(rest == m2, lane, LANES), axis=-1, keepdims=True)
    e2 = jnp.exp(m2 - m1)
    g1 = 1.0 / (1.0 + e2)
    g2 = e2 / (1.0 + e2)
    r_ref[...] = jnp.where(lane == 0, i1.astype(F32),
                           jnp.where(lane == 1, i2.astype(F32),
                                     jnp.where(lane == 2, g1, jnp.where(lane == 3, g2, 0.0))))


def _router(x, g, wr_pad, bm):
    t, d = x.shape
    s_per = d // LANES
    return pl.pallas_call(
        _router_kernel,
        grid=(t // bm,),
        in_specs=[
            pl.BlockSpec((bm, d), lambda i: (i, 0)),
            pl.BlockSpec((1, d), lambda i: (0, 0)),
            pl.BlockSpec((d, LANES), lambda i: (0, 0)),
        ],
        out_specs=[pl.BlockSpec((bm * s_per, LANES), lambda i: (i, 0)), pl.BlockSpec((bm, LANES), lambda i: (i, 0))],
        out_shape=[jax.ShapeDtypeStruct((t * s_per, LANES), F32), jax.ShapeDtypeStruct((t, LANES), F32)],
        compiler_params=_cparams(("arbitrary",), 40),
        name="moe_router",
    )(x, g, wr_pad)


def _row_copy(src_hbm, dst_ref, sem, src_row, dst_row, s_per):
    src = pl.multiple_of(src_row * s_per, s_per)
    dst = pl.multiple_of(dst_row * s_per, s_per)
    return pltpu.make_async_copy(src_hbm.at[pl.ds(src, s_per)], dst_ref.at[pl.ds(dst, s_per)], sem)


def _gather_kernel(idx_ref, src_hbm, o_ref, sem, *, s_per):
    bm = o_ref.shape[0] // s_per
    base = pl.program_id(0) * bm

    def issue(r, c):
        _row_copy(src_hbm, o_ref, sem, idx_ref[base + r], r, s_per).start()
        return c

    lax.fori_loop(0, bm, issue, 0, unroll=8)

    def drain(r, c):
        _row_copy(src_hbm, o_ref, sem, 0, r, s_per).wait()
        return c

    lax.fori_loop(0, bm, drain, 0, unroll=8)


def _gather_rows(idx, src, bm, s_per):
    n = idx.shape[0]
    assert n % bm == 0
    return pl.pallas_call(
        functools.partial(_gather_kernel, s_per=s_per),
        grid_spec=pltpu.PrefetchScalarGridSpec(
            num_scalar_prefetch=1,
            grid=(n // bm,),
            in_specs=[pl.BlockSpec(memory_space=pl.ANY)],
            out_specs=pl.BlockSpec((bm * s_per, LANES), lambda i, idx: (i, 0)),
            scratch_shapes=[pltpu.SemaphoreType.DMA(())],
        ),
        out_shape=jax.ShapeDtypeStruct((n * s_per, LANES), src.dtype),
        compiler_params=_cparams(("arbitrary",), 40),
        name="moe_gather",
    )(idx, src)


def _expert_kernel(be_ref, x_ref, w1_ref, w3_ref, w2_ref, o_ref, xb_ref, acc_ref, *, sub, s_per):
    f = pl.program_id(1)

    @pl.when(f == 0)
    def _():
        xb_ref[...] = _from_row_tiles(x_ref, s_per).astype(BF16)

    part = _swiglu_partial(xb_ref[...], w1_ref, w3_ref, w2_ref, sub)

    @pl.when(f == 0)
    def _():
        acc_ref[...] = part

    @pl.when((f > 0) & (f < pl.num_programs(1) - 1))
    def _():
        acc_ref[...] += part

    @pl.when(f == pl.num_programs(1) - 1)
    def _():
        _to_row_tiles(o_ref, acc_ref[...] + part)


def _experts(block_exp, xb, w1, w3, w2, bm):
    d = w1.shape[1]
    dff = w1.shape[2]
    s_per = d // LANES
    n = xb.shape[0] // s_per
    ffc = _ffn_tile(dff)
    assert dff // ffc >= 2
    return pl.pallas_call(
        functools.partial(_expert_kernel, sub=2 * LANES, s_per=s_per),
        grid_spec=pltpu.PrefetchScalarGridSpec(
            num_scalar_prefetch=1,
            grid=(n // bm, dff // ffc),
            in_specs=[
                pl.BlockSpec((bm * s_per, LANES), lambda i, f, be: (i, 0)),
                pl.BlockSpec((None, d, ffc), lambda i, f, be: (be[i], 0, f)),
                pl.BlockSpec((None, d, ffc), lambda i, f, be: (be[i], 0, f)),
                pl.BlockSpec((None, ffc, d), lambda i, f, be: (be[i], f, 0)),
            ],
            out_specs=pl.BlockSpec((bm * s_per, LANES), lambda i, f, be: (i, 0)),
            scratch_shapes=[pltpu.VMEM((bm, d), BF16), pltpu.VMEM((bm, d), F32)],
        ),
        out_shape=jax.ShapeDtypeStruct((n * s_per, LANES), F32),
        compiler_params=_cparams(("arbitrary", "arbitrary"), 56),
        name="moe_experts",
    )(block_exp, xb, w1, w3, w2)


def _combine_kernel(d0_ref, d1_ref, yb_hbm, x_ref, r_ref, o_ref, r0_ref, r1_ref, sem, *, s_per):
    bm = o_ref.shape[0]
    i = pl.program_id(0)

    def issue(step, slot):
        base = step * bm

        def body(r, c):
            _row_copy(yb_hbm, r0_ref.at[slot], sem.at[slot], d0_ref[base + r], r, s_per).start()
            _row_copy(yb_hbm, r1_ref.at[slot], sem.at[slot], d1_ref[base + r], r, s_per).start()
            return c

        lax.fori_loop(0, bm, body, 0, unroll=8)

    def drain(slot):
        def body(r, c):
            _row_copy(yb_hbm, r0_ref.at[slot], sem.at[slot], 0, r, s_per).wait()
            _row_copy(yb_hbm, r1_ref.at[slot], sem.at[slot], 0, r, s_per).wait()
            return c

        lax.fori_loop(0, bm, body, 0, unroll=8)

    @pl.when(i == 0)
    def _():
        issue(0, 0)

    for slot in range(2):
        @pl.when(i % 2 == slot)
        def _():
            @pl.when(i + 1 < pl.num_programs(0))
            def _():
                issue(i + 1, 1 - slot)

            drain(slot)
            route = r_ref[...]
            o_ref[...] = (x_ref[...] + route[:, 2:3] * _from_row_tiles(r0_ref.at[slot], s_per)
                          + route[:, 3:4] * _from_row_tiles(r1_ref.at[slot], s_per))


def _combine(d0, d1, yb, x, route, bm):
    t, d = x.shape
    s_per = d // LANES
    return pl.pallas_call(
        functools.partial(_combine_kernel, s_per=s_per),
        grid_spec=pltpu.PrefetchScalarGridSpec(
            num_scalar_prefetch=2,
            grid=(t // bm,),
            in_specs=[
                pl.BlockSpec(memory_space=pl.ANY),
                pl.BlockSpec((bm, d), lambda i, a, b: (i, 0)),
                pl.BlockSpec((bm, LANES), lambda i, a, b: (i, 0)),
            ],
            out_specs=pl.BlockSpec((bm, d), lambda i, a, b: (i, 0)),
            scratch_shapes=[pltpu.VMEM((2, bm * s_per, LANES), F32), pltpu.VMEM((2, bm * s_per, LANES), F32),
                            pltpu.SemaphoreType.DMA((2,))],
        ),
        out_shape=jax.ShapeDtypeStruct((t, d), F32),
        compiler_params=_cparams(("arbitrary",), 32),
        name="moe_combine",
    )(d0, d1, yb, x, route)


def _dispatch_tables(idx, bm):
    t = idx.shape[0]
    flat_e = idx.reshape(-1)
    onehot = (flat_e[:, None] == jnp.arange(N_EXPERTS, dtype=I32)[None, :]).astype(I32)
    csum = jnp.cumsum(onehot, axis=0)
    rank = jnp.take_along_axis(csum, flat_e[:, None], axis=1)[:, 0] - 1
    counts = csum[-1]
    padded = (counts + bm - 1) // bm * bm
    pad_end = jnp.cumsum(padded)
    pad_start = pad_end - padded
    dest = (pad_start[flat_e] + rank).astype(I32)
    n_slots = t * TOP_K + N_EXPERTS * bm
    slot_tok = jnp.zeros((n_slots,), I32).at[dest].set(jnp.arange(t * TOP_K, dtype=I32) // TOP_K)
    block_start = jnp.arange(n_slots // bm, dtype=I32) * bm
    block_exp = jnp.minimum(jnp.searchsorted(pad_end, block_start, side='right'), N_EXPERTS - 1).astype(I32)
    return dest.reshape(t, TOP_K), slot_tok, block_exp


def _moe(x, g, w_router, w1, w3, w2, bm_tok, bm_slot):
    d = x.shape[1]
    wr_pad = jnp.zeros((d, LANES), F32).at[:, :N_EXPERTS].set(w_router)
    hn, route = _router(x, g, wr_pad, bm_tok)
    idx = route[:, :TOP_K].astype(I32)
    dest, slot_tok, block_exp = _dispatch_tables(idx, bm_slot)
    n_slots = slot_tok.shape[0]
    xb = _gather_rows(slot_tok, hn, _block(n_slots, 4 * bm_slot), d // LANES)
    yb = _experts(block_exp, xb, w1, w3, w2, bm_slot)
    return _combine(dest[:, 0], dest[:, 1], yb, x, route, bm_tok)


def _norm_kernel(x_ref, g_ref, o_ref):
    o_ref[...] = _rms(x_ref[...], g_ref[...])


def _final_norm(x, g, bm):
    t, d = x.shape
    return pl.pallas_call(
        _norm_kernel,
        grid=(t // bm,),
        in_specs=[pl.BlockSpec((bm, d), lambda i: (i, 0)), pl.BlockSpec((1, d), lambda i: (0, 0))],
        out_specs=pl.BlockSpec((bm, d), lambda i: (i, 0)),
        out_shape=jax.ShapeDtypeStruct((t, d), F32),
        compiler_params=_cparams(("arbitrary",), 32),
        name="final_norm",
    )(x, g)


def _rope_tables(seq):
    half = ATTN_HEAD_DIM // 2
    inv_freq = 1.0 / (ROPE_THETA ** (jnp.arange(half, dtype=F32) / half))
    ang = jnp.arange(seq, dtype=F32)[:, None] * inv_freq[None, :]
    cos = jnp.tile(jnp.cos(ang), (1, 4))
    sin = jnp.sin(ang)
    return cos, jnp.concatenate([-sin, -sin, sin, sin], axis=1)


def _permute_qk_columns(w):
    d = w.shape[0]
    half = ATTN_HEAD_DIM // 2
    qk = w[:, :2 * SEG].reshape(d, 2, ATTN_HEADS, 2, 2, half).transpose(0, 1, 2, 4, 3, 5).reshape(d, 2 * SEG)
    return jnp.concatenate([qk, w[:, 2 * SEG:]], axis=1)


def _block(n, pref):
    return pref if n % pref == 0 else n


def kernel(x, norm_mix, w_in, attn_lambda, attn_subln, ssm_lam_re, ssm_lam_im, ssm_log_dt, ssm_b_re, ssm_b_im, ssm_c_re, ssm_c_im, ssm_d, w_glu, w_branch, w_out, norm_ffn, ffn_w1, ffn_w3, ffn_w2, moe_router, moe_w1, moe_w3, moe_w2, norm_final):
    bsz, seq, d = x.shape
    depth = w_in.shape[0]
    t = bsz * seq
    cos_t, sin_t = _rope_tables(seq)
    xf = x.reshape(t, d)
    bm = _block(seq, 1024)
    for layer in range(depth):
        w = _permute_qk_columns(w_in[layer]).astype(BF16)
        proj, u = _inproj(xf, norm_mix[layer][None], w, cos_t, sin_t, seq, _block(seq, 512))

        lam_init = jnp.full((1,), 0.8 - 0.6 * math.exp(-0.3 * layer), F32)
        a = _attention(proj, lam_init, attn_lambda[layer], attn_subln[layer][None], bsz, seq,
                       _block(seq, 1024), _block(seq // 2, 512))

        a2, b2, c2, apow = _s5_tables(ssm_lam_re[layer], ssm_lam_im[layer], ssm_log_dt[layer],
                                      ssm_b_re[layer], ssm_b_im[layer], ssm_c_re[layer],
                                      ssm_c_im[layer], ssm_d[layer])
        y = _s5(u, a2, b2, c2, apow, bsz, _block(seq // SSM_CHUNK, 512))

        xf = _merge(a, y, proj, xf, w_glu[layer].astype(BF16), w_branch[layer, 0].astype(BF16),
                    w_branch[layer, 1].astype(BF16), w_out[layer].astype(BF16), _block(seq, 512))

        i = layer // 2
        if layer % 2 == 0:
            xf = _ffn(xf, norm_ffn[layer][None], ffn_w1[i].astype(BF16), ffn_w3[i].astype(BF16),
                      ffn_w2[i].astype(BF16), bm)
        else:
            xf = _moe(xf, norm_ffn[layer][None], moe_router[i], moe_w1[i].astype(BF16),
                      moe_w3[i].astype(BF16), moe_w2[i].astype(BF16), _block(seq, 256), 512)
    return _final_norm(xf, norm_final[None], bm).reshape(bsz, seq, d)
```

```python
import functools
import math

import jax
import jax.numpy as jnp
from jax import lax
from jax.experimental import pallas as pl
from jax.experimental.pallas import tpu as pltpu

F32 = jnp.float32
BF16 = jnp.bfloat16
I32 = jnp.int32

RMS_EPS = 1e-6
ROPE_THETA = 10000.0
LOG2E = 1.4426950408889634

LANES = 128
SUBLANES = 8
VMEM_BYTES_V7X = 64 * 1024 * 1024

ATTN_HEADS = 4
ATTN_HEAD_DIM = 64
HEAD_SLAB = 2 * ATTN_HEAD_DIM
SSM_GROUP_CH = 16
SSM_STATE = 64
SSM_OCTET = LANES // SSM_GROUP_CH
SSM_CHUNK = 8
N_EXPERTS = 8
TOP_K = 2
SEG = 512


def _cparams(sem, vmem_mb):
    return pltpu.CompilerParams(dimension_semantics=sem, vmem_limit_bytes=vmem_mb * 1024 * 1024)


def _rms(x, g):
    return x * lax.rsqrt(jnp.mean(x * x, axis=-1, keepdims=True) + RMS_EPS) * g


def _sigmoid(x):
    return 0.5 * jnp.tanh(0.5 * x) + 0.5


def _inproj_kernel(x_ref, g_ref, w_ref, cos_ref, sin_ref, o_ref, u_ref, *, qscale):
    xn = _rms(x_ref[...], g_ref[...]).astype(BF16)
    cos = cos_ref[...]
    sin = sin_ref[...]

    def rot(a):
        parts = []
        for s in range(0, a.shape[1], HEAD_SLAB):
            blk = a[:, s:s + HEAD_SLAB]
            parts.append(blk * cos + pltpu.roll(blk, HEAD_SLAB // 2, 1) * sin)
        return jnp.concatenate(parts, axis=1)

    for j in range(w_ref.shape[1] // SEG):
        acc = jnp.dot(xn, w_ref[:, j * SEG:(j + 1) * SEG], preferred_element_type=F32)
        if j == 0:
            acc = rot(acc) * qscale
        elif j == 1:
            acc = rot(acc)
        elif j == 3:
            u_ref[...] = acc
        elif j >= 4:
            acc = _sigmoid(acc)
        o_ref[:, j * SEG:(j + 1) * SEG] = acc.astype(BF16)


def _inproj(x, g, w, layer, cos_t, sin_t, seq, bm):
    t, d = x.shape
    n = w.shape[2]
    qscale = ATTN_HEAD_DIM ** -0.5 * LOG2E
    nseq = seq // bm
    return pl.pallas_call(
        functools.partial(_inproj_kernel, qscale=qscale),
        grid=(t // bm,),
        in_specs=[
            pl.BlockSpec((bm, d), lambda i: (i, 0)),
            pl.BlockSpec((1, d), lambda i: (0, 0)),
            pl.BlockSpec((None, d, n), lambda i: (layer, 0, 0)),
            pl.BlockSpec((bm, HEAD_SLAB), lambda i: (i % nseq, 0)),
            pl.BlockSpec((bm, HEAD_SLAB), lambda i: (i % nseq, 0)),
        ],
        out_specs=[pl.BlockSpec((bm, n), lambda i: (i, 0)), pl.BlockSpec((bm, SEG), lambda i: (i, 0))],
        out_shape=[jax.ShapeDtypeStruct((t, n), BF16), jax.ShapeDtypeStruct((t, SEG), F32)],
        compiler_params=_cparams(("arbitrary",), 52),
        name="inproj",
    )(x, g, w, cos_t, sin_t)


ONES_ROWS = 16


def _attn_kernel(sc_ref, lp_ref, q_ref, k_ref, v_ref, g_ref, o_ref, vt_ref, acc_ref, m_ref,
                 s0_ref, s1_ref, p0_ref, p1_ref, x0_ref, x1_ref, *, bk):
    qi = pl.program_id(2)
    seq = k_ref.shape[0]
    bq = q_ref.shape[0]
    nkv = seq // bk
    assert nkv >= 2 and nkv % 2 == 0

    @pl.when(qi == 0)
    def _():
        for c in range(nkv):
            vt_ref[c, :HEAD_SLAB, :] = v_ref[c * bk:(c + 1) * bk, :].astype(F32).T.astype(BF16)
            vt_ref[c, HEAD_SLAB:, :] = jnp.ones((ONES_ROWS, bk), BF16)

    qt = q_ref[...].astype(F32).T
    row = lax.broadcasted_iota(I32, qt.shape, 0)
    half0 = (row % ATTN_HEAD_DIM) < (ATTN_HEAD_DIM // 2)
    qst = jnp.concatenate([jnp.where(half0, qt, 0.0), jnp.where(half0, 0.0, qt)], axis=1).astype(BF16)

    acc_ref[...] = jnp.zeros_like(acc_ref)
    m_ref[...] = jnp.full_like(m_ref, -1e30)
    s_refs = (s0_ref, s1_ref)
    p_refs = (p0_ref, p1_ref)
    x_refs = (x0_ref, x1_ref)

    def qk(j, cur):
        off = pl.multiple_of(j * bk, bk)
        st = jnp.dot(k_ref[pl.ds(off, bk), :], qst, preferred_element_type=F32)
        s_refs[cur][...] = st
        x_refs[cur][...] = jnp.max(st, axis=0, keepdims=True)

    def pv(j, cur, alpha):
        acc_ref[...] = acc_ref[...] * alpha + jnp.dot(vt_ref[j], p_refs[cur][...],
                                                      preferred_element_type=F32)

    def softmax(cur):
        m_prev = m_ref[...]
        m_new = jnp.maximum(m_prev, x_refs[cur][...])
        p_refs[cur][...] = jnp.exp2((s_refs[cur][...] - m_new).astype(BF16))
        m_ref[...] = m_new
        return jnp.exp2(m_prev - m_new)

    def stage(j, cur, alpha):
        qk(j + 1, 1 - cur)
        pv(j - 1, 1 - cur, alpha)
        return softmax(cur)

    qk(0, 0)
    qk(1, 1)
    alpha = softmax(0)

    def pair(jj, alpha):
        j = 1 + 2 * jj
        return stage(j + 1, 0, stage(j, 1, alpha))

    alpha = lax.fori_loop(0, (nkv - 2) // 2, pair, alpha)
    pv(nkv - 2, 0, alpha)
    alpha = softmax(1)
    pv(nkv - 1, 1, alpha)

    lam_init = sc_ref[0]
    lp = lp_ref[...]
    lam = (jnp.exp(jnp.sum(lp[0:1] * lp[1:2], axis=-1, keepdims=True))
           - jnp.exp(jnp.sum(lp[2:3] * lp[3:4], axis=-1, keepdims=True)) + lam_init)
    acc = acc_ref[...]
    o = acc[:HEAD_SLAB] / acc[HEAD_SLAB:HEAD_SLAB + 1]
    a = o[:, :bq] - lam * o[:, bq:]
    a = a * lax.rsqrt(jnp.mean(a * a, axis=0, keepdims=True) + RMS_EPS) * (1.0 - lam_init)
    o_ref[...] = (a.T * g_ref[...]).astype(BF16)


def _attention(proj, lam_init, lam_p, subln, bsz, seq, bq, bk):
    t = proj.shape[0]
    nq = seq // bq
    koff = SEG // HEAD_SLAB
    return pl.pallas_call(
        functools.partial(_attn_kernel, bk=bk),
        grid=(bsz, ATTN_HEADS, nq),
        in_specs=[
            pl.BlockSpec(memory_space=pltpu.SMEM),
            pl.BlockSpec((4, ATTN_HEAD_DIM), lambda b, h, i: (0, 0)),
            pl.BlockSpec((bq, HEAD_SLAB), lambda b, h, i: (b * nq + i, h)),
            pl.BlockSpec((seq, HEAD_SLAB), lambda b, h, i: (b, koff + h)),
            pl.BlockSpec((seq, HEAD_SLAB), lambda b, h, i: (b, 2 * koff + h)),
            pl.BlockSpec((1, HEAD_SLAB), lambda b, h, i: (0, 0)),
        ],
        out_specs=pl.BlockSpec((bq, HEAD_SLAB), lambda b, h, i: (b * nq + i, h)),
        out_shape=jax.ShapeDtypeStruct((t, ATTN_HEADS * HEAD_SLAB), BF16),
        scratch_shapes=[
            pltpu.VMEM((seq // bk, HEAD_SLAB + ONES_ROWS, bk), BF16),
            pltpu.VMEM((HEAD_SLAB + ONES_ROWS, 2 * bq), F32),
            pltpu.VMEM((1, 2 * bq), F32),
            pltpu.VMEM((bk, 2 * bq), F32),
            pltpu.VMEM((bk, 2 * bq), F32),
            pltpu.VMEM((bk, 2 * bq), BF16),
            pltpu.VMEM((bk, 2 * bq), BF16),
            pltpu.VMEM((1, 2 * bq), F32),
            pltpu.VMEM((1, 2 * bq), F32),
        ],
        compiler_params=_cparams(("arbitrary", "arbitrary", "arbitrary"), 56),
        name="diff_attention",
    )(lam_init, lam_p, proj, proj, proj, subln)


SSM_TILE = SSM_OCTET * SSM_GROUP_CH
SSM_CW = SSM_CHUNK * SSM_TILE
SSM_SW = 4 * SSM_OCTET * SSM_STATE
SSM_QPER = SSM_SW // LANES
SCAN_SEGS = SUBLANES


def _s5_tables(lam_re, lam_im, log_dt, b_re, b_im, c_re, c_im, d_skip):
    hp = lax.Precision.HIGHEST
    ll = SSM_CHUNK
    g = lam_re.shape[1]
    no = g // SSM_OCTET
    lam = lax.complex(lam_re, lam_im)
    ldt = lam * jnp.exp(log_dt)[..., None]
    a = jnp.exp(ldt)
    bbar = ((a - 1.0) / lam)[..., None] * lax.complex(b_re, b_im)
    cmat = lax.complex(c_re, c_im)
    n = jnp.arange(ll + 1, dtype=F32)
    pw = jnp.exp(ldt[:, None] * n[None, :, None, None])

    def kern(d):
        return jnp.real(jnp.einsum('ghp,dgp,gpk->dghk', cmat[d], pw[d, :ll], bbar[d], precision=hp))

    kf, kb = kern(0), kern(1)
    tt = jnp.arange(ll)
    lag = tt[None, :] - tt[:, None]
    mf = jnp.where((lag >= 0)[:, :, None, None, None], kf[jnp.clip(lag, 0, ll - 1)], 0.0)
    mb = jnp.where((lag <= 0)[:, :, None, None, None], kb[jnp.clip(-lag, 0, ll - 1)], 0.0)
    skip = (jnp.eye(ll, dtype=F32)[:, :, None, None, None]
            * (d_skip.reshape(g, SSM_GROUP_CH)[None, None, :, :, None] * jnp.eye(SSM_GROUP_CH, dtype=F32)[None, None, None]))
    m = (mf + mb + skip).reshape(ll, ll, no, SSM_OCTET, SSM_GROUP_CH, SSM_GROUP_CH)
    a2 = m.transpose(2, 0, 3, 5, 1, 4).reshape(no, SSM_CW, ll * SSM_GROUP_CH)

    ein = jnp.stack([pw[0, ll - 1 - tt][..., None] * bbar[0][None],
                     pw[1, tt][..., None] * bbar[1][None]])
    ein = jnp.stack([jnp.real(ein), jnp.imag(ein)], axis=1)
    ein = ein.reshape(2, 2, ll, no, SSM_OCTET, SSM_STATE, SSM_GROUP_CH)
    b2 = ein.transpose(3, 2, 4, 6, 0, 1, 5).reshape(no, SSM_CW, 4 * SSM_STATE)

    eout = jnp.stack([cmat[0][None] * pw[0, tt + 1][:, :, None, :],
                      cmat[1][None] * pw[1, ll - tt][:, :, None, :]])
    eout = jnp.stack([jnp.real(eout), -jnp.imag(eout)], axis=1)
    eout = eout.reshape(2, 2, ll, no, SSM_OCTET, SSM_GROUP_CH, SSM_STATE)
    c2 = eout.transpose(3, 0, 1, 4, 6, 2, 5).reshape(no, SSM_SW, ll * SSM_GROUP_CH)

    al = pw[:, ll]
    apow = jnp.stack([jnp.real(al), jnp.imag(al)], axis=1)
    apow = apow.reshape(2, 2, no, SSM_OCTET * SSM_STATE).transpose(2, 0, 1, 3)
    return a2.astype(BF16), b2.astype(BF16), c2.astype(BF16), apow.astype(F32)


def _to_row_tiles(ref, x):
    n = x.shape[0]
    s_per = x.shape[1] // LANES
    for s in range(s_per):
        ref[pl.ds(s, n, stride=s_per), :] = x[:, s * LANES:(s + 1) * LANES]


def _from_row_tiles(ref, s_per):
    n = ref.shape[0] // s_per
    return jnp.concatenate([ref[pl.ds(s, n, stride=s_per), :] for s in range(s_per)], axis=1)


def _shr(idx, n):
    assert n & (n - 1) == 0
    return idx >> (n.bit_length() - 1)


def _group_of(idx, width):
    return _shr(idx, width) & (SSM_OCTET - 1)


def _spread(compact, rep_rows, rep_cols, inner, row_w, col_w):
    q = lax.broadcasted_iota(I32, (rep_rows, rep_cols), 0)
    c = lax.broadcasted_iota(I32, (rep_rows, rep_cols), 1)
    rep = ((_shr(q, inner) == _shr(c, inner * SSM_OCTET)) & ((q & (inner - 1)) == (c & (inner - 1)))).astype(BF16)
    full = jnp.dot(compact, rep, preferred_element_type=F32)
    r = lax.broadcasted_iota(I32, full.shape, 0)
    cc = lax.broadcasted_iota(I32, full.shape, 1)
    return jnp.where(_group_of(r, row_w) == _group_of(cc, col_w), full, 0.0).astype(BF16)


def _seg_rows(q, k, seglen):
    return pl.ds(q * SCAN_SEGS + k, seglen, stride=SSM_QPER * SCAN_SEGS)


def _s5_local_kernel(u_ref, a2_ref, b2_ref, yi_ref, st_ref, toe_ref, sin_ref, *, parts):
    ncb = yi_ref.shape[0]
    part = pl.program_id(2)
    segs = SCAN_SEGS // parts
    seglen = ncb // segs

    @pl.when((pl.program_id(1) == 0) & (part == 0))
    def _():
        toe_ref[...] = _spread(a2_ref[...], a2_ref.shape[1], SSM_CW, SSM_GROUP_CH, SSM_GROUP_CH, SSM_GROUP_CH)
        sin_ref[...] = _spread(b2_ref[...], b2_ref.shape[1], SSM_SW, SSM_STATE, SSM_GROUP_CH, SSM_STATE)

    u = jnp.concatenate([u_ref[pl.ds(j, ncb, stride=SSM_CHUNK), :] for j in range(SSM_CHUNK)],
                        axis=1).astype(BF16)
    yi_ref[...] = jnp.dot(u, toe_ref[...], preferred_element_type=F32)
    st = jnp.dot(u, sin_ref[...], preferred_element_type=F32)
    for pv in range(parts):
        @pl.when(part == pv)
        def _():
            for kk in range(segs):
                for q in range(SSM_QPER):
                    st_ref[_seg_rows(q, pv * segs + kk, seglen), :] = (
                        st[kk * seglen:(kk + 1) * seglen, q * LANES:(q + 1) * LANES])


def _cmul(ar, ai, xr, xi):
    return ar * xr - ai * xi, ar * xi + ai * xr


def _s5_scan_kernel(st_ref, ap_ref, x_ref):
    wt = SSM_QPER // 4
    seglen = st_ref.shape[0] // SSM_QPER
    assert seglen & (seglen - 1) == 0
    ap = ap_ref[...]

    def coef(d, r):
        return [jnp.broadcast_to(ap[d, r:r + 1, t * LANES:(t + 1) * LANES], (SCAN_SEGS, LANES)) for t in range(wt)]

    far, fai, bar, bai = coef(0, 0), coef(0, 1), coef(1, 0), coef(1, 1)

    def scan(init, store):
        def step(i, carry):
            fr, fi, br, bi = [list(c) for c in carry]
            rf = i * SSM_QPER
            rb = (seglen - 1 - i) * SSM_QPER
            for t in range(wt):
                if store:
                    x_ref[rf + t] = fr[t]
                    x_ref[rf + wt + t] = fi[t]
                    x_ref[rb + 2 * wt + t] = br[t]
                    x_ref[rb + 3 * wt + t] = bi[t]
                pr, pi = _cmul(far[t], fai[t], fr[t], fi[t])
                fr[t] = pr + st_ref[rf + t]
                fi[t] = pi + st_ref[rf + wt + t]
                pr, pi = _cmul(bar[t], bai[t], br[t], bi[t])
                br[t] = pr + st_ref[rb + 2 * wt + t]
                bi[t] = pi + st_ref[rb + 3 * wt + t]
            return tuple(fr), tuple(fi), tuple(br), tuple(bi)

        return lax.fori_loop(0, seglen, step, init)

    z = tuple(jnp.zeros((SCAN_SEGS, LANES), F32) for _ in range(wt))
    fr, fi, br, bi = scan((z, z, z, z), store=False)

    z1 = jnp.zeros((1, LANES), F32)
    fcr, fci, bcr, bci = [], [], [], []
    for t in range(wt):
        fsr, fsi, bsr, bsi = far[t][0:1], fai[t][0:1], bar[t][0:1], bai[t][0:1]
        for _ in range(seglen.bit_length() - 1):
            fsr, fsi = _cmul(fsr, fsi, fsr, fsi)
            bsr, bsi = _cmul(bsr, bsi, bsr, bsi)
        cr, ci = z1, z1
        rs, is_ = [], []
        for k in range(SCAN_SEGS):
            rs.append(cr)
            is_.append(ci)
            cr, ci = _cmul(fsr, fsi, cr, ci)
            cr, ci = cr + fr[t][k:k + 1], ci + fi[t][k:k + 1]
        fcr.append(jnp.concatenate(rs, axis=0))
        fci.append(jnp.concatenate(is_, axis=0))
        cr, ci = z1, z1
        rs, is_ = [None] * SCAN_SEGS, [None] * SCAN_SEGS
        for k in reversed(range(SCAN_SEGS)):
            rs[k] = cr
            is_[k] = ci
            cr, ci = _cmul(bsr, bsi, cr, ci)
            cr, ci = cr + br[t][k:k + 1], ci + bi[t][k:k + 1]
        bcr.append(jnp.concatenate(rs, axis=0))
        bci.append(jnp.concatenate(is_, axis=0))

    scan((tuple(fcr), tuple(fci), tuple(bcr), tuple(bci)), store=True)


def _s5_out_kernel(yi_ref, x_ref, c2_ref, y_ref, sout_ref, *, parts):
    ncb = yi_ref.shape[0]
    part = pl.program_id(2)
    segs = SCAN_SEGS // parts
    seglen = ncb // segs

    @pl.when((pl.program_id(1) == 0) & (part == 0))
    def _():
        sout_ref[...] = _spread(c2_ref[...], c2_ref.shape[1], SSM_CW, SSM_GROUP_CH, SSM_STATE, SSM_GROUP_CH)

    for pv in range(parts):
        @pl.when(part == pv)
        def _():
            x = jnp.concatenate(
                [jnp.concatenate([x_ref[_seg_rows(q, pv * segs + kk, seglen), :] for q in range(SSM_QPER)], axis=1)
                 for kk in range(segs)], axis=0).astype(BF16)
            y = yi_ref[...] + jnp.dot(x, sout_ref[...], preferred_element_type=F32)
            for t in range(SSM_CHUNK):
                y_ref[pl.ds(t, ncb, stride=SSM_CHUNK), :] = y[:, t * SSM_TILE:(t + 1) * SSM_TILE]


def _s5(u, a2, b2, c2, apow, bsz, ncb):
    t, width = u.shape
    no = width // SSM_TILE
    nc = t // SSM_CHUNK
    ncseq = nc // bsz
    parts = ncseq // ncb
    assert SCAN_SEGS % parts == 0
    rows = ncb * SSM_CHUNK
    slab = ncseq * SSM_QPER
    yi, st = pl.pallas_call(
        functools.partial(_s5_local_kernel, parts=parts),
        grid=(no, bsz, parts),
        in_specs=[
            pl.BlockSpec((rows, SSM_TILE), lambda o, b, p: (b * parts + p, o)),
            pl.BlockSpec((None,) + a2.shape[1:], lambda o, b, p: (o, 0, 0)),
            pl.BlockSpec((None,) + b2.shape[1:], lambda o, b, p: (o, 0, 0)),
        ],
        out_specs=[
            pl.BlockSpec((None, ncb, SSM_CW), lambda o, b, p: (o, b * parts + p, 0)),
            pl.BlockSpec((None, None, slab, LANES), lambda o, b, p: (o, b, 0, 0)),
        ],
        out_shape=[jax.ShapeDtypeStruct((no, nc, SSM_CW), F32),
                   jax.ShapeDtypeStruct((no, bsz, slab, LANES), F32)],
        scratch_shapes=[pltpu.VMEM((SSM_CW, SSM_CW), BF16), pltpu.VMEM((SSM_CW, SSM_SW), BF16)],
        compiler_params=_cparams(("arbitrary", "arbitrary", "arbitrary"), 52),
        name="s5_local",
    )(u, a2, b2)
    tiles = slab // SCAN_SEGS
    xin = pl.pallas_call(
        _s5_scan_kernel,
        grid=(no, bsz),
        in_specs=[
            pl.BlockSpec((None, None, tiles, SCAN_SEGS, LANES), lambda o, b: (o, b, 0, 0, 0)),
            pl.BlockSpec((None, 2, 2, SSM_SW // 4), lambda o, b: (o, 0, 0, 0)),
        ],
        out_specs=pl.BlockSpec((None, None, tiles, SCAN_SEGS, LANES), lambda o, b: (o, b, 0, 0, 0)),
        out_shape=jax.ShapeDtypeStruct((no, bsz, tiles, SCAN_SEGS, LANES), F32),
        compiler_params=_cparams(("arbitrary", "arbitrary"), 48),
        name="s5_scan",
    )(st.reshape(no, bsz, tiles, SCAN_SEGS, LANES), apow)
    return pl.pallas_call(
        functools.partial(_s5_out_kernel, parts=parts),
        grid=(no, bsz, parts),
        in_specs=[
            pl.BlockSpec((None, ncb, SSM_CW), lambda o, b, p: (o, b * parts + p, 0)),
            pl.BlockSpec((None, None, slab, LANES), lambda o, b, p: (o, b, 0, 0)),
            pl.BlockSpec((None,) + c2.shape[1:], lambda o, b, p: (o, 0, 0)),
        ],
        out_specs=pl.BlockSpec((rows, SSM_TILE), lambda o, b, p: (b * parts + p, o)),
        out_shape=jax.ShapeDtypeStruct((t, width), F32),
        scratch_shapes=[pltpu.VMEM((SSM_SW, SSM_CW), BF16)],
        compiler_params=_cparams(("arbitrary", "arbitrary", "arbitrary"), 52),
        name="s5_out",
    )(yi, xin.reshape(no, bsz, slab, LANES), c2)


def _merge_kernel(a_ref, y_ref, g0_ref, g1_ref, x_ref, wglu_ref, wb0_ref, wb1_ref, wout_ref, o_ref):
    y = y_ref[...]
    s = 0.5 * y * (1.0 + lax.erf(y * (2.0 ** -0.5)))
    glu = jnp.dot(s.astype(BF16), wglu_ref[...], preferred_element_type=F32)
    s = s * jax.nn.sigmoid(glu)
    pa = jnp.dot(a_ref[...], wb0_ref[...], preferred_element_type=F32)
    ps = jnp.dot(s.astype(BF16), wb1_ref[...], preferred_element_type=F32)
    merged = g0_ref[...].astype(F32) * pa + g1_ref[...].astype(F32) * ps
    o_ref[...] = x_ref[...] + jnp.dot(merged.astype(BF16), wout_ref[...], preferred_element_type=F32)


def _merge(a, y, proj, x, wglu, wbr, wout, layer, bm):
    t, d = x.shape
    aw = a.shape[1]
    g0 = 4 * SEG // d
    return pl.pallas_call(
        _merge_kernel,
        grid=(t // bm,),
        in_specs=[
            pl.BlockSpec((bm, aw), lambda i: (i, 0)),
            pl.BlockSpec((bm, aw), lambda i: (i, 0)),
            pl.BlockSpec((bm, d), lambda i: (i, g0)),
            pl.BlockSpec((bm, d), lambda i: (i, g0 + 1)),
            pl.BlockSpec((bm, d), lambda i: (i, 0)),
            pl.BlockSpec((None,) + wglu.shape[1:], lambda i: (layer, 0, 0)),
            pl.BlockSpec((None, None) + wbr.shape[2:], lambda i: (layer, 0, 0, 0)),
            pl.BlockSpec((None, None) + wbr.shape[2:], lambda i: (layer, 1, 0, 0)),
            pl.BlockSpec((None,) + wout.shape[1:], lambda i: (layer, 0, 0)),
        ],
        out_specs=pl.BlockSpec((bm, d), lambda i: (i, 0)),
        out_shape=jax.ShapeDtypeStruct((t, d), F32),
        compiler_params=_cparams(("arbitrary",), 48),
        name="merge",
    )(a, y, proj, proj, x, wglu, wbr, wbr, wout)


def _swiglu_partial(h, w1_ref, w3_ref, w2_ref, sub):
    ffc = w1_ref.shape[-1]
    acc = None
    for s in range(0, ffc, sub):
        e = min(s + sub, ffc)
        a = jnp.dot(h, w1_ref[:, s:e], preferred_element_type=F32)
        b = jnp.dot(h, w3_ref[:, s:e], preferred_element_type=F32)
        tt = (a * jax.nn.sigmoid(a) * b).astype(BF16)
        c = jnp.dot(tt, w2_ref[s:e, :], preferred_element_type=F32)
        acc = c if acc is None else acc + c
    return acc


def _ffn_kernel(x_ref, g_ref, w1_ref, w3_ref, w2_ref, o_ref, hn_ref, *, sub):
    f = pl.program_id(1)

    @pl.when(f == 0)
    def _():
        x = x_ref[...]
        hn_ref[...] = _rms(x, g_ref[...]).astype(BF16)
        o_ref[...] = x

    o_ref[...] += _swiglu_partial(hn_ref[...], w1_ref, w3_ref, w2_ref, sub)


def _ffn_tile(dff):
    return dff // 2 if (dff // 2) % LANES == 0 else dff


def _ffn(x, g, w1, w3, w2, li, bm):
    t, d = x.shape
    dff = w1.shape[2]
    ffc = _ffn_tile(dff)
    return pl.pallas_call(
        functools.partial(_ffn_kernel, sub=2 * LANES),
        grid=(t // bm, dff // ffc),
        in_specs=[
            pl.BlockSpec((bm, d), lambda i, f: (i, 0)),
            pl.BlockSpec((1, d), lambda i, f: (0, 0)),
            pl.BlockSpec((None, d, ffc), lambda i, f: (li, 0, f)),
            pl.BlockSpec((None, d, ffc), lambda i, f: (li, 0, f)),
            pl.BlockSpec((None, ffc, d), lambda i, f: (li, f, 0)),
        ],
        out_specs=pl.BlockSpec((bm, d), lambda i, f: (i, 0)),
        out_shape=jax.ShapeDtypeStruct((t, d), F32),
        scratch_shapes=[pltpu.VMEM((bm, d), BF16)],
        compiler_params=_cparams(("arbitrary", "arbitrary"), 56),
        name="ffn_dense",
    )(x, g, w1, w3, w2)


def _router_kernel(x_ref, g_ref, wr_ref, hn_ref, r_ref):
    hn = _rms(x_ref[...], g_ref[...])
    _to_row_tiles(hn_ref, hn)
    logits = jnp.dot(hn, wr_ref[...], preferred_element_type=F32, precision=lax.Precision.HIGHEST)
    lane = lax.broadcasted_iota(I32, logits.shape, 1)
    neg = jnp.float32(-1e30)
    logits = jnp.where(lane < N_EXPERTS, logits, neg)
    m1 = jnp.max(logits, axis=-1, keepdims=True)
    i1 = jnp.min(jnp.where(logits == m1, lane, LANES), axis=-1, keepdims=True)
    rest = jnp.where(lane == i1, neg, logits)
    m2 = jnp.max(rest, axis=-1, keepdims=True)
    i2 = jnp.min(jnp.where(rest == m2, lane, LANES), axis=-1, keepdims=True)
    e2 = jnp.exp(m2 - m1)
    g1 = 1.0 / (1.0 + e2)
    g2 = e2 / (1.0 + e2)
    r_ref[...] = jnp.where(lane == 0, i1.astype(F32),
                           jnp.where(lane == 1, i2.astype(F32),
                                     jnp.where(lane == 2, g1, jnp.where(lane == 3, g2, 0.0))))


def _router(x, g, wr_pad, bm):
    t, d = x.shape
    s_per = d // LANES
    return pl.pallas_call(
        _router_kernel,
        grid=(t // bm,),
        in_specs=[
            pl.BlockSpec((bm, d), lambda i: (i, 0)),
            pl.BlockSpec((1, d), lambda i: (0, 0)),
            pl.BlockSpec((d, LANES), lambda i: (0, 0)),
        ],
        out_specs=[pl.BlockSpec((bm * s_per, LANES), lambda i: (i, 0)), pl.BlockSpec((bm, LANES), lambda i: (i, 0))],
        out_shape=[jax.ShapeDtypeStruct((t * s_per, LANES), F32), jax.ShapeDtypeStruct((t, LANES), F32)],
        compiler_params=_cparams(("arbitrary",), 40),
        name="moe_router",
    )(x, g, wr_pad)


def _row_copy(src_hbm, dst_ref, sem, src_row, dst_row, s_per):
    src = pl.multiple_of(src_row * s_per, s_per)
    dst = pl.multiple_of(dst_row * s_per, s_per)
    return pltpu.make_async_copy(src_hbm.at[pl.ds(src, s_per)], dst_ref.at[pl.ds(dst, s_per)], sem)


def _expert_kernel(be_ref, tok_ref, hn_hbm, w1_ref, w3_ref, w2_ref, o_ref, rows_ref, xb_ref, acc_ref, sem,
                   *, sub, s_per):
    i = pl.program_id(0)
    f = pl.program_id(1)
    bm = xb_ref.shape[0]

    def issue(step, slot):
        base = step * bm

        def body(r, c):
            _row_copy(hn_hbm, rows_ref.at[slot], sem.at[slot], tok_ref[base + r], r, s_per).start()
            return c

        lax.fori_loop(0, bm, body, 0, unroll=8)

    def drain(slot):
        def body(r, c):
            _row_copy(hn_hbm, rows_ref.at[slot], sem.at[slot], 0, r, s_per).wait()
            return c

        lax.fori_loop(0, bm, body, 0, unroll=8)

    @pl.when((f == 0) & (i == 0))
    def _():
        issue(0, 0)

    for slot in range(2):
        @pl.when((f == 0) & (i % 2 == slot))
        def _():
            @pl.when(i + 1 < pl.num_programs(0))
            def _():
                issue(i + 1, 1 - slot)

            drain(slot)
            xb_ref[...] = _from_row_tiles(rows_ref.at[slot], s_per).astype(BF16)

    part = _swiglu_partial(xb_ref[...], w1_ref, w3_ref, w2_ref, sub)

    @pl.when(f == 0)
    def _():
        acc_ref[...] = part

    @pl.when((f > 0) & (f < pl.num_programs(1) - 1))
    def _():
        acc_ref[...] += part

    @pl.when(f == pl.num_programs(1) - 1)
    def _():
        _to_row_tiles(o_ref, acc_ref[...] + part)


def _experts(block_exp, slot_tok, hn, w1, w3, w2, li, bm):
    d = w1.shape[2]
    dff = w1.shape[3]
    s_per = d // LANES
    n = slot_tok.shape[0]
    ffc = _ffn_tile(dff)
    assert dff // ffc >= 2
    return pl.pallas_call(
        functools.partial(_expert_kernel, sub=2 * LANES, s_per=s_per),
        grid_spec=pltpu.PrefetchScalarGridSpec(
            num_scalar_prefetch=2,
            grid=(n // bm, dff // ffc),
            in_specs=[
                pl.BlockSpec(memory_space=pl.ANY),
                pl.BlockSpec((None, None, d, ffc), lambda i, f, be, tok: (li, be[i], 0, f)),
                pl.BlockSpec((None, None, d, ffc), lambda i, f, be, tok: (li, be[i], 0, f)),
                pl.BlockSpec((None, None, ffc, d), lambda i, f, be, tok: (li, be[i], f, 0)),
            ],
            out_specs=pl.BlockSpec((bm * s_per, LANES), lambda i, f, be, tok: (i, 0)),
            scratch_shapes=[pltpu.VMEM((2, bm * s_per, LANES), F32), pltpu.VMEM((bm, d), BF16),
                            pltpu.VMEM((bm, d), F32), pltpu.SemaphoreType.DMA((2,))],
        ),
        out_shape=jax.ShapeDtypeStruct((n * s_per, LANES), F32),
        compiler_params=_cparams(("arbitrary", "arbitrary"), 56),
        name="moe_experts",
    )(block_exp, slot_tok, hn, w1, w3, w2)


def _combine_kernel(d0_ref, d1_ref, yb_hbm, x_ref, r_ref, o_ref, r0_ref, r1_ref, sem, *, s_per):
    bm = o_ref.shape[0]
    i = pl.program_id(0)

    def issue(step, slot):
        base = step * bm

        def body(r, c):
            _row_copy(yb_hbm, r0_ref.at[slot], sem.at[slot], d0_ref[base + r], r, s_per).start()
            _row_copy(yb_hbm, r1_ref.at[slot], sem.at[slot], d1_ref[base + r], r, s_per).start()
            return c

        lax.fori_loop(0, bm, body, 0, unroll=8)

    def drain(slot):
        def body(r, c):
            _row_copy(yb_hbm, r0_ref.at[slot], sem.at[slot], 0, r, s_per).wait()
            _row_copy(yb_hbm, r1_ref.at[slot], sem.at[slot], 0, r, s_per).wait()
            return c

        lax.fori_loop(0, bm, body, 0, unroll=8)

    @pl.when(i == 0)
    def _():
        issue(0, 0)

    for slot in range(2):
        @pl.when(i % 2 == slot)
        def _():
            @pl.when(i + 1 < pl.num_programs(0))
            def _():
                issue(i + 1, 1 - slot)

            drain(slot)
            route = r_ref[...]
            o_ref[...] = (x_ref[...] + route[:, 2:3] * _from_row_tiles(r0_ref.at[slot], s_per)
                          + route[:, 3:4] * _from_row_tiles(r1_ref.at[slot], s_per))


def _combine(d0, d1, yb, x, route, bm):
    t, d = x.shape
    s_per = d // LANES
    return pl.pallas_call(
        functools.partial(_combine_kernel, s_per=s_per),
        grid_spec=pltpu.PrefetchScalarGridSpec(
            num_scalar_prefetch=2,
            grid=(t // bm,),
            in_specs=[
                pl.BlockSpec(memory_space=pl.ANY),
                pl.BlockSpec((bm, d), lambda i, a, b: (i, 0)),
                pl.BlockSpec((bm, LANES), lambda i, a, b: (i, 0)),
            ],
            out_specs=pl.BlockSpec((bm, d), lambda i, a, b: (i, 0)),
            scratch_shapes=[pltpu.VMEM((2, bm * s_per, LANES), F32), pltpu.VMEM((2, bm * s_per, LANES), F32),
                            pltpu.SemaphoreType.DMA((2,))],
        ),
        out_shape=jax.ShapeDtypeStruct((t, d), F32),
        compiler_params=_cparams(("arbitrary",), 32),
        name="moe_combine",
    )(d0, d1, yb, x, route)


def _dispatch_tables(idx, bm):
    t = idx.shape[0]
    flat_e = idx.reshape(-1)
    onehot = (flat_e[:, None] == jnp.arange(N_EXPERTS, dtype=I32)[None, :]).astype(I32)
    csum = jnp.cumsum(onehot, axis=0)
    rank = jnp.take_along_axis(csum, flat_e[:, None], axis=1)[:, 0] - 1
    counts = csum[-1]
    padded = (counts + bm - 1) // bm * bm
    pad_end = jnp.cumsum(padded)
    pad_start = pad_end - padded
    dest = (pad_start[flat_e] + rank).astype(I32)
    n_slots = t * TOP_K + N_EXPERTS * bm
    slot_tok = jnp.zeros((n_slots,), I32).at[dest].set(jnp.arange(t * TOP_K, dtype=I32) // TOP_K)
    block_start = jnp.arange(n_slots // bm, dtype=I32) * bm
    block_exp = jnp.minimum(jnp.searchsorted(pad_end, block_start, side='right'), N_EXPERTS - 1).astype(I32)
    return dest.reshape(t, TOP_K), slot_tok, block_exp


def _moe(x, g, w_router, w1, w3, w2, li, bm_tok, bm_slot):
    d = x.shape[1]
    wr_pad = jnp.zeros((d, LANES), F32).at[:, :N_EXPERTS].set(w_router)
    hn, route = _router(x, g, wr_pad, bm_tok)
    idx = route[:, :TOP_K].astype(I32)
    dest, slot_tok, block_exp = _dispatch_tables(idx, bm_slot)
    yb = _experts(block_exp, slot_tok, hn, w1, w3, w2, li, bm_slot)
    return _combine(dest[:, 0], dest[:, 1], yb, x, route, bm_tok)


def _norm_kernel(x_ref, g_ref, o_ref):
    o_ref[...] = _rms(x_ref[...], g_ref[...])


def _final_norm(x, g, bm):
    t, d = x.shape
    return pl.pallas_call(
        _norm_kernel,
        grid=(t // bm,),
        in_specs=[pl.BlockSpec((bm, d), lambda i: (i, 0)), pl.BlockSpec((1, d), lambda i: (0, 0))],
        out_specs=pl.BlockSpec((bm, d), lambda i: (i, 0)),
        out_shape=jax.ShapeDtypeStruct((t, d), F32),
        compiler_params=_cparams(("arbitrary",), 32),
        name="final_norm",
    )(x, g)


def _rope_tables(seq):
    half = ATTN_HEAD_DIM // 2
    inv_freq = 1.0 / (ROPE_THETA ** (jnp.arange(half, dtype=F32) / half))
    ang = jnp.arange(seq, dtype=F32)[:, None] * inv_freq[None, :]
    cos = jnp.tile(jnp.cos(ang), (1, 4))
    sin = jnp.sin(ang)
    return cos, jnp.concatenate([-sin, -sin, sin, sin], axis=1)


def _permute_qk_columns(w):
    nl, d = w.shape[:2]
    half = ATTN_HEAD_DIM // 2
    qk = w[..., :2 * SEG].reshape(nl, d, 2, ATTN_HEADS, 2, 2, half).transpose(0, 1, 2, 3, 5, 4, 6)
    return jnp.concatenate([qk.reshape(nl, d, 2 * SEG), w[..., 2 * SEG:]], axis=2)


def _block(n, pref):
    return pref if n % pref == 0 else n


def kernel(x, norm_mix, w_in, attn_lambda, attn_subln, ssm_lam_re, ssm_lam_im, ssm_log_dt, ssm_b_re, ssm_b_im, ssm_c_re, ssm_c_im, ssm_d, w_glu, w_branch, w_out, norm_ffn, ffn_w1, ffn_w3, ffn_w2, moe_router, moe_w1, moe_w3, moe_w2, norm_final):
    bsz, seq, d = x.shape
    depth = w_in.shape[0]
    t = bsz * seq
    cos_t, sin_t = _rope_tables(seq)
    xf = x.reshape(t, d)
    bm = _block(seq, 1024)
    w_all = _permute_qk_columns(w_in).astype(BF16)
    a2, b2, c2, apow = jax.vmap(_s5_tables)(ssm_lam_re, ssm_lam_im, ssm_log_dt, ssm_b_re, ssm_b_im,
                                            ssm_c_re, ssm_c_im, ssm_d)
    wglu, wbr, wout = w_glu.astype(BF16), w_branch.astype(BF16), w_out.astype(BF16)
    fw1, fw3, fw2 = ffn_w1.astype(BF16), ffn_w3.astype(BF16), ffn_w2.astype(BF16)
    mw1, mw3, mw2 = moe_w1.astype(BF16), moe_w3.astype(BF16), moe_w2.astype(BF16)
    for layer in range(depth):
        proj, u = _inproj(xf, norm_mix[layer][None], w_all, layer, cos_t, sin_t, seq, _block(seq, 512))

        lam_init = jnp.full((1,), 0.8 - 0.6 * math.exp(-0.3 * layer), F32)
        a = _attention(proj, lam_init, attn_lambda[layer], attn_subln[layer][None], bsz, seq,
                       _block(seq, 1024), _block(seq // 2, 512))

        y = _s5(u, a2[layer], b2[layer], c2[layer], apow[layer], bsz, _block(seq // SSM_CHUNK, 512))

        xf = _merge(a, y, proj, xf, wglu, wbr, wout, layer, _block(seq, 512))

        i = layer // 2
        if layer % 2 == 0:
            xf = _ffn(xf, norm_ffn[layer][None], fw1, fw3, fw2, i, bm)
        else:
            xf = _moe(xf, norm_ffn[layer][None], moe_router[i], mw1, mw3, mw2, i, _block(seq, 256), 512)
    return _final_norm(xf, norm_final[None], bm).reshape(bsz, seq, d)
```

```python
import functools
import math

import jax
import jax.numpy as jnp
from jax import lax
from jax.experimental import pallas as pl
from jax.experimental.pallas import tpu as pltpu

F32 = jnp.float32
BF16 = jnp.bfloat16
I32 = jnp.int32

RMS_EPS = 1e-6
ROPE_THETA = 10000.0
LOG2E = 1.4426950408889634

LANES = 128
SUBLANES = 8
VMEM_BYTES_V7X = 64 * 1024 * 1024

ATTN_HEADS = 4
ATTN_HEAD_DIM = 64
HEAD_SLAB = 2 * ATTN_HEAD_DIM
SSM_GROUP_CH = 16
SSM_STATE = 64
SSM_OCTET = LANES // SSM_GROUP_CH
SSM_CHUNK = 8
N_EXPERTS = 8
TOP_K = 2
SEG = 512


def _cparams(sem, vmem_mb):
    return pltpu.CompilerParams(dimension_semantics=sem, vmem_limit_bytes=vmem_mb * 1024 * 1024)


def _rms(x, g):
    return x * lax.rsqrt(jnp.mean(x * x, axis=-1, keepdims=True) + RMS_EPS) * g


def _sigmoid(x):
    return 0.5 * jnp.tanh(0.5 * x) + 0.5


def _inproj_kernel(x_ref, g_ref, w_ref, cos_ref, sin_ref, o_ref, u_ref, *, qscale):
    xn = _rms(x_ref[...], g_ref[...]).astype(BF16)
    cos = cos_ref[...]
    sin = sin_ref[...]

    def rot(a):
        parts = []
        for s in range(0, a.shape[1], HEAD_SLAB):
            blk = a[:, s:s + HEAD_SLAB]
            parts.append(blk * cos + pltpu.roll(blk, HEAD_SLAB // 2, 1) * sin)
        return jnp.concatenate(parts, axis=1)

    for j in range(w_ref.shape[1] // SEG):
        acc = jnp.dot(xn, w_ref[:, j * SEG:(j + 1) * SEG], preferred_element_type=F32)
        if j == 0:
            acc = rot(acc) * qscale
        elif j == 1:
            acc = rot(acc)
        elif j == 3:
            u_ref[...] = acc
        elif j >= 4:
            acc = _sigmoid(acc)
        o_ref[:, j * SEG:(j + 1) * SEG] = acc.astype(BF16)


def _inproj(x, g, w, layer, cos_t, sin_t, seq, bm):
    t, d = x.shape
    n = w.shape[2]
    qscale = ATTN_HEAD_DIM ** -0.5 * LOG2E
    nseq = seq // bm
    return pl.pallas_call(
        functools.partial(_inproj_kernel, qscale=qscale),
        grid=(t // bm,),
        in_specs=[
            pl.BlockSpec((bm, d), lambda i: (i, 0)),
            pl.BlockSpec((1, d), lambda i: (0, 0)),
            pl.BlockSpec((None, d, n), lambda i: (layer, 0, 0)),
            pl.BlockSpec((bm, HEAD_SLAB), lambda i: (i % nseq, 0)),
            pl.BlockSpec((bm, HEAD_SLAB), lambda i: (i % nseq, 0)),
        ],
        out_specs=[pl.BlockSpec((bm, n), lambda i: (i, 0)), pl.BlockSpec((bm, SEG), lambda i: (i, 0))],
        out_shape=[jax.ShapeDtypeStruct((t, n), BF16), jax.ShapeDtypeStruct((t, SEG), F32)],
        compiler_params=_cparams(("arbitrary",), 52),
        name="inproj",
    )(x, g, w, cos_t, sin_t)


ONES_ROWS = 16


def _attn_kernel(sc_ref, lp_ref, q_ref, k_ref, v_ref, g_ref, o_ref, vt_ref, acc_ref, m_ref,
                 s0_ref, s1_ref, p0_ref, p1_ref, x0_ref, x1_ref, *, bk):
    qi = pl.program_id(2)
    seq = k_ref.shape[0]
    bq = q_ref.shape[0]
    nkv = seq // bk
    assert nkv >= 2 and nkv % 2 == 0

    @pl.when(qi == 0)
    def _():
        for c in range(nkv):
            vt_ref[c, :HEAD_SLAB, :] = v_ref[c * bk:(c + 1) * bk, :].astype(F32).T.astype(BF16)
            vt_ref[c, HEAD_SLAB:, :] = jnp.ones((ONES_ROWS, bk), BF16)

    qt = q_ref[...].astype(F32).T
    row = lax.broadcasted_iota(I32, qt.shape, 0)
    half0 = (row % ATTN_HEAD_DIM) < (ATTN_HEAD_DIM // 2)
    qst = jnp.concatenate([jnp.where(half0, qt, 0.0), jnp.where(half0, 0.0, qt)], axis=1).astype(BF16)

    acc_ref[...] = jnp.zeros_like(acc_ref)
    m_ref[...] = jnp.full_like(m_ref, -1e30)
    s_refs = (s0_ref, s1_ref)
    p_refs = (p0_ref, p1_ref)
    x_refs = (x0_ref, x1_ref)

    def qk(j, cur):
        off = pl.multiple_of(j * bk, bk)
        st = jnp.dot(k_ref[pl.ds(off, bk), :], qst, preferred_element_type=F32)
        s_refs[cur][...] = st
        x_refs[cur][...] = jnp.max(st, axis=0, keepdims=True)

    def pv(j, cur, alpha):
        acc_ref[...] = acc_ref[...] * alpha + jnp.dot(vt_ref[j], p_refs[cur][...],
                                                      preferred_element_type=F32)

    def softmax(cur):
        m_prev = m_ref[...]
        m_new = jnp.maximum(m_prev, x_refs[cur][...])
        p_refs[cur][...] = jnp.exp2((s_refs[cur][...] - m_new).astype(BF16))
        m_ref[...] = m_new
        return jnp.exp2(m_prev - m_new)

    def stage(j, cur, alpha):
        qk(j + 1, 1 - cur)
        pv(j - 1, 1 - cur, alpha)
        return softmax(cur)

    qk(0, 0)
    qk(1, 1)
    alpha = softmax(0)

    def pair(jj, alpha):
        j = 1 + 2 * jj
        return stage(j + 1, 0, stage(j, 1, alpha))

    alpha = lax.fori_loop(0, (nkv - 2) // 2, pair, alpha)
    pv(nkv - 2, 0, alpha)
    alpha = softmax(1)
    pv(nkv - 1, 1, alpha)

    lam_init = sc_ref[0]
    lp = lp_ref[...]
    lam = (jnp.exp(jnp.sum(lp[0:1] * lp[1:2], axis=-1, keepdims=True))
           - jnp.exp(jnp.sum(lp[2:3] * lp[3:4], axis=-1, keepdims=True)) + lam_init)
    acc = acc_ref[...]
    o = acc[:HEAD_SLAB] / acc[HEAD_SLAB:HEAD_SLAB + 1]
    a = o[:, :bq] - lam * o[:, bq:]
    a = a * lax.rsqrt(jnp.mean(a * a, axis=0, keepdims=True) + RMS_EPS) * (1.0 - lam_init)
    o_ref[...] = (a.T * g_ref[...]).astype(BF16)


def _attention(proj, lam_init, lam_p, subln, bsz, seq, bq, bk):
    t = proj.shape[0]
    nq = seq // bq
    koff = SEG // HEAD_SLAB
    return pl.pallas_call(
        functools.partial(_attn_kernel, bk=bk),
        grid=(bsz, ATTN_HEADS, nq),
        in_specs=[
            pl.BlockSpec(memory_space=pltpu.SMEM),
            pl.BlockSpec((4, ATTN_HEAD_DIM), lambda b, h, i: (0, 0)),
            pl.BlockSpec((bq, HEAD_SLAB), lambda b, h, i: (b * nq + i, h)),
            pl.BlockSpec((seq, HEAD_SLAB), lambda b, h, i: (b, koff + h)),
            pl.BlockSpec((seq, HEAD_SLAB), lambda b, h, i: (b, 2 * koff + h)),
            pl.BlockSpec((1, HEAD_SLAB), lambda b, h, i: (0, 0)),
        ],
        out_specs=pl.BlockSpec((bq, HEAD_SLAB), lambda b, h, i: (b * nq + i, h)),
        out_shape=jax.ShapeDtypeStruct((t, ATTN_HEADS * HEAD_SLAB), BF16),
        scratch_shapes=[
            pltpu.VMEM((seq // bk, HEAD_SLAB + ONES_ROWS, bk), BF16),
            pltpu.VMEM((HEAD_SLAB + ONES_ROWS, 2 * bq), F32),
            pltpu.VMEM((1, 2 * bq), F32),
            pltpu.VMEM((bk, 2 * bq), F32),
            pltpu.VMEM((bk, 2 * bq), F32),
            pltpu.VMEM((bk, 2 * bq), BF16),
            pltpu.VMEM((bk, 2 * bq), BF16),
            pltpu.VMEM((1, 2 * bq), F32),
            pltpu.VMEM((1, 2 * bq), F32),
        ],
        compiler_params=_cparams(("arbitrary", "arbitrary", "arbitrary"), 56),
        name="diff_attention",
    )(lam_init, lam_p, proj, proj, proj, subln)


SSM_TILE = SSM_OCTET * SSM_GROUP_CH
SSM_CW = SSM_CHUNK * SSM_TILE
SSM_SW = 4 * SSM_OCTET * SSM_STATE
SSM_QPER = SSM_SW // LANES
SCAN_SEGS = SUBLANES


def _s5_tables(lam_re, lam_im, log_dt, b_re, b_im, c_re, c_im, d_skip):
    hp = lax.Precision.HIGHEST
    ll = SSM_CHUNK
    g = lam_re.shape[1]
    no = g // SSM_OCTET
    lam = lax.complex(lam_re, lam_im)
    ldt = lam * jnp.exp(log_dt)[..., None]
    a = jnp.exp(ldt)
    bbar = ((a - 1.0) / lam)[..., None] * lax.complex(b_re, b_im)
    cmat = lax.complex(c_re, c_im)
    n = jnp.arange(ll + 1, dtype=F32)
    pw = jnp.exp(ldt[:, None] * n[None, :, None, None])

    def kern(d):
        return jnp.real(jnp.einsum('ghp,dgp,gpk->dghk', cmat[d], pw[d, :ll], bbar[d], precision=hp))

    kf, kb = kern(0), kern(1)
    tt = jnp.arange(ll)
    lag = tt[None, :] - tt[:, None]
    mf = jnp.where((lag >= 0)[:, :, None, None, None], kf[jnp.clip(lag, 0, ll - 1)], 0.0)
    mb = jnp.where((lag <= 0)[:, :, None, None, None], kb[jnp.clip(-lag, 0, ll - 1)], 0.0)
    skip = (jnp.eye(ll, dtype=F32)[:, :, None, None, None]
            * (d_skip.reshape(g, SSM_GROUP_CH)[None, None, :, :, None] * jnp.eye(SSM_GROUP_CH, dtype=F32)[None, None, None]))
    m = (mf + mb + skip).reshape(ll, ll, no, SSM_OCTET, SSM_GROUP_CH, SSM_GROUP_CH)
    a2 = m.transpose(2, 0, 3, 5, 1, 4).reshape(no, SSM_CW, ll * SSM_GROUP_CH)

    ein = jnp.stack([pw[0, ll - 1 - tt][..., None] * bbar[0][None],
                     pw[1, tt][..., None] * bbar[1][None]])
    ein = jnp.stack([jnp.real(ein), jnp.imag(ein)], axis=1)
    ein = ein.reshape(2, 2, ll, no, SSM_OCTET, SSM_STATE, SSM_GROUP_CH)
    b2 = ein.transpose(3, 2, 4, 6, 0, 1, 5).reshape(no, SSM_CW, 4 * SSM_STATE)

    eout = jnp.stack([cmat[0][None] * pw[0, tt + 1][:, :, None, :],
                      cmat[1][None] * pw[1, ll - tt][:, :, None, :]])
    eout = jnp.stack([jnp.real(eout), -jnp.imag(eout)], axis=1)
    eout = eout.reshape(2, 2, ll, no, SSM_OCTET, SSM_GROUP_CH, SSM_STATE)
    c2 = eout.transpose(3, 0, 1, 4, 6, 2, 5).reshape(no, SSM_SW, ll * SSM_GROUP_CH)

    al = pw[:, ll]
    apow = jnp.stack([jnp.real(al), jnp.imag(al)], axis=1)
    apow = apow.reshape(2, 2, no, SSM_OCTET * SSM_STATE).transpose(2, 0, 1, 3)
    return a2.astype(BF16), b2.astype(BF16), c2.astype(BF16), apow.astype(F32)


def _to_row_tiles(ref, x):
    n = x.shape[0]
    s_per = x.shape[1] // LANES
    for s in range(s_per):
        ref[pl.ds(s, n, stride=s_per), :] = x[:, s * LANES:(s + 1) * LANES]


def _from_row_tiles(ref, s_per):
    n = ref.shape[0] // s_per
    return jnp.concatenate([ref[pl.ds(s, n, stride=s_per), :] for s in range(s_per)], axis=1)


def _shr(idx, n):
    assert n & (n - 1) == 0
    return idx >> (n.bit_length() - 1)


def _group_of(idx, width):
    return _shr(idx, width) & (SSM_OCTET - 1)


def _spread(compact, rep_rows, rep_cols, inner, row_w, col_w):
    q = lax.broadcasted_iota(I32, (rep_rows, rep_cols), 0)
    c = lax.broadcasted_iota(I32, (rep_rows, rep_cols), 1)
    rep = ((_shr(q, inner) == _shr(c, inner * SSM_OCTET)) & ((q & (inner - 1)) == (c & (inner - 1)))).astype(BF16)
    full = jnp.dot(compact, rep, preferred_element_type=F32)
    r = lax.broadcasted_iota(I32, full.shape, 0)
    cc = lax.broadcasted_iota(I32, full.shape, 1)
    return jnp.where(_group_of(r, row_w) == _group_of(cc, col_w), full, 0.0).astype(BF16)


def _seg_rows(q, k, seglen):
    return pl.ds(q * SCAN_SEGS + k, seglen, stride=SSM_QPER * SCAN_SEGS)


def _s5_local_kernel(u_ref, a2_ref, b2_ref, yi_ref, st_ref, toe_ref, sin_ref, *, parts):
    ncb = yi_ref.shape[0]
    part = pl.program_id(2)
    segs = SCAN_SEGS // parts
    seglen = ncb // segs

    @pl.when((pl.program_id(1) == 0) & (part == 0))
    def _():
        toe_ref[...] = _spread(a2_ref[...], a2_ref.shape[1], SSM_CW, SSM_GROUP_CH, SSM_GROUP_CH, SSM_GROUP_CH)
        sin_ref[...] = _spread(b2_ref[...], b2_ref.shape[1], SSM_SW, SSM_STATE, SSM_GROUP_CH, SSM_STATE)

    u = jnp.concatenate([u_ref[pl.ds(j, ncb, stride=SSM_CHUNK), :] for j in range(SSM_CHUNK)],
                        axis=1).astype(BF16)
    yi_ref[...] = jnp.dot(u, toe_ref[...], preferred_element_type=F32)
    st = jnp.dot(u, sin_ref[...], preferred_element_type=F32)
    for pv in range(parts):
        @pl.when(part == pv)
        def _():
            for kk in range(segs):
                for q in range(SSM_QPER):
                    st_ref[_seg_rows(q, pv * segs + kk, seglen), :] = (
                        st[kk * seglen:(kk + 1) * seglen, q * LANES:(q + 1) * LANES])


def _cmul(ar, ai, xr, xi):
    return ar * xr - ai * xi, ar * xi + ai * xr


def _s5_scan_kernel(st_ref, ap_ref, x_ref):
    wt = SSM_QPER // 4
    seglen = st_ref.shape[0] // SSM_QPER
    assert seglen & (seglen - 1) == 0
    ap = ap_ref[...]

    def coef(d, r):
        return [jnp.broadcast_to(ap[d, r:r + 1, t * LANES:(t + 1) * LANES], (SCAN_SEGS, LANES)) for t in range(wt)]

    far, fai, bar, bai = coef(0, 0), coef(0, 1), coef(1, 0), coef(1, 1)

    def scan(init, store):
        def step(i, carry):
            fr, fi, br, bi = [list(c) for c in carry]
            rf = i * SSM_QPER
            rb = (seglen - 1 - i) * SSM_QPER
            for t in range(wt):
                if store:
                    x_ref[rf + t] = fr[t]
                    x_ref[rf + wt + t] = fi[t]
                    x_ref[rb + 2 * wt + t] = br[t]
                    x_ref[rb + 3 * wt + t] = bi[t]
                pr, pi = _cmul(far[t], fai[t], fr[t], fi[t])
                fr[t] = pr + st_ref[rf + t]
                fi[t] = pi + st_ref[rf + wt + t]
                pr, pi = _cmul(bar[t], bai[t], br[t], bi[t])
                br[t] = pr + st_ref[rb + 2 * wt + t]
                bi[t] = pi + st_ref[rb + 3 * wt + t]
            return tuple(fr), tuple(fi), tuple(br), tuple(bi)

        return lax.fori_loop(0, seglen, step, init)

    z = tuple(jnp.zeros((SCAN_SEGS, LANES), F32) for _ in range(wt))
    fr, fi, br, bi = scan((z, z, z, z), store=False)

    z1 = jnp.zeros((1, LANES), F32)
    fcr, fci, bcr, bci = [], [], [], []
    for t in range(wt):
        fsr, fsi, bsr, bsi = far[t][0:1], fai[t][0:1], bar[t][0:1], bai[t][0:1]
        for _ in range(seglen.bit_length() - 1):
            fsr, fsi = _cmul(fsr, fsi, fsr, fsi)
            bsr, bsi = _cmul(bsr, bsi, bsr, bsi)
        cr, ci = z1, z1
        rs, is_ = [], []
        for k in range(SCAN_SEGS):
            rs.append(cr)
            is_.append(ci)
            cr, ci = _cmul(fsr, fsi, cr, ci)
            cr, ci = cr + fr[t][k:k + 1], ci + fi[t][k:k + 1]
        fcr.append(jnp.concatenate(rs, axis=0))
        fci.append(jnp.concatenate(is_, axis=0))
        cr, ci = z1, z1
        rs, is_ = [None] * SCAN_SEGS, [None] * SCAN_SEGS
        for k in reversed(range(SCAN_SEGS)):
            rs[k] = cr
            is_[k] = ci
            cr, ci = _cmul(bsr, bsi, cr, ci)
            cr, ci = cr + br[t][k:k + 1], ci + bi[t][k:k + 1]
        bcr.append(jnp.concatenate(rs, axis=0))
        bci.append(jnp.concatenate(is_, axis=0))

    scan((tuple(fcr), tuple(fci), tuple(bcr), tuple(bci)), store=True)


def _s5_out_kernel(yi_ref, x_ref, c2_ref, y_ref, sout_ref, *, parts):
    ncb = yi_ref.shape[0]
    part = pl.program_id(2)
    segs = SCAN_SEGS // parts
    seglen = ncb // segs

    @pl.when((pl.program_id(1) == 0) & (part == 0))
    def _():
        sout_ref[...] = _spread(c2_ref[...], c2_ref.shape[1], SSM_CW, SSM_GROUP_CH, SSM_STATE, SSM_GROUP_CH)

    for pv in range(parts):
        @pl.when(part == pv)
        def _():
            x = jnp.concatenate(
                [jnp.concatenate([x_ref[_seg_rows(q, pv * segs + kk, seglen), :] for q in range(SSM_QPER)], axis=1)
                 for kk in range(segs)], axis=0).astype(BF16)
            y = yi_ref[...] + jnp.dot(x, sout_ref[...], preferred_element_type=F32)
            for t in range(SSM_CHUNK):
                y_ref[pl.ds(t, ncb, stride=SSM_CHUNK), :] = y[:, t * SSM_TILE:(t + 1) * SSM_TILE]


def _s5(u, a2, b2, c2, apow, bsz, ncb):
    t, width = u.shape
    no = width // SSM_TILE
    nc = t // SSM_CHUNK
    ncseq = nc // bsz
    parts = ncseq // ncb
    assert SCAN_SEGS % parts == 0
    rows = ncb * SSM_CHUNK
    slab = ncseq * SSM_QPER
    yi, st = pl.pallas_call(
        functools.partial(_s5_local_kernel, parts=parts),
        grid=(no, bsz, parts),
        in_specs=[
            pl.BlockSpec((rows, SSM_TILE), lambda o, b, p: (b * parts + p, o)),
            pl.BlockSpec((None,) + a2.shape[1:], lambda o, b, p: (o, 0, 0)),
            pl.BlockSpec((None,) + b2.shape[1:], lambda o, b, p: (o, 0, 0)),
        ],
        out_specs=[
            pl.BlockSpec((None, ncb, SSM_CW), lambda o, b, p: (o, b * parts + p, 0)),
            pl.BlockSpec((None, None, slab, LANES), lambda o, b, p: (o, b, 0, 0)),
        ],
        out_shape=[jax.ShapeDtypeStruct((no, nc, SSM_CW), F32),
                   jax.ShapeDtypeStruct((no, bsz, slab, LANES), F32)],
        scratch_shapes=[pltpu.VMEM((SSM_CW, SSM_CW), BF16), pltpu.VMEM((SSM_CW, SSM_SW), BF16)],
        compiler_params=_cparams(("arbitrary", "arbitrary", "arbitrary"), 52),
        name="s5_local",
    )(u, a2, b2)
    tiles = slab // SCAN_SEGS
    xin = pl.pallas_call(
        _s5_scan_kernel,
        grid=(no, bsz),
        in_specs=[
            pl.BlockSpec((None, None, tiles, SCAN_SEGS, LANES), lambda o, b: (o, b, 0, 0, 0)),
            pl.BlockSpec((None, 2, 2, SSM_SW // 4), lambda o, b: (o, 0, 0, 0)),
        ],
        out_specs=pl.BlockSpec((None, None, tiles, SCAN_SEGS, LANES), lambda o, b: (o, b, 0, 0, 0)),
        out_shape=jax.ShapeDtypeStruct((no, bsz, tiles, SCAN_SEGS, LANES), F32),
        compiler_params=_cparams(("arbitrary", "arbitrary"), 48),
        name="s5_scan",
    )(st.reshape(no, bsz, tiles, SCAN_SEGS, LANES), apow)
    return pl.pallas_call(
        functools.partial(_s5_out_kernel, parts=parts),
        grid=(no, bsz, parts),
        in_specs=[
            pl.BlockSpec((None, ncb, SSM_CW), lambda o, b, p: (o, b * parts + p, 0)),
            pl.BlockSpec((None, None, slab, LANES), lambda o, b, p: (o, b, 0, 0)),
            pl.BlockSpec((None,) + c2.shape[1:], lambda o, b, p: (o, 0, 0)),
        ],
        out_specs=pl.BlockSpec((rows, SSM_TILE), lambda o, b, p: (b * parts + p, o)),
        out_shape=jax.ShapeDtypeStruct((t, width), F32),
        scratch_shapes=[pltpu.VMEM((SSM_SW, SSM_CW), BF16)],
        compiler_params=_cparams(("arbitrary", "arbitrary", "arbitrary"), 52),
        name="s5_out",
    )(yi, xin.reshape(no, bsz, slab, LANES), c2)


def _merge_kernel(a_ref, y_ref, g0_ref, g1_ref, x_ref, wglu_ref, wb0_ref, wb1_ref, wout_ref, o_ref):
    y = y_ref[...]
    s = 0.5 * y * (1.0 + lax.erf(y * (2.0 ** -0.5)))
    glu = jnp.dot(s.astype(BF16), wglu_ref[...], preferred_element_type=F32)
    s = s * jax.nn.sigmoid(glu)
    pa = jnp.dot(a_ref[...], wb0_ref[...], preferred_element_type=F32)
    ps = jnp.dot(s.astype(BF16), wb1_ref[...], preferred_element_type=F32)
    merged = g0_ref[...].astype(F32) * pa + g1_ref[...].astype(F32) * ps
    o_ref[...] = x_ref[...] + jnp.dot(merged.astype(BF16), wout_ref[...], preferred_element_type=F32)


def _merge(a, y, proj, x, wglu, wbr, wout, layer, bm):
    t, d = x.shape
    aw = a.shape[1]
    g0 = 4 * SEG // d
    return pl.pallas_call(
        _merge_kernel,
        grid=(t // bm,),
        in_specs=[
            pl.BlockSpec((bm, aw), lambda i: (i, 0)),
            pl.BlockSpec((bm, aw), lambda i: (i, 0)),
            pl.BlockSpec((bm, d), lambda i: (i, g0)),
            pl.BlockSpec((bm, d), lambda i: (i, g0 + 1)),
            pl.BlockSpec((bm, d), lambda i: (i, 0)),
            pl.BlockSpec((None,) + wglu.shape[1:], lambda i: (layer, 0, 0)),
            pl.BlockSpec((None, None) + wbr.shape[2:], lambda i: (layer, 0, 0, 0)),
            pl.BlockSpec((None, None) + wbr.shape[2:], lambda i: (layer, 1, 0, 0)),
            pl.BlockSpec((None,) + wout.shape[1:], lambda i: (layer, 0, 0)),
        ],
        out_specs=pl.BlockSpec((bm, d), lambda i: (i, 0)),
        out_shape=jax.ShapeDtypeStruct((t, d), F32),
        compiler_params=_cparams(("arbitrary",), 48),
        name="merge",
    )(a, y, proj, proj, x, wglu, wbr, wbr, wout)


def _swiglu_partial(h, w1_ref, w3_ref, w2_ref, sub):
    ffc = w1_ref.shape[-1]
    acc = None
    for s in range(0, ffc, sub):
        e = min(s + sub, ffc)
        a = jnp.dot(h, w1_ref[:, s:e], preferred_element_type=F32)
        b = jnp.dot(h, w3_ref[:, s:e], preferred_element_type=F32)
        tt = (a * jax.nn.sigmoid(a) * b).astype(BF16)
        c = jnp.dot(tt, w2_ref[s:e, :], preferred_element_type=F32)
        acc = c if acc is None else acc + c
    return acc


def _ffn_kernel(x_ref, g_ref, w1_ref, w3_ref, w2_ref, o_ref, hn_ref, *, sub):
    f = pl.program_id(1)

    @pl.when(f == 0)
    def _():
        x = x_ref[...]
        hn_ref[...] = _rms(x, g_ref[...]).astype(BF16)
        o_ref[...] = x

    o_ref[...] += _swiglu_partial(hn_ref[...], w1_ref, w3_ref, w2_ref, sub)


def _ffn_tile(dff):
    return dff // 2 if (dff // 2) % LANES == 0 else dff


def _ffn(x, g, w1, w3, w2, li, bm):
    t, d = x.shape
    dff = w1.shape[2]
    ffc = _ffn_tile(dff)
    return pl.pallas_call(
        functools.partial(_ffn_kernel, sub=2 * LANES),
        grid=(t // bm, dff // ffc),
        in_specs=[
            pl.BlockSpec((bm, d), lambda i, f: (i, 0)),
            pl.BlockSpec((1, d), lambda i, f: (0, 0)),
            pl.BlockSpec((None, d, ffc), lambda i, f: (li, 0, f)),
            pl.BlockSpec((None, d, ffc), lambda i, f: (li, 0, f)),
            pl.BlockSpec((None, ffc, d), lambda i, f: (li, f, 0)),
        ],
        out_specs=pl.BlockSpec((bm, d), lambda i, f: (i, 0)),
        out_shape=jax.ShapeDtypeStruct((t, d), F32),
        scratch_shapes=[pltpu.VMEM((bm, d), BF16)],
        compiler_params=_cparams(("arbitrary", "arbitrary"), 56),
        name="ffn_dense",
    )(x, g, w1, w3, w2)


def _router_kernel(x_ref, g_ref, wr_ref, hn_ref, r_ref):
    hn = _rms(x_ref[...], g_ref[...])
    _to_row_tiles(hn_ref, hn)
    logits = jnp.dot(hn, wr_ref[...], preferred_element_type=F32, precision=lax.Precision.HIGHEST)
    lane = lax.broadcasted_iota(I32, logits.shape, 1)
    neg = jnp.float32(-1e30)
    logits = jnp.where(lane < N_EXPERTS, logits, neg)
    m1 = jnp.max(logits, axis=-1, keepdims=True)
    i1 = jnp.min(jnp.where(logits == m1, lane, LANES), axis=-1, keepdims=True)
    rest = jnp.where(lane == i1, neg, logits)
    m2 = jnp.max(rest, axis=-1, keepdims=True)
    i2 = jnp.min(jnp.where(rest == m2, lane, LANES), axis=-1, keepdims=True)
    e2 = jnp.exp(m2 - m1)
    g1 = 1.0 / (1.0 + e2)
    g2 = e2 / (1.0 + e2)
    r_ref[...] = jnp.where(lane == 0, i1.astype(F32),
                           jnp.where(lane == 1, i2.astype(F32),
                                     jnp.where(lane == 2, g1, jnp.where(lane == 3, g2, 0.0))))


def _router(x, g, wr_pad, bm):
    t, d = x.shape
    s_per = d // LANES
    return pl.pallas_call(
        _router_kernel,
        grid=(t // bm,),
        in_specs=[
            pl.BlockSpec((bm, d), lambda i: (i, 0)),
            pl.BlockSpec((1, d), lambda i: (0, 0)),
            pl.BlockSpec((d, LANES), lambda i: (0, 0)),
        ],
        out_specs=[pl.BlockSpec((bm * s_per, LANES), lambda i: (i, 0)), pl.BlockSpec((bm, LANES), lambda i: (i, 0))],
        out_shape=[jax.ShapeDtypeStruct((t * s_per, LANES), F32), jax.ShapeDtypeStruct((t, LANES), F32)],
        compiler_params=_cparams(("arbitrary",), 40),
        name="moe_router",
    )(x, g, wr_pad)


def _row_copy(src_hbm, dst_ref, sem, src_row, dst_row, s_per):
    src = pl.multiple_of(src_row * s_per, s_per)
    dst = pl.multiple_of(dst_row * s_per, s_per)
    return pltpu.make_async_copy(src_hbm.at[pl.ds(src, s_per)], dst_ref.at[pl.ds(dst, s_per)], sem)


FF_TILE = 2 * LANES


def _expert_kernel(be_ref, tok_ref, hn_hbm, w1_hbm, w3_hbm, w2_hbm, o_ref,
                   rows_ref, w1_ref, w3_ref, w2_ref, sta_ref, stb_ref, rsem, wsem, *, li, s_per):
    i = pl.program_id(0)
    bm = rows_ref.shape[1] // s_per
    dff = w1_ref.shape[1]
    nt = dff // FF_TILE

    def issue(step, slot):
        base = step * bm

        def body(r, c):
            _row_copy(hn_hbm, rows_ref.at[slot], rsem.at[slot], tok_ref[base + r], r, s_per).start()
            return c

        lax.fori_loop(0, bm, body, 0, unroll=8)

    def drain(slot):
        def body(r, c):
            _row_copy(hn_hbm, rows_ref.at[slot], rsem.at[slot], 0, r, s_per).wait()
            return c

        lax.fori_loop(0, bm, body, 0, unroll=8)

    @pl.when(i == 0)
    def _():
        issue(0, 0)

    e = be_ref[i]
    changed = (i == 0) | (e != be_ref[jnp.maximum(i - 1, 0)])

    @pl.when(changed)
    def _():
        def stream(tiles, stage_ref):
            def copy(k):
                src, _ = tiles[k]
                return pltpu.make_async_copy(src, stage_ref.at[k % 2], wsem.at[k % 2])

            copy(0).start()
            for k in range(len(tiles)):
                if k + 1 < len(tiles):
                    copy(k + 1).start()
                copy(k).wait()
                tiles[k][1](stage_ref[k % 2].astype(BF16))

        def col_sink(ref, c):
            def put(v):
                ref[:, c * FF_TILE:(c + 1) * FF_TILE] = v
            return put

        def row_sink(ref, c):
            def put(v):
                ref[c * FF_TILE:(c + 1) * FF_TILE, :] = v
            return put

        cols = []
        for c in range(nt):
            cols.append((w1_hbm.at[li, e, :, pl.ds(c * FF_TILE, FF_TILE)], col_sink(w1_ref, c)))
            cols.append((w3_hbm.at[li, e, :, pl.ds(c * FF_TILE, FF_TILE)], col_sink(w3_ref, c)))
        stream(cols, sta_ref)
        stream([(w2_hbm.at[li, e, pl.ds(c * FF_TILE, FF_TILE), :], row_sink(w2_ref, c)) for c in range(nt)],
               stb_ref)

    for slot in range(2):
        @pl.when(i % 2 == slot)
        def _():
            @pl.when(i + 1 < pl.num_programs(0))
            def _():
                issue(i + 1, 1 - slot)

            drain(slot)
            xb = _from_row_tiles(rows_ref.at[slot], s_per).astype(BF16)
            _to_row_tiles(o_ref, _swiglu_partial(xb, w1_ref, w3_ref, w2_ref, FF_TILE))


def _experts(block_exp, slot_tok, hn, w1, w3, w2, li, bm):
    d = w1.shape[2]
    dff = w1.shape[3]
    s_per = d // LANES
    n = slot_tok.shape[0]
    assert dff % FF_TILE == 0
    return pl.pallas_call(
        functools.partial(_expert_kernel, li=li, s_per=s_per),
        grid_spec=pltpu.PrefetchScalarGridSpec(
            num_scalar_prefetch=2,
            grid=(n // bm,),
            in_specs=[pl.BlockSpec(memory_space=pl.ANY)] * 4,
            out_specs=pl.BlockSpec((bm * s_per, LANES), lambda i, be, tok: (i, 0)),
            scratch_shapes=[
                pltpu.VMEM((2, bm * s_per, LANES), F32),
                pltpu.VMEM((d, dff), BF16), pltpu.VMEM((d, dff), BF16), pltpu.VMEM((dff, d), BF16),
                pltpu.VMEM((2, d, FF_TILE), w1.dtype), pltpu.VMEM((2, FF_TILE, d), w2.dtype),
                pltpu.SemaphoreType.DMA((2,)), pltpu.SemaphoreType.DMA((2,)),
            ],
        ),
        out_shape=jax.ShapeDtypeStruct((n * s_per, LANES), F32),
        compiler_params=_cparams(("arbitrary",), 56),
        name="moe_experts",
    )(block_exp, slot_tok, hn, w1, w3, w2)


def _combine_kernel(d0_ref, d1_ref, yb_hbm, x_ref, r_ref, o_ref, r0_ref, r1_ref, sem, *, s_per):
    bm = o_ref.shape[0]
    i = pl.program_id(0)

    def issue(step, slot):
        base = step * bm

        def body(r, c):
            _row_copy(yb_hbm, r0_ref.at[slot], sem.at[slot], d0_ref[base + r], r, s_per).start()
            _row_copy(yb_hbm, r1_ref.at[slot], sem.at[slot], d1_ref[base + r], r, s_per).start()
            return c

        lax.fori_loop(0, bm, body, 0, unroll=8)

    def drain(slot):
        def body(r, c):
            _row_copy(yb_hbm, r0_ref.at[slot], sem.at[slot], 0, r, s_per).wait()
            _row_copy(yb_hbm, r1_ref.at[slot], sem.at[slot], 0, r, s_per).wait()
            return c

        lax.fori_loop(0, bm, body, 0, unroll=8)

    @pl.when(i == 0)
    def _():
        issue(0, 0)

    for slot in range(2):
        @pl.when(i % 2 == slot)
        def _():
            @pl.when(i + 1 < pl.num_programs(0))
            def _():
                issue(i + 1, 1 - slot)

            drain(slot)
            route = r_ref[...]
            o_ref[...] = (x_ref[...] + route[:, 2:3] * _from_row_tiles(r0_ref.at[slot], s_per)
                          + route[:, 3:4] * _from_row_tiles(r1_ref.at[slot], s_per))


def _combine(d0, d1, yb, x, route, bm):
    t, d = x.shape
    s_per = d // LANES
    return pl.pallas_call(
        functools.partial(_combine_kernel, s_per=s_per),
        grid_spec=pltpu.PrefetchScalarGridSpec(
            num_scalar_prefetch=2,
            grid=(t // bm,),
            in_specs=[
                pl.BlockSpec(memory_space=pl.ANY),
                pl.BlockSpec((bm, d), lambda i, a, b: (i, 0)),
                pl.BlockSpec((bm, LANES), lambda i, a, b: (i, 0)),
            ],
            out_specs=pl.BlockSpec((bm, d), lambda i, a, b: (i, 0)),
            scratch_shapes=[pltpu.VMEM((2, bm * s_per, LANES), F32), pltpu.VMEM((2, bm * s_per, LANES), F32),
                            pltpu.SemaphoreType.DMA((2,))],
        ),
        out_shape=jax.ShapeDtypeStruct((t, d), F32),
        compiler_params=_cparams(("arbitrary",), 32),
        name="moe_combine",
    )(d0, d1, yb, x, route)


def _dispatch_tables(idx, bm):
    t = idx.shape[0]
    flat_e = idx.reshape(-1)
    onehot = (flat_e[:, None] == jnp.arange(N_EXPERTS, dtype=I32)[None, :]).astype(I32)
    csum = jnp.cumsum(onehot, axis=0)
    rank = jnp.take_along_axis(csum, flat_e[:, None], axis=1)[:, 0] - 1
    counts = csum[-1]
    padded = (counts + bm - 1) // bm * bm
    pad_end = jnp.cumsum(padded)
    pad_start = pad_end - padded
    dest = (pad_start[flat_e] + rank).astype(I32)
    n_slots = t * TOP_K + N_EXPERTS * bm
    slot_tok = jnp.zeros((n_slots,), I32).at[dest].set(jnp.arange(t * TOP_K, dtype=I32) // TOP_K)
    block_start = jnp.arange(n_slots // bm, dtype=I32) * bm
    block_exp = jnp.minimum(jnp.searchsorted(pad_end, block_start, side='right'), N_EXPERTS - 1).astype(I32)
    return dest.reshape(t, TOP_K), slot_tok, block_exp


def _moe(x, g, w_router, w1, w3, w2, li, bm_tok, bm_slot):
    d = x.shape[1]
    wr_pad = jnp.zeros((d, LANES), F32).at[:, :N_EXPERTS].set(w_router)
    hn, route = _router(x, g, wr_pad, bm_tok)
    idx = route[:, :TOP_K].astype(I32)
    dest, slot_tok, block_exp = _dispatch_tables(idx, bm_slot)
    yb = _experts(block_exp, slot_tok, hn, w1, w3, w2, li, bm_slot)
    return _combine(dest[:, 0], dest[:, 1], yb, x, route, bm_tok)


def _norm_kernel(x_ref, g_ref, o_ref):
    o_ref[...] = _rms(x_ref[...], g_ref[...])


def _final_norm(x, g, bm):
    t, d = x.shape
    return pl.pallas_call(
        _norm_kernel,
        grid=(t // bm,),
        in_specs=[pl.BlockSpec((bm, d), lambda i: (i, 0)), pl.BlockSpec((1, d), lambda i: (0, 0))],
        out_specs=pl.BlockSpec((bm, d), lambda i: (i, 0)),
        out_shape=jax.ShapeDtypeStruct((t, d), F32),
        compiler_params=_cparams(("arbitrary",), 32),
        name="final_norm",
    )(x, g)


def _rope_tables(seq):
    half = ATTN_HEAD_DIM // 2
    inv_freq = 1.0 / (ROPE_THETA ** (jnp.arange(half, dtype=F32) / half))
    ang = jnp.arange(seq, dtype=F32)[:, None] * inv_freq[None, :]
    cos = jnp.tile(jnp.cos(ang), (1, 4))
    sin = jnp.sin(ang)
    return cos, jnp.concatenate([-sin, -sin, sin, sin], axis=1)


def _permute_qk_columns(w):
    nl, d = w.shape[:2]
    half = ATTN_HEAD_DIM // 2
    qk = w[..., :2 * SEG].reshape(nl, d, 2, ATTN_HEADS, 2, 2, half).transpose(0, 1, 2, 3, 5, 4, 6)
    return jnp.concatenate([qk.reshape(nl, d, 2 * SEG), w[..., 2 * SEG:]], axis=2)


def _block(n, pref):
    return pref if n % pref == 0 else n


def kernel(x, norm_mix, w_in, attn_lambda, attn_subln, ssm_lam_re, ssm_lam_im, ssm_log_dt, ssm_b_re, ssm_b_im, ssm_c_re, ssm_c_im, ssm_d, w_glu, w_branch, w_out, norm_ffn, ffn_w1, ffn_w3, ffn_w2, moe_router, moe_w1, moe_w3, moe_w2, norm_final):
    bsz, seq, d = x.shape
    depth = w_in.shape[0]
    t = bsz * seq
    cos_t, sin_t = _rope_tables(seq)
    xf = x.reshape(t, d)
    bm = _block(seq, 1024)
    w_all = _permute_qk_columns(w_in).astype(BF16)
    a2, b2, c2, apow = jax.vmap(_s5_tables)(ssm_lam_re, ssm_lam_im, ssm_log_dt, ssm_b_re, ssm_b_im,
                                            ssm_c_re, ssm_c_im, ssm_d)
    wglu, wbr, wout = w_glu.astype(BF16), w_branch.astype(BF16), w_out.astype(BF16)
    fw1, fw3, fw2 = ffn_w1.astype(BF16), ffn_w3.astype(BF16), ffn_w2.astype(BF16)
    for layer in range(depth):
        proj, u = _inproj(xf, norm_mix[layer][None], w_all, layer, cos_t, sin_t, seq, _block(seq, 512))

        lam_init = jnp.full((1,), 0.8 - 0.6 * math.exp(-0.3 * layer), F32)
        a = _attention(proj, lam_init, attn_lambda[layer], attn_subln[layer][None], bsz, seq,
                       _block(seq, 1024), _block(seq // 2, 512))

        y = _s5(u, a2[layer], b2[layer], c2[layer], apow[layer], bsz, _block(seq // SSM_CHUNK, 512))

        xf = _merge(a, y, proj, xf, wglu, wbr, wout, layer, _block(seq, 512))

        i = layer // 2
        if layer % 2 == 0:
            xf = _ffn(xf, norm_ffn[layer][None], fw1, fw3, fw2, i, bm)
        else:
            xf = _moe(xf, norm_ffn[layer][None], moe_router[i], moe_w1, moe_w3, moe_w2, i, _block(seq, 256), 512)
    return _final_norm(xf, norm_final[None], bm).reshape(bsz, seq, d)
```

```python
import functools
import math

import jax
import jax.numpy as jnp
from jax import lax
from jax.experimental import pallas as pl
from jax.experimental.pallas import tpu as pltpu

F32 = jnp.float32
BF16 = jnp.bfloat16
I32 = jnp.int32

RMS_EPS = 1e-6
ROPE_THETA = 10000.0
LOG2E = 1.4426950408889634

LANES = 128
SUBLANES = 8
VMEM_BYTES_V7X = 64 * 1024 * 1024

ATTN_HEADS = 4
ATTN_HEAD_DIM = 64
HEAD_SLAB = 2 * ATTN_HEAD_DIM
SSM_GROUP_CH = 16
SSM_STATE = 64
SSM_OCTET = LANES // SSM_GROUP_CH
SSM_CHUNK = 8
N_EXPERTS = 8
TOP_K = 2
SEG = 512


def _cparams(sem, vmem_mb):
    return pltpu.CompilerParams(dimension_semantics=sem, vmem_limit_bytes=vmem_mb * 1024 * 1024)


def _rms(x, g):
    return x * lax.rsqrt(jnp.mean(x * x, axis=-1, keepdims=True) + RMS_EPS) * g


def _sigmoid(x):
    return 0.5 * jnp.tanh(0.5 * x) + 0.5


def _inproj_kernel(x_ref, g_ref, w_ref, cos_ref, sin_ref, o_ref, u_ref, *, qscale):
    xn = _rms(x_ref[...], g_ref[...]).astype(BF16)
    cos = cos_ref[...]
    sin = sin_ref[...]

    half = ATTN_HEAD_DIM // 2
    first = (lax.broadcasted_iota(I32, cos.shape, 1) % ATTN_HEAD_DIM) < half

    def rot(a):
        parts = []
        for s in range(0, a.shape[1], HEAD_SLAB):
            blk = a[:, s:s + HEAD_SLAB]
            partner = jnp.where(first, pltpu.roll(blk, HEAD_SLAB - half, 1), pltpu.roll(blk, half, 1))
            parts.append(blk * cos + partner * sin)
        return jnp.concatenate(parts, axis=1)

    for j in range(w_ref.shape[1] // SEG):
        acc = jnp.dot(xn, w_ref[:, j * SEG:(j + 1) * SEG], preferred_element_type=F32)
        if j == 0:
            acc = rot(acc) * qscale
        elif j == 1:
            acc = rot(acc)
        elif j == 3:
            u_ref[...] = acc
        elif j >= 4:
            acc = _sigmoid(acc)
        o_ref[:, j * SEG:(j + 1) * SEG] = acc.astype(BF16)


def _inproj(x, g, w, layer, cos_t, sin_t, seq, bm):
    t, d = x.shape
    n = w.shape[2]
    qscale = ATTN_HEAD_DIM ** -0.5 * LOG2E
    nseq = seq // bm
    return pl.pallas_call(
        functools.partial(_inproj_kernel, qscale=qscale),
        grid=(t // bm,),
        in_specs=[
            pl.BlockSpec((bm, d), lambda i: (i, 0)),
            pl.BlockSpec((1, d), lambda i: (0, 0)),
            pl.BlockSpec((None, d, n), lambda i: (layer, 0, 0)),
            pl.BlockSpec((bm, HEAD_SLAB), lambda i: (i % nseq, 0)),
            pl.BlockSpec((bm, HEAD_SLAB), lambda i: (i % nseq, 0)),
        ],
        out_specs=[pl.BlockSpec((bm, n), lambda i: (i, 0)), pl.BlockSpec((bm, SEG), lambda i: (i, 0))],
        out_shape=[jax.ShapeDtypeStruct((t, n), BF16), jax.ShapeDtypeStruct((t, SEG), F32)],
        compiler_params=_cparams(("arbitrary",), 52),
        name="inproj",
    )(x, g, w, cos_t, sin_t)


ONES_ROWS = 16


def _attn_kernel(sc_ref, lp_ref, q_ref, k_ref, v_ref, g_ref, o_ref, vt_ref, acc_ref, m_ref,
                 s0_ref, s1_ref, p0_ref, p1_ref, x0_ref, x1_ref, *, bk):
    qi = pl.program_id(2)
    seq = k_ref.shape[0]
    bq = q_ref.shape[0]
    nkv = seq // bk
    assert nkv >= 2 and nkv % 2 == 0

    @pl.when(qi == 0)
    def _():
        for c in range(nkv):
            vt_ref[c, :HEAD_SLAB, :] = v_ref[c * bk:(c + 1) * bk, :].astype(F32).T.astype(BF16)
            vt_ref[c, HEAD_SLAB:, :] = jnp.ones((ONES_ROWS, bk), BF16)

    qt = q_ref[...].astype(F32).T
    row = lax.broadcasted_iota(I32, qt.shape, 0)
    half0 = row < ATTN_HEAD_DIM
    qst = jnp.concatenate([jnp.where(half0, qt, 0.0), jnp.where(half0, 0.0, qt)], axis=1).astype(BF16)

    acc_ref[...] = jnp.zeros_like(acc_ref)
    m_ref[...] = jnp.full_like(m_ref, -1e30)
    s_refs = (s0_ref, s1_ref)
    p_refs = (p0_ref, p1_ref)
    x_refs = (x0_ref, x1_ref)

    def qk(j, cur):
        off = pl.multiple_of(j * bk, bk)
        st = jnp.dot(k_ref[pl.ds(off, bk), :], qst, preferred_element_type=F32)
        s_refs[cur][...] = st
        x_refs[cur][...] = jnp.max(st, axis=0, keepdims=True)

    def pv(j, cur, alpha):
        acc_ref[...] = acc_ref[...] * alpha + jnp.dot(vt_ref[j], p_refs[cur][...],
                                                      preferred_element_type=F32)

    def softmax(cur):
        m_prev = m_ref[...]
        m_new = jnp.maximum(m_prev, x_refs[cur][...])
        p_refs[cur][...] = jnp.exp2((s_refs[cur][...] - m_new).astype(BF16))
        m_ref[...] = m_new
        return jnp.exp2(m_prev - m_new)

    def stage(j, cur, alpha):
        qk(j + 1, 1 - cur)
        pv(j - 1, 1 - cur, alpha)
        return softmax(cur)

    qk(0, 0)
    qk(1, 1)
    alpha = softmax(0)

    def pair(jj, alpha):
        j = 1 + 2 * jj
        return stage(j + 1, 0, stage(j, 1, alpha))

    alpha = lax.fori_loop(0, (nkv - 2) // 2, pair, alpha)
    pv(nkv - 2, 0, alpha)
    alpha = softmax(1)
    pv(nkv - 1, 1, alpha)

    lam_init = sc_ref[0]
    lp = lp_ref[...]
    lam = (jnp.exp(jnp.sum(lp[0:1] * lp[1:2], axis=-1, keepdims=True))
           - jnp.exp(jnp.sum(lp[2:3] * lp[3:4], axis=-1, keepdims=True)) + lam_init)
    acc = acc_ref[...]
    o = acc[:HEAD_SLAB] / acc[HEAD_SLAB:HEAD_SLAB + 1]
    a = o[:, :bq] - lam * o[:, bq:]
    a = a * lax.rsqrt(jnp.mean(a * a, axis=0, keepdims=True) + RMS_EPS) * (1.0 - lam_init)
    o_ref[...] = (a.T * g_ref[...]).astype(BF16)


def _attention(proj, lam_init, lam_p, subln, bsz, seq, bq, bk):
    t = proj.shape[0]
    nq = seq // bq
    koff = SEG // HEAD_SLAB
    return pl.pallas_call(
        functools.partial(_attn_kernel, bk=bk),
        grid=(bsz, ATTN_HEADS, nq),
        in_specs=[
            pl.BlockSpec(memory_space=pltpu.SMEM),
            pl.BlockSpec((4, ATTN_HEAD_DIM), lambda b, h, i: (0, 0)),
            pl.BlockSpec((bq, HEAD_SLAB), lambda b, h, i: (b * nq + i, h)),
            pl.BlockSpec((seq, HEAD_SLAB), lambda b, h, i: (b, koff + h)),
            pl.BlockSpec((seq, HEAD_SLAB), lambda b, h, i: (b, 2 * koff + h)),
            pl.BlockSpec((1, HEAD_SLAB), lambda b, h, i: (0, 0)),
        ],
        out_specs=pl.BlockSpec((bq, HEAD_SLAB), lambda b, h, i: (b * nq + i, h)),
        out_shape=jax.ShapeDtypeStruct((t, ATTN_HEADS * HEAD_SLAB), BF16),
        scratch_shapes=[
            pltpu.VMEM((seq // bk, HEAD_SLAB + ONES_ROWS, bk), BF16),
            pltpu.VMEM((HEAD_SLAB + ONES_ROWS, 2 * bq), F32),
            pltpu.VMEM((1, 2 * bq), F32),
            pltpu.VMEM((bk, 2 * bq), F32),
            pltpu.VMEM((bk, 2 * bq), F32),
            pltpu.VMEM((bk, 2 * bq), BF16),
            pltpu.VMEM((bk, 2 * bq), BF16),
            pltpu.VMEM((1, 2 * bq), F32),
            pltpu.VMEM((1, 2 * bq), F32),
        ],
        compiler_params=_cparams(("arbitrary", "arbitrary", "arbitrary"), 56),
        name="diff_attention",
    )(lam_init, lam_p, proj, proj, proj, subln)


SSM_TILE = SSM_OCTET * SSM_GROUP_CH
SSM_CW = SSM_CHUNK * SSM_TILE
SSM_SW = 4 * SSM_OCTET * SSM_STATE
SSM_QPER = SSM_SW // LANES
SCAN_SEGS = SUBLANES


def _s5_tables(lam_re, lam_im, log_dt, b_re, b_im, c_re, c_im, d_skip):
    hp = lax.Precision.HIGHEST
    ll = SSM_CHUNK
    g = lam_re.shape[1]
    no = g // SSM_OCTET
    lam = lax.complex(lam_re, lam_im)
    ldt = lam * jnp.exp(log_dt)[..., None]
    a = jnp.exp(ldt)
    bbar = ((a - 1.0) / lam)[..., None] * lax.complex(b_re, b_im)
    cmat = lax.complex(c_re, c_im)
    n = jnp.arange(ll + 1, dtype=F32)
    pw = jnp.exp(ldt[:, None] * n[None, :, None, None])

    def kern(d):
        return jnp.real(jnp.einsum('ghp,dgp,gpk->dghk', cmat[d], pw[d, :ll], bbar[d], precision=hp))

    kf, kb = kern(0), kern(1)
    tt = jnp.arange(ll)
    lag = tt[None, :] - tt[:, None]
    mf = jnp.where((lag >= 0)[:, :, None, None, None], kf[jnp.clip(lag, 0, ll - 1)], 0.0)
    mb = jnp.where((lag <= 0)[:, :, None, None, None], kb[jnp.clip(-lag, 0, ll - 1)], 0.0)
    skip = (jnp.eye(ll, dtype=F32)[:, :, None, None, None]
            * (d_skip.reshape(g, SSM_GROUP_CH)[None, None, :, :, None] * jnp.eye(SSM_GROUP_CH, dtype=F32)[None, None, None]))
    m = (mf + mb + skip).reshape(ll, ll, no, SSM_OCTET, SSM_GROUP_CH, SSM_GROUP_CH)
    a2 = m.transpose(2, 0, 3, 5, 1, 4).reshape(no, SSM_CW, ll * SSM_GROUP_CH)

    ein = jnp.stack([pw[0, ll - 1 - tt][..., None] * bbar[0][None],
                     pw[1, tt][..., None] * bbar[1][None]])
    ein = jnp.stack([jnp.real(ein), jnp.imag(ein)], axis=1)
    ein = ein.reshape(2, 2, ll, no, SSM_OCTET, SSM_STATE, SSM_GROUP_CH)
    b2 = ein.transpose(3, 2, 4, 6, 0, 1, 5).reshape(no, SSM_CW, 4 * SSM_STATE)

    eout = jnp.stack([cmat[0][None] * pw[0, tt + 1][:, :, None, :],
                      cmat[1][None] * pw[1, ll - tt][:, :, None, :]])
    eout = jnp.stack([jnp.real(eout), -jnp.imag(eout)], axis=1)
    eout = eout.reshape(2, 2, ll, no, SSM_OCTET, SSM_GROUP_CH, SSM_STATE)
    c2 = eout.transpose(3, 0, 1, 4, 6, 2, 5).reshape(no, SSM_SW, ll * SSM_GROUP_CH)

    al = pw[:, ll]
    apow = jnp.stack([jnp.real(al), jnp.imag(al)], axis=1)
    apow = apow.reshape(2, 2, no, SSM_OCTET * SSM_STATE).transpose(2, 0, 1, 3)
    return a2.astype(BF16), b2.astype(BF16), c2.astype(BF16), apow.astype(F32)


def _to_row_tiles(ref, x):
    n = x.shape[0]
    s_per = x.shape[1] // LANES
    for s in range(s_per):
        ref[pl.ds(s, n, stride=s_per), :] = x[:, s * LANES:(s + 1) * LANES]


def _from_row_tiles(ref, s_per):
    n = ref.shape[0] // s_per
    return jnp.concatenate([ref[pl.ds(s, n, stride=s_per), :] for s in range(s_per)], axis=1)


def _shr(idx, n):
    assert n & (n - 1) == 0
    return idx >> (n.bit_length() - 1)


def _group_of(idx, width):
    return _shr(idx, width) & (SSM_OCTET - 1)


def _spread(compact, rep_rows, rep_cols, inner, row_w, col_w):
    q = lax.broadcasted_iota(I32, (rep_rows, rep_cols), 0)
    c = lax.broadcasted_iota(I32, (rep_rows, rep_cols), 1)
    rep = ((_shr(q, inner) == _shr(c, inner * SSM_OCTET)) & ((q & (inner - 1)) == (c & (inner - 1)))).astype(BF16)
    full = jnp.dot(compact, rep, preferred_element_type=F32)
    r = lax.broadcasted_iota(I32, full.shape, 0)
    cc = lax.broadcasted_iota(I32, full.shape, 1)
    return jnp.where(_group_of(r, row_w) == _group_of(cc, col_w), full, 0.0).astype(BF16)


def _seg_rows(q, k, seglen):
    return pl.ds(q * SCAN_SEGS + k, seglen, stride=SSM_QPER * SCAN_SEGS)


def _s5_local_kernel(u_ref, a2_ref, b2_ref, yi_ref, st_ref, toe_ref, sin_ref, *, parts):
    ncb = yi_ref.shape[0]
    part = pl.program_id(2)
    segs = SCAN_SEGS // parts
    seglen = ncb // segs

    @pl.when((pl.program_id(1) == 0) & (part == 0))
    def _():
        toe_ref[...] = _spread(a2_ref[...], a2_ref.shape[1], SSM_CW, SSM_GROUP_CH, SSM_GROUP_CH, SSM_GROUP_CH)
        sin_ref[...] = _spread(b2_ref[...], b2_ref.shape[1], SSM_SW, SSM_STATE, SSM_GROUP_CH, SSM_STATE)

    u = jnp.concatenate([u_ref[pl.ds(j, ncb, stride=SSM_CHUNK), :] for j in range(SSM_CHUNK)],
                        axis=1).astype(BF16)
    yi_ref[...] = jnp.dot(u, toe_ref[...], preferred_element_type=F32)
    st = jnp.dot(u, sin_ref[...], preferred_element_type=F32)
    for pv in range(parts):
        @pl.when(part == pv)
        def _():
            for kk in range(segs):
                for q in range(SSM_QPER):
                    st_ref[_seg_rows(q, pv * segs + kk, seglen), :] = (
                        st[kk * seglen:(kk + 1) * seglen, q * LANES:(q + 1) * LANES])


def _cmul(ar, ai, xr, xi):
    return ar * xr - ai * xi, ar * xi + ai * xr


def _s5_scan_kernel(st_ref, ap_ref, x_ref):
    wt = SSM_QPER // 4
    seglen = st_ref.shape[0] // SSM_QPER
    assert seglen & (seglen - 1) == 0
    ap = ap_ref[...]

    def coef(d, r):
        return [jnp.broadcast_to(ap[d, r:r + 1, t * LANES:(t + 1) * LANES], (SCAN_SEGS, LANES)) for t in range(wt)]

    far, fai, bar, bai = coef(0, 0), coef(0, 1), coef(1, 0), coef(1, 1)

    def scan(init, store):
        def step(i, carry):
            fr, fi, br, bi = [list(c) for c in carry]
            rf = i * SSM_QPER
            rb = (seglen - 1 - i) * SSM_QPER
            for t in range(wt):
                if store:
                    x_ref[rf + t] = fr[t]
                    x_ref[rf + wt + t] = fi[t]
                    x_ref[rb + 2 * wt + t] = br[t]
                    x_ref[rb + 3 * wt + t] = bi[t]
                pr, pi = _cmul(far[t], fai[t], fr[t], fi[t])
                fr[t] = pr + st_ref[rf + t]
                fi[t] = pi + st_ref[rf + wt + t]
                pr, pi = _cmul(bar[t], bai[t], br[t], bi[t])
                br[t] = pr + st_ref[rb + 2 * wt + t]
                bi[t] = pi + st_ref[rb + 3 * wt + t]
            return tuple(fr), tuple(fi), tuple(br), tuple(bi)

        return lax.fori_loop(0, seglen, step, init)

    z = tuple(jnp.zeros((SCAN_SEGS, LANES), F32) for _ in range(wt))
    fr, fi, br, bi = scan((z, z, z, z), store=False)

    z1 = jnp.zeros((1, LANES), F32)
    fcr, fci, bcr, bci = [], [], [], []
    for t in range(wt):
        fsr, fsi, bsr, bsi = far[t][0:1], fai[t][0:1], bar[t][0:1], bai[t][0:1]
        for _ in range(seglen.bit_length() - 1):
            fsr, fsi = _cmul(fsr, fsi, fsr, fsi)
            bsr, bsi = _cmul(bsr, bsi, bsr, bsi)
        cr, ci = z1, z1
        rs, is_ = [], []
        for k in range(SCAN_SEGS):
            rs.append(cr)
            is_.append(ci)
            cr, ci = _cmul(fsr, fsi, cr, ci)
            cr, ci = cr + fr[t][k:k + 1], ci + fi[t][k:k + 1]
        fcr.append(jnp.concatenate(rs, axis=0))
        fci.append(jnp.concatenate(is_, axis=0))
        cr, ci = z1, z1
        rs, is_ = [None] * SCAN_SEGS, [None] * SCAN_SEGS
        for k in reversed(range(SCAN_SEGS)):
            rs[k] = cr
            is_[k] = ci
            cr, ci = _cmul(bsr, bsi, cr, ci)
            cr, ci = cr + br[t][k:k + 1], ci + bi[t][k:k + 1]
        bcr.append(jnp.concatenate(rs, axis=0))
        bci.append(jnp.concatenate(is_, axis=0))

    scan((tuple(fcr), tuple(fci), tuple(bcr), tuple(bci)), store=True)


def _s5_out_kernel(yi_ref, x_ref, c2_ref, y_ref, sout_ref, *, parts):
    ncb = yi_ref.shape[0]
    part = pl.program_id(2)
    segs = SCAN_SEGS // parts
    seglen = ncb // segs

    @pl.when((pl.program_id(1) == 0) & (part == 0))
    def _():
        sout_ref[...] = _spread(c2_ref[...], c2_ref.shape[1], SSM_CW, SSM_GROUP_CH, SSM_STATE, SSM_GROUP_CH)

    for pv in range(parts):
        @pl.when(part == pv)
        def _():
            x = jnp.concatenate(
                [jnp.concatenate([x_ref[_seg_rows(q, pv * segs + kk, seglen), :] for q in range(SSM_QPER)], axis=1)
                 for kk in range(segs)], axis=0).astype(BF16)
            y = yi_ref[...] + jnp.dot(x, sout_ref[...], preferred_element_type=F32)
            for t in range(SSM_CHUNK):
                y_ref[pl.ds(t, ncb, stride=SSM_CHUNK), :] = y[:, t * SSM_TILE:(t + 1) * SSM_TILE]


def _s5(u, a2, b2, c2, apow, bsz, ncb):
    t, width = u.shape
    no = width // SSM_TILE
    nc = t // SSM_CHUNK
    ncseq = nc // bsz
    parts = ncseq // ncb
    assert SCAN_SEGS % parts == 0
    rows = ncb * SSM_CHUNK
    slab = ncseq * SSM_QPER
    yi, st = pl.pallas_call(
        functools.partial(_s5_local_kernel, parts=parts),
        grid=(no, bsz, parts),
        in_specs=[
            pl.BlockSpec((rows, SSM_TILE), lambda o, b, p: (b * parts + p, o)),
            pl.BlockSpec((None,) + a2.shape[1:], lambda o, b, p: (o, 0, 0)),
            pl.BlockSpec((None,) + b2.shape[1:], lambda o, b, p: (o, 0, 0)),
        ],
        out_specs=[
            pl.BlockSpec((None, ncb, SSM_CW), lambda o, b, p: (o, b * parts + p, 0)),
            pl.BlockSpec((None, None, slab, LANES), lambda o, b, p: (o, b, 0, 0)),
        ],
        out_shape=[jax.ShapeDtypeStruct((no, nc, SSM_CW), F32),
                   jax.ShapeDtypeStruct((no, bsz, slab, LANES), F32)],
        scratch_shapes=[pltpu.VMEM((SSM_CW, SSM_CW), BF16), pltpu.VMEM((SSM_CW, SSM_SW), BF16)],
        compiler_params=_cparams(("arbitrary", "arbitrary", "arbitrary"), 52),
        name="s5_local",
    )(u, a2, b2)
    tiles = slab // SCAN_SEGS
    xin = pl.pallas_call(
        _s5_scan_kernel,
        grid=(no, bsz),
        in_specs=[
            pl.BlockSpec((None, None, tiles, SCAN_SEGS, LANES), lambda o, b: (o, b, 0, 0, 0)),
            pl.BlockSpec((None, 2, 2, SSM_SW // 4), lambda o, b: (o, 0, 0, 0)),
        ],
        out_specs=pl.BlockSpec((None, None, tiles, SCAN_SEGS, LANES), lambda o, b: (o, b, 0, 0, 0)),
        out_shape=jax.ShapeDtypeStruct((no, bsz, tiles, SCAN_SEGS, LANES), F32),
        compiler_params=_cparams(("arbitrary", "arbitrary"), 48),
        name="s5_scan",
    )(st.reshape(no, bsz, tiles, SCAN_SEGS, LANES), apow)
    return pl.pallas_call(
        functools.partial(_s5_out_kernel, parts=parts),
        grid=(no, bsz, parts),
        in_specs=[
            pl.BlockSpec((None, ncb, SSM_CW), lambda o, b, p: (o, b * parts + p, 0)),
            pl.BlockSpec((None, None, slab, LANES), lambda o, b, p: (o, b, 0, 0)),
            pl.BlockSpec((None,) + c2.shape[1:], lambda o, b, p: (o, 0, 0)),
        ],
        out_specs=pl.BlockSpec((rows, SSM_TILE), lambda o, b, p: (b * parts + p, o)),
        out_shape=jax.ShapeDtypeStruct((t, width), F32),
        scratch_shapes=[pltpu.VMEM((SSM_SW, SSM_CW), BF16)],
        compiler_params=_cparams(("arbitrary", "arbitrary", "arbitrary"), 52),
        name="s5_out",
    )(yi, xin.reshape(no, bsz, slab, LANES), c2)


def _merge_kernel(a_ref, y_ref, g0_ref, g1_ref, x_ref, wglu_ref, wb0_ref, wb1_ref, wout_ref, o_ref):
    y = y_ref[...]
    s = 0.5 * y * (1.0 + lax.erf(y * (2.0 ** -0.5)))
    glu = jnp.dot(s.astype(BF16), wglu_ref[...], preferred_element_type=F32)
    s = s * jax.nn.sigmoid(glu)
    pa = jnp.dot(a_ref[...], wb0_ref[...], preferred_element_type=F32)
    ps = jnp.dot(s.astype(BF16), wb1_ref[...], preferred_element_type=F32)
    merged = g0_ref[...].astype(F32) * pa + g1_ref[...].astype(F32) * ps
    o_ref[...] = x_ref[...] + jnp.dot(merged.astype(BF16), wout_ref[...], preferred_element_type=F32)


def _merge(a, y, proj, x, wglu, wbr, wout, layer, bm):
    t, d = x.shape
    aw = a.shape[1]
    g0 = 4 * SEG // d
    return pl.pallas_call(
        _merge_kernel,
        grid=(t // bm,),
        in_specs=[
            pl.BlockSpec((bm, aw), lambda i: (i, 0)),
            pl.BlockSpec((bm, aw), lambda i: (i, 0)),
            pl.BlockSpec((bm, d), lambda i: (i, g0)),
            pl.BlockSpec((bm, d), lambda i: (i, g0 + 1)),
            pl.BlockSpec((bm, d), lambda i: (i, 0)),
            pl.BlockSpec((None,) + wglu.shape[1:], lambda i: (layer, 0, 0)),
            pl.BlockSpec((None, None) + wbr.shape[2:], lambda i: (layer, 0, 0, 0)),
            pl.BlockSpec((None, None) + wbr.shape[2:], lambda i: (layer, 1, 0, 0)),
            pl.BlockSpec((None,) + wout.shape[1:], lambda i: (layer, 0, 0)),
        ],
        out_specs=pl.BlockSpec((bm, d), lambda i: (i, 0)),
        out_shape=jax.ShapeDtypeStruct((t, d), F32),
        compiler_params=_cparams(("arbitrary",), 48),
        name="merge",
    )(a, y, proj, proj, x, wglu, wbr, wbr, wout)


def _swiglu_partial(h, w1_ref, w3_ref, w2_ref, sub, side_work=None):
    ffc = w1_ref.shape[-1]
    acc = None
    for k, s in enumerate(range(0, ffc, sub)):
        e = min(s + sub, ffc)
        if side_work is not None:
            side_work(k, -(-ffc // sub))
        a = jnp.dot(h, w1_ref[:, s:e], preferred_element_type=F32)
        b = jnp.dot(h, w3_ref[:, s:e], preferred_element_type=F32)
        tt = (a * jax.nn.sigmoid(a) * b).astype(BF16)
        c = jnp.dot(tt, w2_ref[s:e, :], preferred_element_type=F32)
        acc = c if acc is None else acc + c
    return acc


def _ffn_kernel(x_ref, g_ref, w1_ref, w3_ref, w2_ref, o_ref, hn_ref, *, sub):
    f = pl.program_id(1)

    @pl.when(f == 0)
    def _():
        x = x_ref[...]
        hn_ref[...] = _rms(x, g_ref[...]).astype(BF16)
        o_ref[...] = x

    o_ref[...] += _swiglu_partial(hn_ref[...], w1_ref, w3_ref, w2_ref, sub)


def _ffn_tile(dff):
    return dff // 2 if (dff // 2) % LANES == 0 else dff


def _ffn(x, g, w1, w3, w2, li, bm):
    t, d = x.shape
    dff = w1.shape[2]
    ffc = _ffn_tile(dff)
    return pl.pallas_call(
        functools.partial(_ffn_kernel, sub=2 * LANES),
        grid=(t // bm, dff // ffc),
        in_specs=[
            pl.BlockSpec((bm, d), lambda i, f: (i, 0)),
            pl.BlockSpec((1, d), lambda i, f: (0, 0)),
            pl.BlockSpec((None, d, ffc), lambda i, f: (li, 0, f)),
            pl.BlockSpec((None, d, ffc), lambda i, f: (li, 0, f)),
            pl.BlockSpec((None, ffc, d), lambda i, f: (li, f, 0)),
        ],
        out_specs=pl.BlockSpec((bm, d), lambda i, f: (i, 0)),
        out_shape=jax.ShapeDtypeStruct((t, d), F32),
        scratch_shapes=[pltpu.VMEM((bm, d), BF16)],
        compiler_params=_cparams(("arbitrary", "arbitrary"), 56),
        name="ffn_dense",
    )(x, g, w1, w3, w2)


def _router_kernel(x_ref, g_ref, wr_ref, hn_ref, r_ref):
    hn = _rms(x_ref[...], g_ref[...])
    _to_row_tiles(hn_ref, hn)
    logits = jnp.dot(hn, wr_ref[...], preferred_element_type=F32, precision=lax.Precision.HIGHEST)
    lane = lax.broadcasted_iota(I32, logits.shape, 1)
    neg = jnp.float32(-1e30)
    logits = jnp.where(lane < N_EXPERTS, logits, neg)
    m1 = jnp.max(logits, axis=-1, keepdims=True)
    i1 = jnp.min(jnp.where(logits == m1, lane, LANES), axis=-1, keepdims=True)
    rest = jnp.where(lane == i1, neg, logits)
    m2 = jnp.max(rest, axis=-1, keepdims=True)
    i2 = jnp.min(jnp.where(rest == m2, lane, LANES), axis=-1, keepdims=True)
    e2 = jnp.exp(m2 - m1)
    g1 = 1.0 / (1.0 + e2)
    g2 = e2 / (1.0 + e2)
    r_ref[...] = jnp.where(lane == 0, i1.astype(F32),
                           jnp.where(lane == 1, i2.astype(F32),
                                     jnp.where(lane == 2, g1, jnp.where(lane == 3, g2, 0.0))))


def _router(x, g, wr_pad, bm):
    t, d = x.shape
    s_per = d // LANES
    return pl.pallas_call(
        _router_kernel,
        grid=(t // bm,),
        in_specs=[
            pl.BlockSpec((bm, d), lambda i: (i, 0)),
            pl.BlockSpec((1, d), lambda i: (0, 0)),
            pl.BlockSpec((d, LANES), lambda i: (0, 0)),
        ],
        out_specs=[pl.BlockSpec((bm * s_per, LANES), lambda i: (i, 0)), pl.BlockSpec((bm, LANES), lambda i: (i, 0))],
        out_shape=[jax.ShapeDtypeStruct((t * s_per, LANES), F32), jax.ShapeDtypeStruct((t, LANES), F32)],
        compiler_params=_cparams(("arbitrary",), 40),
        name="moe_router",
    )(x, g, wr_pad)


def _row_copy(src_hbm, dst_ref, sem, src_row, dst_row, s_per):
    src = pl.multiple_of(src_row * s_per, s_per)
    dst = pl.multiple_of(dst_row * s_per, s_per)
    return pltpu.make_async_copy(src_hbm.at[pl.ds(src, s_per)], dst_ref.at[pl.ds(dst, s_per)], sem)


FF_TILE = 2 * LANES


def _expert_kernel(be_ref, tok_ref, hn_hbm, w1_hbm, w3_hbm, w2_hbm, o_ref,
                   rows_ref, w1_ref, w3_ref, w2_ref, sta_ref, stb_ref, rsem, wsem, *, li, s_per):
    i = pl.program_id(0)
    bm = rows_ref.shape[1] // s_per
    dff = w1_ref.shape[1]
    nt = dff // FF_TILE

    def issue(step, slot):
        base = step * bm

        def body(r, c):
            _row_copy(hn_hbm, rows_ref.at[slot], rsem.at[slot], tok_ref[base + r], r, s_per).start()
            return c

        lax.fori_loop(0, bm, body, 0, unroll=8)

    def drain(slot):
        def body(r, c):
            _row_copy(hn_hbm, rows_ref.at[slot], rsem.at[slot], 0, r, s_per).wait()
            return c

        lax.fori_loop(0, bm, body, 0, unroll=8)

    @pl.when(i == 0)
    def _():
        issue(0, 0)

    e = be_ref[i]
    changed = (i == 0) | (e != be_ref[jnp.maximum(i - 1, 0)])

    @pl.when(changed)
    def _():
        def stream(tiles, stage_ref):
            def copy(k):
                src, _ = tiles[k]
                return pltpu.make_async_copy(src, stage_ref.at[k % 2], wsem.at[k % 2])

            copy(0).start()
            for k in range(len(tiles)):
                if k + 1 < len(tiles):
                    copy(k + 1).start()
                copy(k).wait()
                tiles[k][1](stage_ref[k % 2].astype(BF16))

        def col_sink(ref, c):
            def put(v):
                ref[:, c * FF_TILE:(c + 1) * FF_TILE] = v
            return put

        def row_sink(ref, c):
            def put(v):
                ref[c * FF_TILE:(c + 1) * FF_TILE, :] = v
            return put

        cols = []
        for c in range(nt):
            cols.append((w1_hbm.at[li, e, :, pl.ds(c * FF_TILE, FF_TILE)], col_sink(w1_ref, c)))
            cols.append((w3_hbm.at[li, e, :, pl.ds(c * FF_TILE, FF_TILE)], col_sink(w3_ref, c)))
        stream(cols, sta_ref)
        stream([(w2_hbm.at[li, e, pl.ds(c * FF_TILE, FF_TILE), :], row_sink(w2_ref, c)) for c in range(nt)],
               stb_ref)

    for slot in range(2):
        @pl.when(i % 2 == slot)
        def _():
            drain(slot)
            xb = _from_row_tiles(rows_ref.at[slot], s_per).astype(BF16)

            def prefetch(k, n):
                per = -(-bm // n)
                for r in range(k * per, min(bm, (k + 1) * per)):
                    _row_copy(hn_hbm, rows_ref.at[1 - slot], rsem.at[1 - slot],
                              tok_ref[(i + 1) * bm + r], r, s_per).start()

            _to_row_tiles(o_ref, _swiglu_partial(xb, w1_ref, w3_ref, w2_ref, FF_TILE, prefetch))

            @pl.when(i == pl.num_programs(0) - 1)
            def _():
                drain(1 - slot)


def _experts(block_exp, slot_tok, hn, w1, w3, w2, li, bm):
    d = w1.shape[2]
    dff = w1.shape[3]
    s_per = d // LANES
    n = slot_tok.shape[0]
    assert dff % FF_TILE == 0
    slot_tok = jnp.concatenate([slot_tok, jnp.zeros((bm,), slot_tok.dtype)])
    return pl.pallas_call(
        functools.partial(_expert_kernel, li=li, s_per=s_per),
        grid_spec=pltpu.PrefetchScalarGridSpec(
            num_scalar_prefetch=2,
            grid=(n // bm,),
            in_specs=[pl.BlockSpec(memory_space=pl.ANY)] * 4,
            out_specs=pl.BlockSpec((bm * s_per, LANES), lambda i, be, tok: (i, 0)),
            scratch_shapes=[
                pltpu.VMEM((2, bm * s_per, LANES), F32),
                pltpu.VMEM((d, dff), BF16), pltpu.VMEM((d, dff), BF16), pltpu.VMEM((dff, d), BF16),
                pltpu.VMEM((2, d, FF_TILE), w1.dtype), pltpu.VMEM((2, FF_TILE, d), w2.dtype),
                pltpu.SemaphoreType.DMA((2,)), pltpu.SemaphoreType.DMA((2,)),
            ],
        ),
        out_shape=jax.ShapeDtypeStruct((n * s_per, LANES), F32),
        compiler_params=_cparams(("arbitrary",), 56),
        name="moe_experts",
    )(block_exp, slot_tok, hn, w1, w3, w2)


def _combine_kernel(d0_ref, d1_ref, yb_hbm, x_ref, r_ref, g_ref, o_ref, r0_ref, r1_ref, sem, *, s_per, normalize):
    bm = o_ref.shape[0]
    i = pl.program_id(0)

    def issue(step, slot):
        base = step * bm

        def body(r, c):
            _row_copy(yb_hbm, r0_ref.at[slot], sem.at[slot], d0_ref[base + r], r, s_per).start()
            _row_copy(yb_hbm, r1_ref.at[slot], sem.at[slot], d1_ref[base + r], r, s_per).start()
            return c

        lax.fori_loop(0, bm, body, 0, unroll=8)

    def drain(slot):
        def body(r, c):
            _row_copy(yb_hbm, r0_ref.at[slot], sem.at[slot], 0, r, s_per).wait()
            _row_copy(yb_hbm, r1_ref.at[slot], sem.at[slot], 0, r, s_per).wait()
            return c

        lax.fori_loop(0, bm, body, 0, unroll=8)

    @pl.when(i == 0)
    def _():
        issue(0, 0)

    for slot in range(2):
        @pl.when(i % 2 == slot)
        def _():
            @pl.when(i + 1 < pl.num_programs(0))
            def _():
                issue(i + 1, 1 - slot)

            drain(slot)
            route = r_ref[...]
            out = (x_ref[...] + route[:, 2:3] * _from_row_tiles(r0_ref.at[slot], s_per)
                   + route[:, 3:4] * _from_row_tiles(r1_ref.at[slot], s_per))
            o_ref[...] = _rms(out, g_ref[...]) if normalize else out


def _combine(d0, d1, yb, x, route, g_final, normalize, bm):
    t, d = x.shape
    s_per = d // LANES
    return pl.pallas_call(
        functools.partial(_combine_kernel, s_per=s_per, normalize=normalize),
        grid_spec=pltpu.PrefetchScalarGridSpec(
            num_scalar_prefetch=2,
            grid=(t // bm,),
            in_specs=[
                pl.BlockSpec(memory_space=pl.ANY),
                pl.BlockSpec((bm, d), lambda i, a, b: (i, 0)),
                pl.BlockSpec((bm, LANES), lambda i, a, b: (i, 0)),
                pl.BlockSpec((1, d), lambda i, a, b: (0, 0)),
            ],
            out_specs=pl.BlockSpec((bm, d), lambda i, a, b: (i, 0)),
            scratch_shapes=[pltpu.VMEM((2, bm * s_per, LANES), F32), pltpu.VMEM((2, bm * s_per, LANES), F32),
                            pltpu.SemaphoreType.DMA((2,))],
        ),
        out_shape=jax.ShapeDtypeStruct((t, d), F32),
        compiler_params=_cparams(("arbitrary",), 32),
        name="moe_combine",
    )(d0, d1, yb, x, route, g_final)


def _dispatch_tables(idx, bm):
    t = idx.shape[0]
    flat_e = idx.reshape(-1)
    onehot = (flat_e[:, None] == jnp.arange(N_EXPERTS, dtype=I32)[None, :]).astype(I32)
    csum = jnp.cumsum(onehot, axis=0)
    rank = jnp.take_along_axis(csum, flat_e[:, None], axis=1)[:, 0] - 1
    counts = csum[-1]
    padded = (counts + bm - 1) // bm * bm
    pad_end = jnp.cumsum(padded)
    pad_start = pad_end - padded
    dest = (pad_start[flat_e] + rank).astype(I32)
    n_slots = t * TOP_K + N_EXPERTS * bm
    slot_tok = jnp.zeros((n_slots,), I32).at[dest].set(jnp.arange(t * TOP_K, dtype=I32) // TOP_K)
    block_start = jnp.arange(n_slots // bm, dtype=I32) * bm
    block_exp = jnp.minimum(jnp.searchsorted(pad_end, block_start, side='right'), N_EXPERTS - 1).astype(I32)
    return dest.reshape(t, TOP_K), slot_tok, block_exp


def _moe(x, g, w_router, w1, w3, w2, li, g_final, normalize, bm_tok, bm_slot):
    d = x.shape[1]
    wr_pad = jnp.zeros((d, LANES), F32).at[:, :N_EXPERTS].set(w_router)
    hn, route = _router(x, g, wr_pad, bm_tok)
    idx = route[:, :TOP_K].astype(I32)
    dest, slot_tok, block_exp = _dispatch_tables(idx, bm_slot)
    yb = _experts(block_exp, slot_tok, hn, w1, w3, w2, li, bm_slot)
    return _combine(dest[:, 0], dest[:, 1], yb, x, route, g_final, normalize, bm_tok)


def _norm_kernel(x_ref, g_ref, o_ref):
    o_ref[...] = _rms(x_ref[...], g_ref[...])


def _final_norm(x, g, bm):
    t, d = x.shape
    return pl.pallas_call(
        _norm_kernel,
        grid=(t // bm,),
        in_specs=[pl.BlockSpec((bm, d), lambda i: (i, 0)), pl.BlockSpec((1, d), lambda i: (0, 0))],
        out_specs=pl.BlockSpec((bm, d), lambda i: (i, 0)),
        out_shape=jax.ShapeDtypeStruct((t, d), F32),
        compiler_params=_cparams(("arbitrary",), 32),
        name="final_norm",
    )(x, g)


def _rope_tables(seq):
    half = ATTN_HEAD_DIM // 2
    inv_freq = 1.0 / (ROPE_THETA ** (jnp.arange(half, dtype=F32) / half))
    ang = jnp.arange(seq, dtype=F32)[:, None] * inv_freq[None, :]
    cos = jnp.tile(jnp.cos(ang), (1, 4))
    sin = jnp.sin(ang)
    return cos, jnp.concatenate([-sin, sin, -sin, sin], axis=1)


def _block(n, pref):
    return pref if n % pref == 0 else n


def kernel(x, norm_mix, w_in, attn_lambda, attn_subln, ssm_lam_re, ssm_lam_im, ssm_log_dt, ssm_b_re, ssm_b_im, ssm_c_re, ssm_c_im, ssm_d, w_glu, w_branch, w_out, norm_ffn, ffn_w1, ffn_w3, ffn_w2, moe_router, moe_w1, moe_w3, moe_w2, norm_final):
    bsz, seq, d = x.shape
    depth = w_in.shape[0]
    t = bsz * seq
    cos_t, sin_t = _rope_tables(seq)
    xf = x.reshape(t, d)
    bm = _block(seq, 1024)
    w_all = w_in.astype(BF16)
    a2, b2, c2, apow = jax.vmap(_s5_tables)(ssm_lam_re, ssm_lam_im, ssm_log_dt, ssm_b_re, ssm_b_im,
                                            ssm_c_re, ssm_c_im, ssm_d)
    wglu, wbr, wout = w_glu.astype(BF16), w_branch.astype(BF16), w_out.astype(BF16)
    fw1, fw3, fw2 = ffn_w1.astype(BF16), ffn_w3.astype(BF16), ffn_w2.astype(BF16)
    for layer in range(depth):
        proj, u = _inproj(xf, norm_mix[layer][None], w_all, layer, cos_t, sin_t, seq, _block(seq, 512))

        lam_init = jnp.full((1,), 0.8 - 0.6 * math.exp(-0.3 * layer), F32)
        a = _attention(proj, lam_init, attn_lambda[layer], attn_subln[layer][None], bsz, seq,
                       _block(seq, 1024), _block(seq // 2, 512))

        y = _s5(u, a2[layer], b2[layer], c2[layer], apow[layer], bsz, _block(seq // SSM_CHUNK, 512))

        xf = _merge(a, y, proj, xf, wglu, wbr, wout, layer, _block(seq, 512))

        i = layer // 2
        if layer % 2 == 0:
            xf = _ffn(xf, norm_ffn[layer][None], fw1, fw3, fw2, i, bm)
        else:
            last = layer == depth - 1
            xf = _moe(xf, norm_ffn[layer][None], moe_router[i], moe_w1, moe_w3, moe_w2, i,
                      norm_final[None], last, _block(seq, 256), 512)
    if depth % 2 == 1:
        xf = _final_norm(xf, norm_final[None], bm)
    return xf.reshape(bsz, seq, d)
```

```python
import functools
import math

import jax
import jax.numpy as jnp
from jax import lax
from jax.experimental import pallas as pl
from jax.experimental.pallas import tpu as pltpu

F32 = jnp.float32
BF16 = jnp.bfloat16
I32 = jnp.int32

RMS_EPS = 1e-6
ROPE_THETA = 10000.0
LOG2E = 1.4426950408889634

LANES = 128
SUBLANES = 8
VMEM_BYTES_V7X = 64 * 1024 * 1024

ATTN_HEADS = 4
ATTN_HEAD_DIM = 64
HEAD_SLAB = 2 * ATTN_HEAD_DIM
SSM_GROUP_CH = 16
SSM_STATE = 64
SSM_OCTET = LANES // SSM_GROUP_CH
SSM_CHUNK = 8
N_EXPERTS = 8
TOP_K = 2
SEG = 512


def _cparams(sem, vmem_mb):
    return pltpu.CompilerParams(dimension_semantics=sem, vmem_limit_bytes=vmem_mb * 1024 * 1024)


def _rms(x, g):
    return x * lax.rsqrt(jnp.mean(x * x, axis=-1, keepdims=True) + RMS_EPS) * g


def _sigmoid(x):
    return 0.5 * jnp.tanh(0.5 * x) + 0.5


def _inproj_kernel(x_ref, g_ref, w_ref, cos_ref, sin_ref, o_ref, u_ref, *, qscale):
    xn = _rms(x_ref[...], g_ref[...]).astype(BF16)
    cos = cos_ref[...]
    sin = sin_ref[...]

    half = ATTN_HEAD_DIM // 2
    first = (lax.broadcasted_iota(I32, cos.shape, 1) % ATTN_HEAD_DIM) < half

    def rot(a):
        parts = []
        for s in range(0, a.shape[1], HEAD_SLAB):
            blk = a[:, s:s + HEAD_SLAB]
            partner = jnp.where(first, pltpu.roll(blk, HEAD_SLAB - half, 1), pltpu.roll(blk, half, 1))
            parts.append(blk * cos + partner * sin)
        return jnp.concatenate(parts, axis=1)

    for j in range(w_ref.shape[1] // SEG):
        acc = jnp.dot(xn, w_ref[:, j * SEG:(j + 1) * SEG], preferred_element_type=F32)
        if j == 0:
            acc = rot(acc) * qscale
        elif j == 1:
            acc = rot(acc)
        elif j == 3:
            u_ref[...] = acc
        elif j >= 4:
            acc = _sigmoid(acc)
        o_ref[:, j * SEG:(j + 1) * SEG] = acc.astype(BF16)


def _inproj(x, g, w, layer, cos_t, sin_t, seq, bm):
    t, d = x.shape
    n = w.shape[2]
    qscale = ATTN_HEAD_DIM ** -0.5 * LOG2E
    nseq = seq // bm
    return pl.pallas_call(
        functools.partial(_inproj_kernel, qscale=qscale),
        grid=(t // bm,),
        in_specs=[
            pl.BlockSpec((bm, d), lambda i: (i, 0)),
            pl.BlockSpec((1, d), lambda i: (0, 0)),
            pl.BlockSpec((None, d, n), lambda i: (layer, 0, 0)),
            pl.BlockSpec((bm, HEAD_SLAB), lambda i: (i % nseq, 0)),
            pl.BlockSpec((bm, HEAD_SLAB), lambda i: (i % nseq, 0)),
        ],
        out_specs=[pl.BlockSpec((bm, n), lambda i: (i, 0)), pl.BlockSpec((bm, SEG), lambda i: (i, 0))],
        out_shape=[jax.ShapeDtypeStruct((t, n), BF16), jax.ShapeDtypeStruct((t, SEG), F32)],
        compiler_params=_cparams(("arbitrary",), 52),
        name="inproj",
    )(x, g, w, cos_t, sin_t)


ONES_ROWS = 16


def _attn_kernel(sc_ref, lp_ref, q_ref, k_ref, v_ref, g_ref, o_ref, vt_ref, acc_ref, m_ref,
                 s0_ref, s1_ref, p0_ref, p1_ref, x0_ref, x1_ref, *, bk):
    qi = pl.program_id(2)
    seq = k_ref.shape[0]
    bq = q_ref.shape[0]
    nkv = seq // bk
    assert nkv >= 2 and nkv % 2 == 0

    @pl.when(qi == 0)
    def _():
        for c in range(nkv):
            vt_ref[c, :HEAD_SLAB, :] = v_ref[c * bk:(c + 1) * bk, :].astype(F32).T.astype(BF16)
            vt_ref[c, HEAD_SLAB:, :] = jnp.ones((ONES_ROWS, bk), BF16)

    qt = q_ref[...].astype(F32).T
    row = lax.broadcasted_iota(I32, qt.shape, 0)
    half0 = row < ATTN_HEAD_DIM
    qst = jnp.concatenate([jnp.where(half0, qt, 0.0), jnp.where(half0, 0.0, qt)], axis=1).astype(BF16)

    acc_ref[...] = jnp.zeros_like(acc_ref)
    m_ref[...] = jnp.full_like(m_ref, -1e30)
    s_refs = (s0_ref, s1_ref)
    p_refs = (p0_ref, p1_ref)
    x_refs = (x0_ref, x1_ref)

    def qk(j, cur):
        off = pl.multiple_of(j * bk, bk)
        st = jnp.dot(k_ref[pl.ds(off, bk), :], qst, preferred_element_type=F32)
        s_refs[cur][...] = st
        x_refs[cur][...] = jnp.max(st, axis=0, keepdims=True)

    def pv(j, cur, alpha):
        acc_ref[...] = acc_ref[...] * alpha + jnp.dot(vt_ref[j], p_refs[cur][...],
                                                      preferred_element_type=F32)

    def softmax(cur):
        m_prev = m_ref[...]
        m_new = jnp.maximum(m_prev, x_refs[cur][...])
        p_refs[cur][...] = jnp.exp2((s_refs[cur][...] - m_new).astype(BF16))
        m_ref[...] = m_new
        return jnp.exp2(m_prev - m_new)

    def stage(j, cur, alpha):
        qk(j + 1, 1 - cur)
        pv(j - 1, 1 - cur, alpha)
        return softmax(cur)

    qk(0, 0)
    qk(1, 1)
    alpha = softmax(0)

    def pair(jj, alpha):
        j = 1 + 2 * jj
        return stage(j + 1, 0, stage(j, 1, alpha))

    alpha = lax.fori_loop(0, (nkv - 2) // 2, pair, alpha)
    pv(nkv - 2, 0, alpha)
    alpha = softmax(1)
    pv(nkv - 1, 1, alpha)

    lam_init = sc_ref[0]
    lp = lp_ref[...]
    lam = (jnp.exp(jnp.sum(lp[0:1] * lp[1:2], axis=-1, keepdims=True))
           - jnp.exp(jnp.sum(lp[2:3] * lp[3:4], axis=-1, keepdims=True)) + lam_init)
    acc = acc_ref[...]
    o = acc[:HEAD_SLAB] / acc[HEAD_SLAB:HEAD_SLAB + 1]
    a = o[:, :bq] - lam * o[:, bq:]
    a = a * lax.rsqrt(jnp.mean(a * a, axis=0, keepdims=True) + RMS_EPS) * (1.0 - lam_init)
    o_ref[...] = (a.T * g_ref[...]).astype(BF16)


def _attention(proj, lam_init, lam_p, subln, bsz, seq, bq, bk):
    t = proj.shape[0]
    nq = seq // bq
    koff = SEG // HEAD_SLAB
    return pl.pallas_call(
        functools.partial(_attn_kernel, bk=bk),
        grid=(bsz, ATTN_HEADS, nq),
        in_specs=[
            pl.BlockSpec(memory_space=pltpu.SMEM),
            pl.BlockSpec((4, ATTN_HEAD_DIM), lambda b, h, i: (0, 0)),
            pl.BlockSpec((bq, HEAD_SLAB), lambda b, h, i: (b * nq + i, h)),
            pl.BlockSpec((seq, HEAD_SLAB), lambda b, h, i: (b, koff + h)),
            pl.BlockSpec((seq, HEAD_SLAB), lambda b, h, i: (b, 2 * koff + h)),
            pl.BlockSpec((1, HEAD_SLAB), lambda b, h, i: (0, 0)),
        ],
        out_specs=pl.BlockSpec((bq, HEAD_SLAB), lambda b, h, i: (b * nq + i, h)),
        out_shape=jax.ShapeDtypeStruct((t, ATTN_HEADS * HEAD_SLAB), BF16),
        scratch_shapes=[
            pltpu.VMEM((seq // bk, HEAD_SLAB + ONES_ROWS, bk), BF16),
            pltpu.VMEM((HEAD_SLAB + ONES_ROWS, 2 * bq), F32),
            pltpu.VMEM((1, 2 * bq), F32),
            pltpu.VMEM((bk, 2 * bq), F32),
            pltpu.VMEM((bk, 2 * bq), F32),
            pltpu.VMEM((bk, 2 * bq), BF16),
            pltpu.VMEM((bk, 2 * bq), BF16),
            pltpu.VMEM((1, 2 * bq), F32),
            pltpu.VMEM((1, 2 * bq), F32),
        ],
        compiler_params=_cparams(("arbitrary", "arbitrary", "arbitrary"), 56),
        name="diff_attention",
    )(lam_init, lam_p, proj, proj, proj, subln)


SSM_TILE = SSM_OCTET * SSM_GROUP_CH
SSM_CW = SSM_CHUNK * SSM_TILE
SSM_SW = 4 * SSM_OCTET * SSM_STATE
SSM_QPER = SSM_SW // LANES
SCAN_SEGS = SUBLANES


def _s5_tables(lam_re, lam_im, log_dt, b_re, b_im, c_re, c_im, d_skip):
    hp = lax.Precision.HIGHEST
    ll = SSM_CHUNK
    g = lam_re.shape[1]
    no = g // SSM_OCTET
    lam = lax.complex(lam_re, lam_im)
    ldt = lam * jnp.exp(log_dt)[..., None]
    a = jnp.exp(ldt)
    bbar = ((a - 1.0) / lam)[..., None] * lax.complex(b_re, b_im)
    cmat = lax.complex(c_re, c_im)
    n = jnp.arange(ll + 1, dtype=F32)
    pw = jnp.exp(ldt[:, None] * n[None, :, None, None])

    def kern(d):
        return jnp.real(jnp.einsum('ghp,dgp,gpk->dghk', cmat[d], pw[d, :ll], bbar[d], precision=hp))

    kf, kb = kern(0), kern(1)
    tt = jnp.arange(ll)
    lag = tt[None, :] - tt[:, None]
    mf = jnp.where((lag >= 0)[:, :, None, None, None], kf[jnp.clip(lag, 0, ll - 1)], 0.0)
    mb = jnp.where((lag <= 0)[:, :, None, None, None], kb[jnp.clip(-lag, 0, ll - 1)], 0.0)
    skip = (jnp.eye(ll, dtype=F32)[:, :, None, None, None]
            * (d_skip.reshape(g, SSM_GROUP_CH)[None, None, :, :, None] * jnp.eye(SSM_GROUP_CH, dtype=F32)[None, None, None]))
    m = (mf + mb + skip).reshape(ll, ll, no, SSM_OCTET, SSM_GROUP_CH, SSM_GROUP_CH)
    a2 = m.transpose(2, 0, 3, 5, 1, 4).reshape(no, SSM_CW, ll * SSM_GROUP_CH)

    ein = jnp.stack([pw[0, ll - 1 - tt][..., None] * bbar[0][None],
                     pw[1, tt][..., None] * bbar[1][None]])
    ein = jnp.stack([jnp.real(ein), jnp.imag(ein)], axis=1)
    ein = ein.reshape(2, 2, ll, no, SSM_OCTET, SSM_STATE, SSM_GROUP_CH)
    b2 = ein.transpose(3, 2, 4, 6, 0, 1, 5).reshape(no, SSM_CW, 4 * SSM_STATE)

    eout = jnp.stack([cmat[0][None] * pw[0, tt + 1][:, :, None, :],
                      cmat[1][None] * pw[1, ll - tt][:, :, None, :]])
    eout = jnp.stack([jnp.real(eout), -jnp.imag(eout)], axis=1)
    eout = eout.reshape(2, 2, ll, no, SSM_OCTET, SSM_GROUP_CH, SSM_STATE)
    c2 = eout.transpose(3, 0, 1, 4, 6, 2, 5).reshape(no, SSM_SW, ll * SSM_GROUP_CH)

    al = pw[:, ll]
    apow = jnp.stack([jnp.real(al), jnp.imag(al)], axis=1)
    apow = apow.reshape(2, 2, no, SSM_OCTET * SSM_STATE).transpose(2, 0, 1, 3)
    return a2.astype(BF16), b2.astype(BF16), c2.astype(BF16), apow.astype(F32)


def _to_row_tiles(ref, x):
    n = x.shape[0]
    s_per = x.shape[1] // LANES
    for s in range(s_per):
        ref[pl.ds(s, n, stride=s_per), :] = x[:, s * LANES:(s + 1) * LANES]


def _from_row_tiles(ref, s_per):
    n = ref.shape[0] // s_per
    return jnp.concatenate([ref[pl.ds(s, n, stride=s_per), :] for s in range(s_per)], axis=1)


def _shr(idx, n):
    assert n & (n - 1) == 0
    return idx >> (n.bit_length() - 1)


def _group_of(idx, width):
    return _shr(idx, width) & (SSM_OCTET - 1)


def _spread(compact, rep_rows, rep_cols, inner, row_w, col_w):
    q = lax.broadcasted_iota(I32, (rep_rows, rep_cols), 0)
    c = lax.broadcasted_iota(I32, (rep_rows, rep_cols), 1)
    rep = ((_shr(q, inner) == _shr(c, inner * SSM_OCTET)) & ((q & (inner - 1)) == (c & (inner - 1)))).astype(BF16)
    full = jnp.dot(compact, rep, preferred_element_type=F32)
    r = lax.broadcasted_iota(I32, full.shape, 0)
    cc = lax.broadcasted_iota(I32, full.shape, 1)
    return jnp.where(_group_of(r, row_w) == _group_of(cc, col_w), full, 0.0).astype(BF16)


def _seg_rows(q, k, seglen):
    return pl.ds(q * SCAN_SEGS + k, seglen, stride=SSM_QPER * SCAN_SEGS)


def _s5_local_kernel(u_ref, a2_ref, b2_ref, yi_ref, st_ref, toe_ref, sin_ref, *, parts):
    ncb = yi_ref.shape[0]
    part = pl.program_id(2)
    segs = SCAN_SEGS // parts
    seglen = ncb // segs

    @pl.when((pl.program_id(1) == 0) & (part == 0))
    def _():
        toe_ref[...] = _spread(a2_ref[...], a2_ref.shape[1], SSM_CW, SSM_GROUP_CH, SSM_GROUP_CH, SSM_GROUP_CH)
        sin_ref[...] = _spread(b2_ref[...], b2_ref.shape[1], SSM_SW, SSM_STATE, SSM_GROUP_CH, SSM_STATE)

    u = jnp.concatenate([u_ref[pl.ds(j, ncb, stride=SSM_CHUNK), :] for j in range(SSM_CHUNK)],
                        axis=1).astype(BF16)
    yi_ref[...] = jnp.dot(u, toe_ref[...], preferred_element_type=F32)
    st = jnp.dot(u, sin_ref[...], preferred_element_type=F32)
    for pv in range(parts):
        @pl.when(part == pv)
        def _():
            for kk in range(segs):
                for q in range(SSM_QPER):
                    st_ref[_seg_rows(q, pv * segs + kk, seglen), :] = (
                        st[kk * seglen:(kk + 1) * seglen, q * LANES:(q + 1) * LANES])


def _cmul(ar, ai, xr, xi):
    return ar * xr - ai * xi, ar * xi + ai * xr


def _s5_scan_kernel(st_ref, ap_ref, x_ref):
    wt = SSM_QPER // 4
    seglen = st_ref.shape[0] // SSM_QPER
    assert seglen & (seglen - 1) == 0
    ap = ap_ref[...]

    def coef(d, r):
        return [jnp.broadcast_to(ap[d, r:r + 1, t * LANES:(t + 1) * LANES], (SCAN_SEGS, LANES)) for t in range(wt)]

    far, fai, bar, bai = coef(0, 0), coef(0, 1), coef(1, 0), coef(1, 1)

    def scan(init, store):
        def step(i, carry):
            fr, fi, br, bi = [list(c) for c in carry]
            rf = i * SSM_QPER
            rb = (seglen - 1 - i) * SSM_QPER
            for t in range(wt):
                if store:
                    x_ref[rf + t] = fr[t]
                    x_ref[rf + wt + t] = fi[t]
                    x_ref[rb + 2 * wt + t] = br[t]
                    x_ref[rb + 3 * wt + t] = bi[t]
                pr, pi = _cmul(far[t], fai[t], fr[t], fi[t])
                fr[t] = pr + st_ref[rf + t]
                fi[t] = pi + st_ref[rf + wt + t]
                pr, pi = _cmul(bar[t], bai[t], br[t], bi[t])
                br[t] = pr + st_ref[rb + 2 * wt + t]
                bi[t] = pi + st_ref[rb + 3 * wt + t]
            return tuple(fr), tuple(fi), tuple(br), tuple(bi)

        return lax.fori_loop(0, seglen, step, init)

    z = tuple(jnp.zeros((SCAN_SEGS, LANES), F32) for _ in range(wt))
    fr, fi, br, bi = scan((z, z, z, z), store=False)

    z1 = jnp.zeros((1, LANES), F32)
    fcr, fci, bcr, bci = [], [], [], []
    for t in range(wt):
        fsr, fsi, bsr, bsi = far[t][0:1], fai[t][0:1], bar[t][0:1], bai[t][0:1]
        for _ in range(seglen.bit_length() - 1):
            fsr, fsi = _cmul(fsr, fsi, fsr, fsi)
            bsr, bsi = _cmul(bsr, bsi, bsr, bsi)
        cr, ci = z1, z1
        rs, is_ = [], []
        for k in range(SCAN_SEGS):
            rs.append(cr)
            is_.append(ci)
            cr, ci = _cmul(fsr, fsi, cr, ci)
            cr, ci = cr + fr[t][k:k + 1], ci + fi[t][k:k + 1]
        fcr.append(jnp.concatenate(rs, axis=0))
        fci.append(jnp.concatenate(is_, axis=0))
        cr, ci = z1, z1
        rs, is_ = [None] * SCAN_SEGS, [None] * SCAN_SEGS
        for k in reversed(range(SCAN_SEGS)):
            rs[k] = cr
            is_[k] = ci
            cr, ci = _cmul(bsr, bsi, cr, ci)
            cr, ci = cr + br[t][k:k + 1], ci + bi[t][k:k + 1]
        bcr.append(jnp.concatenate(rs, axis=0))
        bci.append(jnp.concatenate(is_, axis=0))

    scan((tuple(fcr), tuple(fci), tuple(bcr), tuple(bci)), store=True)


def _s5_out_kernel(yi_ref, x_ref, c2_ref, y_ref, sout_ref, *, parts):
    ncb = yi_ref.shape[0]
    part = pl.program_id(2)
    segs = SCAN_SEGS // parts
    seglen = ncb // segs

    @pl.when((pl.program_id(1) == 0) & (part == 0))
    def _():
        sout_ref[...] = _spread(c2_ref[...], c2_ref.shape[1], SSM_CW, SSM_GROUP_CH, SSM_STATE, SSM_GROUP_CH)

    for pv in range(parts):
        @pl.when(part == pv)
        def _():
            x = jnp.concatenate(
                [jnp.concatenate([x_ref[_seg_rows(q, pv * segs + kk, seglen), :] for q in range(SSM_QPER)], axis=1)
                 for kk in range(segs)], axis=0).astype(BF16)
            y = yi_ref[...] + jnp.dot(x, sout_ref[...], preferred_element_type=F32)
            for t in range(SSM_CHUNK):
                y_ref[pl.ds(t, ncb, stride=SSM_CHUNK), :] = y[:, t * SSM_TILE:(t + 1) * SSM_TILE]


def _s5(u, a2, b2, c2, apow, bsz, ncb):
    t, width = u.shape
    no = width // SSM_TILE
    nc = t // SSM_CHUNK
    ncseq = nc // bsz
    parts = ncseq // ncb
    assert SCAN_SEGS % parts == 0
    rows = ncb * SSM_CHUNK
    slab = ncseq * SSM_QPER
    yi, st = pl.pallas_call(
        functools.partial(_s5_local_kernel, parts=parts),
        grid=(no, bsz, parts),
        in_specs=[
            pl.BlockSpec((rows, SSM_TILE), lambda o, b, p: (b * parts + p, o)),
            pl.BlockSpec((None,) + a2.shape[1:], lambda o, b, p: (o, 0, 0)),
            pl.BlockSpec((None,) + b2.shape[1:], lambda o, b, p: (o, 0, 0)),
        ],
        out_specs=[
            pl.BlockSpec((None, ncb, SSM_CW), lambda o, b, p: (o, b * parts + p, 0)),
            pl.BlockSpec((None, None, slab, LANES), lambda o, b, p: (o, b, 0, 0)),
        ],
        out_shape=[jax.ShapeDtypeStruct((no, nc, SSM_CW), F32),
                   jax.ShapeDtypeStruct((no, bsz, slab, LANES), F32)],
        scratch_shapes=[pltpu.VMEM((SSM_CW, SSM_CW), BF16), pltpu.VMEM((SSM_CW, SSM_SW), BF16)],
        compiler_params=_cparams(("arbitrary", "arbitrary", "arbitrary"), 52),
        name="s5_local",
    )(u, a2, b2)
    tiles = slab // SCAN_SEGS
    xin = pl.pallas_call(
        _s5_scan_kernel,
        grid=(no, bsz),
        in_specs=[
            pl.BlockSpec((None, None, tiles, SCAN_SEGS, LANES), lambda o, b: (o, b, 0, 0, 0)),
            pl.BlockSpec((None, 2, 2, SSM_SW // 4), lambda o, b: (o, 0, 0, 0)),
        ],
        out_specs=pl.BlockSpec((None, None, tiles, SCAN_SEGS, LANES), lambda o, b: (o, b, 0, 0, 0)),
        out_shape=jax.ShapeDtypeStruct((no, bsz, tiles, SCAN_SEGS, LANES), F32),
        compiler_params=_cparams(("arbitrary", "arbitrary"), 48),
        name="s5_scan",
    )(st.reshape(no, bsz, tiles, SCAN_SEGS, LANES), apow)
    return pl.pallas_call(
        functools.partial(_s5_out_kernel, parts=parts),
        grid=(no, bsz, parts),
        in_specs=[
            pl.BlockSpec((None, ncb, SSM_CW), lambda o, b, p: (o, b * parts + p, 0)),
            pl.BlockSpec((None, None, slab, LANES), lambda o, b, p: (o, b, 0, 0)),
            pl.BlockSpec((None,) + c2.shape[1:], lambda o, b, p: (o, 0, 0)),
        ],
        out_specs=pl.BlockSpec((rows, SSM_TILE), lambda o, b, p: (b * parts + p, o)),
        out_shape=jax.ShapeDtypeStruct((t, width), F32),
        scratch_shapes=[pltpu.VMEM((SSM_SW, SSM_CW), BF16)],
        compiler_params=_cparams(("arbitrary", "arbitrary", "arbitrary"), 52),
        name="s5_out",
    )(yi, xin.reshape(no, bsz, slab, LANES), c2)


def _merge_kernel(a_ref, y_ref, g0_ref, g1_ref, x_ref, wglu_ref, wb0_ref, wb1_ref, wout_ref, o_ref):
    y = y_ref[...]
    s = 0.5 * y * (1.0 + lax.erf(y * (2.0 ** -0.5)))
    glu = jnp.dot(s.astype(BF16), wglu_ref[...], preferred_element_type=F32)
    s = s * jax.nn.sigmoid(glu)
    pa = jnp.dot(a_ref[...], wb0_ref[...], preferred_element_type=F32)
    ps = jnp.dot(s.astype(BF16), wb1_ref[...], preferred_element_type=F32)
    merged = g0_ref[...].astype(F32) * pa + g1_ref[...].astype(F32) * ps
    o_ref[...] = x_ref[...] + jnp.dot(merged.astype(BF16), wout_ref[...], preferred_element_type=F32)


def _merge(a, y, proj, x, wglu, wbr, wout, layer, bm):
    t, d = x.shape
    aw = a.shape[1]
    g0 = 4 * SEG // d
    return pl.pallas_call(
        _merge_kernel,
        grid=(t // bm,),
        in_specs=[
            pl.BlockSpec((bm, aw), lambda i: (i, 0)),
            pl.BlockSpec((bm, aw), lambda i: (i, 0)),
            pl.BlockSpec((bm, d), lambda i: (i, g0)),
            pl.BlockSpec((bm, d), lambda i: (i, g0 + 1)),
            pl.BlockSpec((bm, d), lambda i: (i, 0)),
            pl.BlockSpec((None,) + wglu.shape[1:], lambda i: (layer, 0, 0)),
            pl.BlockSpec((None, None) + wbr.shape[2:], lambda i: (layer, 0, 0, 0)),
            pl.BlockSpec((None, None) + wbr.shape[2:], lambda i: (layer, 1, 0, 0)),
            pl.BlockSpec((None,) + wout.shape[1:], lambda i: (layer, 0, 0)),
        ],
        out_specs=pl.BlockSpec((bm, d), lambda i: (i, 0)),
        out_shape=jax.ShapeDtypeStruct((t, d), F32),
        compiler_params=_cparams(("arbitrary",), 48),
        name="merge",
    )(a, y, proj, proj, x, wglu, wbr, wbr, wout)


def _swiglu_partial(h, w1_ref, w3_ref, w2_ref, sub, side_work=None):
    ffc = w1_ref.shape[-1]
    acc = None
    for k, s in enumerate(range(0, ffc, sub)):
        e = min(s + sub, ffc)
        if side_work is not None:
            side_work(k, -(-ffc // sub))
        a = jnp.dot(h, w1_ref[:, s:e], preferred_element_type=F32)
        b = jnp.dot(h, w3_ref[:, s:e], preferred_element_type=F32)
        tt = (a * jax.nn.sigmoid(a) * b).astype(BF16)
        c = jnp.dot(tt, w2_ref[s:e, :], preferred_element_type=F32)
        acc = c if acc is None else acc + c
    return acc


def _ffn_kernel(x_ref, g_ref, w1_ref, w3_ref, w2_ref, o_ref, hn_ref, *, sub):
    f = pl.program_id(1)

    @pl.when(f == 0)
    def _():
        x = x_ref[...]
        hn_ref[...] = _rms(x, g_ref[...]).astype(BF16)
        o_ref[...] = x

    o_ref[...] += _swiglu_partial(hn_ref[...], w1_ref, w3_ref, w2_ref, sub)


def _ffn_tile(dff):
    return dff // 2 if (dff // 2) % LANES == 0 else dff


def _ffn(x, g, w1, w3, w2, li, bm):
    t, d = x.shape
    dff = w1.shape[2]
    ffc = _ffn_tile(dff)
    return pl.pallas_call(
        functools.partial(_ffn_kernel, sub=2 * LANES),
        grid=(t // bm, dff // ffc),
        in_specs=[
            pl.BlockSpec((bm, d), lambda i, f: (i, 0)),
            pl.BlockSpec((1, d), lambda i, f: (0, 0)),
            pl.BlockSpec((None, d, ffc), lambda i, f: (li, 0, f)),
            pl.BlockSpec((None, d, ffc), lambda i, f: (li, 0, f)),
            pl.BlockSpec((None, ffc, d), lambda i, f: (li, f, 0)),
        ],
        out_specs=pl.BlockSpec((bm, d), lambda i, f: (i, 0)),
        out_shape=jax.ShapeDtypeStruct((t, d), F32),
        scratch_shapes=[pltpu.VMEM((bm, d), BF16)],
        compiler_params=_cparams(("arbitrary", "arbitrary"), 56),
        name="ffn_dense",
    )(x, g, w1, w3, w2)


def _router_kernel(x_ref, g_ref, wr_ref, hn_ref, r_ref):
    hn = _rms(x_ref[...], g_ref[...])
    _to_row_tiles(hn_ref, hn)
    logits = jnp.dot(hn, wr_ref[...], preferred_element_type=F32, precision=lax.Precision.HIGHEST)
    lane = lax.broadcasted_iota(I32, logits.shape, 1)
    neg = jnp.float32(-1e30)
    logits = jnp.where(lane < N_EXPERTS, logits, neg)
    m1 = jnp.max(logits, axis=-1, keepdims=True)
    i1 = jnp.min(jnp.where(logits == m1, lane, LANES), axis=-1, keepdims=True)
    rest = jnp.where(lane == i1, neg, logits)
    m2 = jnp.max(rest, axis=-1, keepdims=True)
    i2 = jnp.min(jnp.where(rest == m2, lane, LANES), axis=-1, keepdims=True)
    e2 = jnp.exp(m2 - m1)
    g1 = 1.0 / (1.0 + e2)
    g2 = e2 / (1.0 + e2)
    r_ref[...] = jnp.where(lane == 0, i1.astype(F32),
                           jnp.where(lane == 1, i2.astype(F32),
                                     jnp.where(lane == 2, g1, jnp.where(lane == 3, g2, 0.0))))


def _router(x, g, wr_pad, bm):
    t, d = x.shape
    s_per = d // LANES
    return pl.pallas_call(
        _router_kernel,
        grid=(t // bm,),
        in_specs=[
            pl.BlockSpec((bm, d), lambda i: (i, 0)),
            pl.BlockSpec((1, d), lambda i: (0, 0)),
            pl.BlockSpec((d, LANES), lambda i: (0, 0)),
        ],
        out_specs=[pl.BlockSpec((bm * s_per, LANES), lambda i: (i, 0)), pl.BlockSpec((bm, LANES), lambda i: (i, 0))],
        out_shape=[jax.ShapeDtypeStruct((t * s_per, LANES), F32), jax.ShapeDtypeStruct((t, LANES), F32)],
        compiler_params=_cparams(("arbitrary",), 40),
        name="moe_router",
    )(x, g, wr_pad)


def _row_copy(src_hbm, dst_ref, sem, src_row, dst_row, s_per):
    src = pl.multiple_of(src_row * s_per, s_per)
    dst = pl.multiple_of(dst_row * s_per, s_per)
    return pltpu.make_async_copy(src_hbm.at[pl.ds(src, s_per)], dst_ref.at[pl.ds(dst, s_per)], sem)


FF_TILE = 2 * LANES


def _expert_kernel(be_ref, tok_ref, hn_hbm, w1_hbm, w3_hbm, w2_hbm, o_ref,
                   rows_ref, w1_ref, w3_ref, w2_ref, sta_ref, stb_ref, rsem, wsem, *, li, s_per):
    i = pl.program_id(0)
    bm = rows_ref.shape[1] // s_per
    dff = w1_ref.shape[1]
    nt = dff // FF_TILE

    def issue(step, slot):
        base = step * bm

        def body(r, c):
            _row_copy(hn_hbm, rows_ref.at[slot], rsem.at[slot], tok_ref[base + r], r, s_per).start()
            return c

        lax.fori_loop(0, bm, body, 0, unroll=8)

    def drain(slot):
        def body(r, c):
            _row_copy(hn_hbm, rows_ref.at[slot], rsem.at[slot], 0, r, s_per).wait()
            return c

        lax.fori_loop(0, bm, body, 0, unroll=8)

    @pl.when(i == 0)
    def _():
        issue(0, 0)

    nblk = pl.num_programs(0)
    active = i < be_ref[nblk]
    e = be_ref[i]
    changed = active & ((i == 0) | (e != be_ref[jnp.maximum(i - 1, 0)]))

    @pl.when(changed)
    def _():
        def stream(tiles, stage_ref):
            def copy(k):
                src, _ = tiles[k]
                return pltpu.make_async_copy(src, stage_ref.at[k % 2], wsem.at[k % 2])

            copy(0).start()
            for k in range(len(tiles)):
                if k + 1 < len(tiles):
                    copy(k + 1).start()
                copy(k).wait()
                tiles[k][1](stage_ref[k % 2].astype(BF16))

        def row_sink(ref, c, rows):
            def put(v):
                ref[c * rows:(c + 1) * rows, :] = v
            return put

        ra = sta_ref.shape[1]
        up = []
        for c in range(w1_ref.shape[0] // ra):
            up.append((w1_hbm.at[li, e, pl.ds(c * ra, ra), :], row_sink(w1_ref, c, ra)))
            up.append((w3_hbm.at[li, e, pl.ds(c * ra, ra), :], row_sink(w3_ref, c, ra)))
        stream(up, sta_ref)
        stream([(w2_hbm.at[li, e, pl.ds(c * FF_TILE, FF_TILE), :], row_sink(w2_ref, c, FF_TILE))
                for c in range(nt)], stb_ref)

    for slot in range(2):
        @pl.when((i % 2 == slot) & active)
        def _():
            drain(slot)
            xb = _from_row_tiles(rows_ref.at[slot], s_per).astype(BF16)

            def prefetch(k, n):
                per = -(-bm // n)
                for r in range(k * per, min(bm, (k + 1) * per)):
                    _row_copy(hn_hbm, rows_ref.at[1 - slot], rsem.at[1 - slot],
                              tok_ref[(i + 1) * bm + r], r, s_per).start()

            _to_row_tiles(o_ref, _swiglu_partial(xb, w1_ref, w3_ref, w2_ref, FF_TILE, prefetch))

        @pl.when((i % 2 == slot) & jnp.logical_not(active))
        def _():
            drain(slot)
            issue(i + 1, 1 - slot)
            o_ref[...] = jnp.zeros_like(o_ref)

        @pl.when((i % 2 == slot) & (i == nblk - 1))
        def _():
            drain(1 - slot)


def _experts(block_exp, slot_tok, hn, w1, w3, w2, li, bm):
    d = w1.shape[2]
    dff = w1.shape[3]
    s_per = d // LANES
    n = slot_tok.shape[0]
    assert dff % FF_TILE == 0
    slot_tok = jnp.concatenate([slot_tok, jnp.zeros((bm,), slot_tok.dtype)])
    return pl.pallas_call(
        functools.partial(_expert_kernel, li=li, s_per=s_per),
        grid_spec=pltpu.PrefetchScalarGridSpec(
            num_scalar_prefetch=2,
            grid=(n // bm,),
            in_specs=[pl.BlockSpec(memory_space=pl.ANY)] * 4,
            out_specs=pl.BlockSpec((bm * s_per, LANES), lambda i, be, tok: (i, 0)),
            scratch_shapes=[
                pltpu.VMEM((2, bm * s_per, LANES), F32),
                pltpu.VMEM((d, dff), BF16), pltpu.VMEM((d, dff), BF16), pltpu.VMEM((dff, d), BF16),
                pltpu.VMEM((2, LANES, dff), w1.dtype), pltpu.VMEM((2, FF_TILE, d), w2.dtype),
                pltpu.SemaphoreType.DMA((2,)), pltpu.SemaphoreType.DMA((2,)),
            ],
        ),
        out_shape=jax.ShapeDtypeStruct((n * s_per, LANES), F32),
        compiler_params=_cparams(("arbitrary",), 56),
        name="moe_experts",
    )(block_exp, slot_tok, hn, w1, w3, w2)


def _combine_kernel(d0_ref, d1_ref, yb_hbm, x_ref, r_ref, g_ref, o_ref, r0_ref, r1_ref, sem, *, s_per, normalize):
    bm = o_ref.shape[0]
    i = pl.program_id(0)

    def issue(step, slot):
        base = step * bm

        def body(r, c):
            _row_copy(yb_hbm, r0_ref.at[slot], sem.at[slot], d0_ref[base + r], r, s_per).start()
            _row_copy(yb_hbm, r1_ref.at[slot], sem.at[slot], d1_ref[base + r], r, s_per).start()
            return c

        lax.fori_loop(0, bm, body, 0, unroll=8)

    def drain(slot):
        def body(r, c):
            _row_copy(yb_hbm, r0_ref.at[slot], sem.at[slot], 0, r, s_per).wait()
            _row_copy(yb_hbm, r1_ref.at[slot], sem.at[slot], 0, r, s_per).wait()
            return c

        lax.fori_loop(0, bm, body, 0, unroll=8)

    @pl.when(i == 0)
    def _():
        issue(0, 0)

    for slot in range(2):
        @pl.when(i % 2 == slot)
        def _():
            @pl.when(i + 1 < pl.num_programs(0))
            def _():
                issue(i + 1, 1 - slot)

            drain(slot)
            route = r_ref[...]
            out = (x_ref[...] + route[:, 2:3] * _from_row_tiles(r0_ref.at[slot], s_per)
                   + route[:, 3:4] * _from_row_tiles(r1_ref.at[slot], s_per))
            o_ref[...] = _rms(out, g_ref[...]) if normalize else out


def _combine(d0, d1, yb, x, route, g_final, normalize, bm):
    t, d = x.shape
    s_per = d // LANES
    return pl.pallas_call(
        functools.partial(_combine_kernel, s_per=s_per, normalize=normalize),
        grid_spec=pltpu.PrefetchScalarGridSpec(
            num_scalar_prefetch=2,
            grid=(t // bm,),
            in_specs=[
                pl.BlockSpec(memory_space=pl.ANY),
                pl.BlockSpec((bm, d), lambda i, a, b: (i, 0)),
                pl.BlockSpec((bm, LANES), lambda i, a, b: (i, 0)),
                pl.BlockSpec((1, d), lambda i, a, b: (0, 0)),
            ],
            out_specs=pl.BlockSpec((bm, d), lambda i, a, b: (i, 0)),
            scratch_shapes=[pltpu.VMEM((2, bm * s_per, LANES), F32), pltpu.VMEM((2, bm * s_per, LANES), F32),
                            pltpu.SemaphoreType.DMA((2,))],
        ),
        out_shape=jax.ShapeDtypeStruct((t, d), F32),
        compiler_params=_cparams(("arbitrary",), 32),
        name="moe_combine",
    )(d0, d1, yb, x, route, g_final)


def _dispatch_tables(idx, bm):
    t = idx.shape[0]
    flat_e = idx.reshape(-1)
    onehot = (flat_e[:, None] == jnp.arange(N_EXPERTS, dtype=I32)[None, :]).astype(I32)
    csum = jnp.cumsum(onehot, axis=0)
    rank = jnp.take_along_axis(csum, flat_e[:, None], axis=1)[:, 0] - 1
    counts = csum[-1]
    padded = (counts + bm - 1) // bm * bm
    pad_end = jnp.cumsum(padded)
    pad_start = pad_end - padded
    dest = (pad_start[flat_e] + rank).astype(I32)
    n_slots = t * TOP_K + N_EXPERTS * bm
    slot_tok = jnp.zeros((n_slots,), I32).at[dest].set(jnp.arange(t * TOP_K, dtype=I32) // TOP_K)
    block_start = jnp.arange(n_slots // bm, dtype=I32) * bm
    block_exp = jnp.minimum(jnp.searchsorted(pad_end, block_start, side='right'), N_EXPERTS - 1).astype(I32)
    block_exp = jnp.concatenate([block_exp, (pad_end[-1:] // bm).astype(I32)])
    return dest.reshape(t, TOP_K), slot_tok, block_exp


def _moe(x, g, w_router, w1, w3, w2, li, g_final, normalize, bm_tok, bm_slot):
    d = x.shape[1]
    wr_pad = jnp.zeros((d, LANES), F32).at[:, :N_EXPERTS].set(w_router)
    hn, route = _router(x, g, wr_pad, bm_tok)
    idx = route[:, :TOP_K].astype(I32)
    dest, slot_tok, block_exp = _dispatch_tables(idx, bm_slot)
    yb = _experts(block_exp, slot_tok, hn, w1, w3, w2, li, bm_slot)
    return _combine(dest[:, 0], dest[:, 1], yb, x, route, g_final, normalize, bm_tok)


def _norm_kernel(x_ref, g_ref, o_ref):
    o_ref[...] = _rms(x_ref[...], g_ref[...])


def _final_norm(x, g, bm):
    t, d = x.shape
    return pl.pallas_call(
        _norm_kernel,
        grid=(t // bm,),
        in_specs=[pl.BlockSpec((bm, d), lambda i: (i, 0)), pl.BlockSpec((1, d), lambda i: (0, 0))],
        out_specs=pl.BlockSpec((bm, d), lambda i: (i, 0)),
        out_shape=jax.ShapeDtypeStruct((t, d), F32),
        compiler_params=_cparams(("arbitrary",), 32),
        name="final_norm",
    )(x, g)


def _rope_tables(seq):
    half = ATTN_HEAD_DIM // 2
    inv_freq = 1.0 / (ROPE_THETA ** (jnp.arange(half, dtype=F32) / half))
    ang = jnp.arange(seq, dtype=F32)[:, None] * inv_freq[None, :]
    cos = jnp.tile(jnp.cos(ang), (1, 4))
    sin = jnp.sin(ang)
    return cos, jnp.concatenate([-sin, sin, -sin, sin], axis=1)


def _block(n, pref):
    return pref if n % pref == 0 else n


def kernel(x, norm_mix, w_in, attn_lambda, attn_subln, ssm_lam_re, ssm_lam_im, ssm_log_dt, ssm_b_re, ssm_b_im, ssm_c_re, ssm_c_im, ssm_d, w_glu, w_branch, w_out, norm_ffn, ffn_w1, ffn_w3, ffn_w2, moe_router, moe_w1, moe_w3, moe_w2, norm_final):
    bsz, seq, d = x.shape
    depth = w_in.shape[0]
    t = bsz * seq
    cos_t, sin_t = _rope_tables(seq)
    xf = x.reshape(t, d)
    bm = _block(seq, 1024)
    w_all = w_in.astype(BF16)
    a2, b2, c2, apow = jax.vmap(_s5_tables)(ssm_lam_re, ssm_lam_im, ssm_log_dt, ssm_b_re, ssm_b_im,
                                            ssm_c_re, ssm_c_im, ssm_d)
    wglu, wbr, wout = w_glu.astype(BF16), w_branch.astype(BF16), w_out.astype(BF16)
    fw1, fw3, fw2 = ffn_w1.astype(BF16), ffn_w3.astype(BF16), ffn_w2.astype(BF16)
    for layer in range(depth):
        proj, u = _inproj(xf, norm_mix[layer][None], w_all, layer, cos_t, sin_t, seq, _block(seq, 512))

        lam_init = jnp.full((1,), 0.8 - 0.6 * math.exp(-0.3 * layer), F32)
        a = _attention(proj, lam_init, attn_lambda[layer], attn_subln[layer][None], bsz, seq,
                       _block(seq, 1024), _block(seq // 2, 512))

        y = _s5(u, a2[layer], b2[layer], c2[layer], apow[layer], bsz, _block(seq // SSM_CHUNK, 512))

        xf = _merge(a, y, proj, xf, wglu, wbr, wout, layer, _block(seq, 512))

        i = layer // 2
        if layer % 2 == 0:
            xf = _ffn(xf, norm_ffn[layer][None], fw1, fw3, fw2, i, bm)
        else:
            last = layer == depth - 1
            xf = _moe(xf, norm_ffn[layer][None], moe_router[i], moe_w1, moe_w3, moe_w2, i,
                      norm_final[None], last, _block(seq, 256), 512)
    if depth % 2 == 1:
        xf = _final_norm(xf, norm_final[None], bm)
    return xf.reshape(bsz, seq, d)
```

```python
import functools
import math

import jax
import jax.numpy as jnp
from jax import lax
from jax.experimental import pallas as pl
from jax.experimental.pallas import tpu as pltpu

F32 = jnp.float32
BF16 = jnp.bfloat16
I32 = jnp.int32

RMS_EPS = 1e-6
ROPE_THETA = 10000.0
LOG2E = 1.4426950408889634

LANES = 128
SUBLANES = 8
VMEM_BYTES_V7X = 64 * 1024 * 1024

ATTN_HEADS = 4
ATTN_HEAD_DIM = 64
HEAD_SLAB = 2 * ATTN_HEAD_DIM
SSM_GROUP_CH = 16
SSM_STATE = 64
SSM_OCTET = LANES // SSM_GROUP_CH
SSM_CHUNK = 8
N_EXPERTS = 8
TOP_K = 2
SEG = 512


def _cparams(sem, vmem_mb):
    return pltpu.CompilerParams(dimension_semantics=sem, vmem_limit_bytes=vmem_mb * 1024 * 1024)


def _rms(x, g):
    return x * lax.rsqrt(jnp.mean(x * x, axis=-1, keepdims=True) + RMS_EPS) * g


def _sigmoid(x):
    return 0.5 * jnp.tanh(0.5 * x) + 0.5


def _inproj_kernel(x_ref, g_ref, w_ref, cos_ref, sin_ref, o_ref, u_ref, *, qscale):
    xn = _rms(x_ref[...], g_ref[...]).astype(BF16)
    cos = cos_ref[...]
    sin = sin_ref[...]

    half = ATTN_HEAD_DIM // 2
    first = (lax.broadcasted_iota(I32, cos.shape, 1) % ATTN_HEAD_DIM) < half

    def rot(a):
        parts = []
        for s in range(0, a.shape[1], HEAD_SLAB):
            blk = a[:, s:s + HEAD_SLAB]
            partner = jnp.where(first, pltpu.roll(blk, HEAD_SLAB - half, 1), pltpu.roll(blk, half, 1))
            parts.append(blk * cos + partner * sin)
        return jnp.concatenate(parts, axis=1)

    for j in range(w_ref.shape[1] // SEG):
        acc = jnp.dot(xn, w_ref[:, j * SEG:(j + 1) * SEG], preferred_element_type=F32)
        if j == 0:
            acc = rot(acc) * qscale
        elif j == 1:
            acc = rot(acc)
        elif j == 3:
            u_ref[...] = acc
        elif j >= 4:
            acc = _sigmoid(acc)
        o_ref[:, j * SEG:(j + 1) * SEG] = acc.astype(BF16)


def _inproj(x, g, w, layer, cos_t, sin_t, seq, bm):
    t, d = x.shape
    n = w.shape[2]
    qscale = ATTN_HEAD_DIM ** -0.5 * LOG2E
    nseq = seq // bm
    return pl.pallas_call(
        functools.partial(_inproj_kernel, qscale=qscale),
        grid=(t // bm,),
        in_specs=[
            pl.BlockSpec((bm, d), lambda i: (i, 0)),
            pl.BlockSpec((1, d), lambda i: (0, 0)),
            pl.BlockSpec((None, d, n), lambda i: (layer, 0, 0)),
            pl.BlockSpec((bm, HEAD_SLAB), lambda i: (i % nseq, 0)),
            pl.BlockSpec((bm, HEAD_SLAB), lambda i: (i % nseq, 0)),
        ],
        out_specs=[pl.BlockSpec((bm, n), lambda i: (i, 0)), pl.BlockSpec((bm, SEG), lambda i: (i, 0))],
        out_shape=[jax.ShapeDtypeStruct((t, n), BF16), jax.ShapeDtypeStruct((t, SEG), F32)],
        compiler_params=_cparams(("arbitrary",), 52),
        name="inproj",
    )(x, g, w, cos_t, sin_t)


ONES_ROWS = 16


def _attn_kernel(sc_ref, lp_ref, q_ref, k_ref, v_ref, g_ref, o_ref, vt_ref, acc_ref, m_ref,
                 s0_ref, s1_ref, p0_ref, p1_ref, x0_ref, x1_ref, *, bk):
    qi = pl.program_id(2)
    seq = k_ref.shape[0]
    bq = q_ref.shape[0]
    nkv = seq // bk
    assert nkv >= 2 and nkv % 2 == 0

    @pl.when(qi == 0)
    def _():
        for c in range(nkv):
            vt_ref[c, :HEAD_SLAB, :] = v_ref[c * bk:(c + 1) * bk, :].astype(F32).T.astype(BF16)
            vt_ref[c, HEAD_SLAB:, :] = jnp.ones((ONES_ROWS, bk), BF16)

    qt = q_ref[...].astype(F32).T
    row = lax.broadcasted_iota(I32, qt.shape, 0)
    half0 = row < ATTN_HEAD_DIM
    qst = jnp.concatenate([jnp.where(half0, qt, 0.0), jnp.where(half0, 0.0, qt)], axis=1).astype(BF16)

    acc_ref[...] = jnp.zeros_like(acc_ref)
    m_ref[...] = jnp.full_like(m_ref, -1e30)
    s_refs = (s0_ref, s1_ref)
    p_refs = (p0_ref, p1_ref)
    x_refs = (x0_ref, x1_ref)

    def qk(j, cur):
        off = pl.multiple_of(j * bk, bk)
        st = jnp.dot(k_ref[pl.ds(off, bk), :], qst, preferred_element_type=F32)
        s_refs[cur][...] = st
        x_refs[cur][...] = jnp.max(st, axis=0, keepdims=True)

    def pv(j, cur, alpha):
        acc_ref[...] = acc_ref[...] * alpha + jnp.dot(vt_ref[j], p_refs[cur][...],
                                                      preferred_element_type=F32)

    def softmax(cur):
        m_prev = m_ref[...]
        m_new = jnp.maximum(m_prev, x_refs[cur][...])
        p_refs[cur][...] = jnp.exp2((s_refs[cur][...] - m_new).astype(BF16))
        m_ref[...] = m_new
        return jnp.exp2(m_prev - m_new)

    def stage(j, cur, alpha):
        qk(j + 1, 1 - cur)
        pv(j - 1, 1 - cur, alpha)
        return softmax(cur)

    qk(0, 0)
    qk(1, 1)
    alpha = softmax(0)

    def pair(jj, alpha):
        j = 1 + 2 * jj
        return stage(j + 1, 0, stage(j, 1, alpha))

    alpha = lax.fori_loop(0, (nkv - 2) // 2, pair, alpha)
    pv(nkv - 2, 0, alpha)
    alpha = softmax(1)
    pv(nkv - 1, 1, alpha)

    lam_init = sc_ref[0]
    lp = lp_ref[...]
    lam = (jnp.exp(jnp.sum(lp[0:1] * lp[1:2], axis=-1, keepdims=True))
           - jnp.exp(jnp.sum(lp[2:3] * lp[3:4], axis=-1, keepdims=True)) + lam_init)
    acc = acc_ref[...]
    o = acc[:HEAD_SLAB] / acc[HEAD_SLAB:HEAD_SLAB + 1]
    a = o[:, :bq] - lam * o[:, bq:]
    a = a * lax.rsqrt(jnp.mean(a * a, axis=0, keepdims=True) + RMS_EPS) * (1.0 - lam_init)
    o_ref[...] = (a.T * g_ref[...]).astype(BF16)


def _attention(proj, lam_init, lam_p, subln, bsz, seq, bq, bk):
    t = proj.shape[0]
    nq = seq // bq
    koff = SEG // HEAD_SLAB
    return pl.pallas_call(
        functools.partial(_attn_kernel, bk=bk),
        grid=(bsz, ATTN_HEADS, nq),
        in_specs=[
            pl.BlockSpec(memory_space=pltpu.SMEM),
            pl.BlockSpec((4, ATTN_HEAD_DIM), lambda b, h, i: (0, 0)),
            pl.BlockSpec((bq, HEAD_SLAB), lambda b, h, i: (b * nq + i, h)),
            pl.BlockSpec((seq, HEAD_SLAB), lambda b, h, i: (b, koff + h)),
            pl.BlockSpec((seq, HEAD_SLAB), lambda b, h, i: (b, 2 * koff + h)),
            pl.BlockSpec((1, HEAD_SLAB), lambda b, h, i: (0, 0)),
        ],
        out_specs=pl.BlockSpec((bq, HEAD_SLAB), lambda b, h, i: (b * nq + i, h)),
        out_shape=jax.ShapeDtypeStruct((t, ATTN_HEADS * HEAD_SLAB), BF16),
        scratch_shapes=[
            pltpu.VMEM((seq // bk, HEAD_SLAB + ONES_ROWS, bk), BF16),
            pltpu.VMEM((HEAD_SLAB + ONES_ROWS, 2 * bq), F32),
            pltpu.VMEM((1, 2 * bq), F32),
            pltpu.VMEM((bk, 2 * bq), F32),
            pltpu.VMEM((bk, 2 * bq), F32),
            pltpu.VMEM((bk, 2 * bq), BF16),
            pltpu.VMEM((bk, 2 * bq), BF16),
            pltpu.VMEM((1, 2 * bq), F32),
            pltpu.VMEM((1, 2 * bq), F32),
        ],
        compiler_params=_cparams(("arbitrary", "arbitrary", "arbitrary"), 56),
        name="diff_attention",
    )(lam_init, lam_p, proj, proj, proj, subln)


SSM_TILE = SSM_OCTET * SSM_GROUP_CH
SSM_CW = SSM_CHUNK * SSM_TILE
SSM_SW = 4 * SSM_OCTET * SSM_STATE
SSM_QPER = SSM_SW // LANES
SCAN_SEGS = SUBLANES


def _s5_tables(lam_re, lam_im, log_dt, b_re, b_im, c_re, c_im, d_skip):
    hp = lax.Precision.HIGHEST
    ll = SSM_CHUNK
    g = lam_re.shape[1]
    no = g // SSM_OCTET
    lam = lax.complex(lam_re, lam_im)
    ldt = lam * jnp.exp(log_dt)[..., None]
    a = jnp.exp(ldt)
    bbar = ((a - 1.0) / lam)[..., None] * lax.complex(b_re, b_im)
    cmat = lax.complex(c_re, c_im)
    n = jnp.arange(ll + 1, dtype=F32)
    pw = jnp.exp(ldt[:, None] * n[None, :, None, None])

    def kern(d):
        return jnp.real(jnp.einsum('ghp,dgp,gpk->dghk', cmat[d], pw[d, :ll], bbar[d], precision=hp))

    kf, kb = kern(0), kern(1)
    tt = jnp.arange(ll)
    lag = tt[None, :] - tt[:, None]
    mf = jnp.where((lag >= 0)[:, :, None, None, None], kf[jnp.clip(lag, 0, ll - 1)], 0.0)
    mb = jnp.where((lag <= 0)[:, :, None, None, None], kb[jnp.clip(-lag, 0, ll - 1)], 0.0)
    skip = (jnp.eye(ll, dtype=F32)[:, :, None, None, None]
            * (d_skip.reshape(g, SSM_GROUP_CH)[None, None, :, :, None] * jnp.eye(SSM_GROUP_CH, dtype=F32)[None, None, None]))
    m = (mf + mb + skip).reshape(ll, ll, no, SSM_OCTET, SSM_GROUP_CH, SSM_GROUP_CH)
    a2 = m.transpose(2, 0, 3, 5, 1, 4).reshape(no, SSM_CW, ll * SSM_GROUP_CH)

    ein = jnp.stack([pw[0, ll - 1 - tt][..., None] * bbar[0][None],
                     pw[1, tt][..., None] * bbar[1][None]])
    ein = jnp.stack([jnp.real(ein), jnp.imag(ein)], axis=1)
    ein = ein.reshape(2, 2, ll, no, SSM_OCTET, SSM_STATE, SSM_GROUP_CH)
    b2 = ein.transpose(3, 2, 4, 6, 0, 1, 5).reshape(no, SSM_CW, 4 * SSM_STATE)

    eout = jnp.stack([cmat[0][None] * pw[0, tt + 1][:, :, None, :],
                      cmat[1][None] * pw[1, ll - tt][:, :, None, :]])
    eout = jnp.stack([jnp.real(eout), -jnp.imag(eout)], axis=1)
    eout = eout.reshape(2, 2, ll, no, SSM_OCTET, SSM_GROUP_CH, SSM_STATE)
    c2 = eout.transpose(3, 0, 1, 4, 6, 2, 5).reshape(no, SSM_SW, ll * SSM_GROUP_CH)

    al = pw[:, ll]
    apow = jnp.stack([jnp.real(al), jnp.imag(al)], axis=1)
    apow = apow.reshape(2, 2, no, SSM_OCTET * SSM_STATE).transpose(2, 0, 1, 3)
    return a2.astype(BF16), b2.astype(BF16), c2.astype(BF16), apow.astype(F32)


def _to_row_tiles(ref, x):
    n = x.shape[0]
    s_per = x.shape[1] // LANES
    for s in range(s_per):
        ref[pl.ds(s, n, stride=s_per), :] = x[:, s * LANES:(s + 1) * LANES]


def _from_row_tiles(ref, s_per):
    n = ref.shape[0] // s_per
    return jnp.concatenate([ref[pl.ds(s, n, stride=s_per), :] for s in range(s_per)], axis=1)


def _shr(idx, n):
    assert n & (n - 1) == 0
    return idx >> (n.bit_length() - 1)


def _group_of(idx, width):
    return _shr(idx, width) & (SSM_OCTET - 1)


def _spread(compact, rep_rows, rep_cols, inner, row_w, col_w):
    q = lax.broadcasted_iota(I32, (rep_rows, rep_cols), 0)
    c = lax.broadcasted_iota(I32, (rep_rows, rep_cols), 1)
    rep = ((_shr(q, inner) == _shr(c, inner * SSM_OCTET)) & ((q & (inner - 1)) == (c & (inner - 1)))).astype(BF16)
    full = jnp.dot(compact, rep, preferred_element_type=F32)
    r = lax.broadcasted_iota(I32, full.shape, 0)
    cc = lax.broadcasted_iota(I32, full.shape, 1)
    return jnp.where(_group_of(r, row_w) == _group_of(cc, col_w), full, 0.0).astype(BF16)


def _seg_rows(q, k, seglen):
    return pl.ds(q * SCAN_SEGS + k, seglen, stride=SSM_QPER * SCAN_SEGS)


def _s5_local_kernel(u_ref, a2_ref, b2_ref, yi_ref, st_ref, toe_ref, sin_ref, *, parts):
    ncb = yi_ref.shape[0]
    part = pl.program_id(2)
    segs = SCAN_SEGS // parts
    seglen = ncb // segs

    @pl.when((pl.program_id(1) == 0) & (part == 0))
    def _():
        toe_ref[...] = _spread(a2_ref[...], a2_ref.shape[1], SSM_CW, SSM_GROUP_CH, SSM_GROUP_CH, SSM_GROUP_CH)
        sin_ref[...] = _spread(b2_ref[...], b2_ref.shape[1], SSM_SW, SSM_STATE, SSM_GROUP_CH, SSM_STATE)

    u = jnp.concatenate([u_ref[pl.ds(j, ncb, stride=SSM_CHUNK), :] for j in range(SSM_CHUNK)],
                        axis=1).astype(BF16)
    yi_ref[...] = jnp.dot(u, toe_ref[...], preferred_element_type=F32)
    st = jnp.dot(u, sin_ref[...], preferred_element_type=F32)
    for pv in range(parts):
        @pl.when(part == pv)
        def _():
            for kk in range(segs):
                for q in range(SSM_QPER):
                    st_ref[_seg_rows(q, pv * segs + kk, seglen), :] = (
                        st[kk * seglen:(kk + 1) * seglen, q * LANES:(q + 1) * LANES])


def _cmul(ar, ai, xr, xi):
    return ar * xr - ai * xi, ar * xi + ai * xr


def _s5_scan_kernel(st_ref, ap_ref, x_ref):
    wt = SSM_QPER // 4
    seglen = st_ref.shape[0] // SSM_QPER
    assert seglen & (seglen - 1) == 0
    ap = ap_ref[...]

    def coef(d, r):
        return [jnp.broadcast_to(ap[d, r:r + 1, t * LANES:(t + 1) * LANES], (SCAN_SEGS, LANES)) for t in range(wt)]

    far, fai, bar, bai = coef(0, 0), coef(0, 1), coef(1, 0), coef(1, 1)

    def scan(init, store):
        def step(i, carry):
            fr, fi, br, bi = [list(c) for c in carry]
            rf = i * SSM_QPER
            rb = (seglen - 1 - i) * SSM_QPER
            for t in range(wt):
                if store:
                    x_ref[rf + t] = fr[t]
                    x_ref[rf + wt + t] = fi[t]
                    x_ref[rb + 2 * wt + t] = br[t]
                    x_ref[rb + 3 * wt + t] = bi[t]
                pr, pi = _cmul(far[t], fai[t], fr[t], fi[t])
                fr[t] = pr + st_ref[rf + t]
                fi[t] = pi + st_ref[rf + wt + t]
                pr, pi = _cmul(bar[t], bai[t], br[t], bi[t])
                br[t] = pr + st_ref[rb + 2 * wt + t]
                bi[t] = pi + st_ref[rb + 3 * wt + t]
            return tuple(fr), tuple(fi), tuple(br), tuple(bi)

        return lax.fori_loop(0, seglen, step, init)

    z = tuple(jnp.zeros((SCAN_SEGS, LANES), F32) for _ in range(wt))
    fr, fi, br, bi = scan((z, z, z, z), store=False)

    z1 = jnp.zeros((1, LANES), F32)
    fcr, fci, bcr, bci = [], [], [], []
    for t in range(wt):
        fsr, fsi, bsr, bsi = far[t][0:1], fai[t][0:1], bar[t][0:1], bai[t][0:1]
        for _ in range(seglen.bit_length() - 1):
            fsr, fsi = _cmul(fsr, fsi, fsr, fsi)
            bsr, bsi = _cmul(bsr, bsi, bsr, bsi)
        cr, ci = z1, z1
        rs, is_ = [], []
        for k in range(SCAN_SEGS):
            rs.append(cr)
            is_.append(ci)
            cr, ci = _cmul(fsr, fsi, cr, ci)
            cr, ci = cr + fr[t][k:k + 1], ci + fi[t][k:k + 1]
        fcr.append(jnp.concatenate(rs, axis=0))
        fci.append(jnp.concatenate(is_, axis=0))
        cr, ci = z1, z1
        rs, is_ = [None] * SCAN_SEGS, [None] * SCAN_SEGS
        for k in reversed(range(SCAN_SEGS)):
            rs[k] = cr
            is_[k] = ci
            cr, ci = _cmul(bsr, bsi, cr, ci)
            cr, ci = cr + br[t][k:k + 1], ci + bi[t][k:k + 1]
        bcr.append(jnp.concatenate(rs, axis=0))
        bci.append(jnp.concatenate(is_, axis=0))

    scan((tuple(fcr), tuple(fci), tuple(bcr), tuple(bci)), store=True)


def _s5_out_kernel(yi_ref, x_ref, c2_ref, y_ref, sout_ref, *, parts):
    ncb = yi_ref.shape[0]
    part = pl.program_id(2)
    segs = SCAN_SEGS // parts
    seglen = ncb // segs

    @pl.when((pl.program_id(1) == 0) & (part == 0))
    def _():
        sout_ref[...] = _spread(c2_ref[...], c2_ref.shape[1], SSM_CW, SSM_GROUP_CH, SSM_STATE, SSM_GROUP_CH)

    for pv in range(parts):
        @pl.when(part == pv)
        def _():
            x = jnp.concatenate(
                [jnp.concatenate([x_ref[_seg_rows(q, pv * segs + kk, seglen), :] for q in range(SSM_QPER)], axis=1)
                 for kk in range(segs)], axis=0).astype(BF16)
            y = yi_ref[...] + jnp.dot(x, sout_ref[...], preferred_element_type=F32)
            for t in range(SSM_CHUNK):
                y_ref[pl.ds(t, ncb, stride=SSM_CHUNK), :] = y[:, t * SSM_TILE:(t + 1) * SSM_TILE]


def _s5(u, a2, b2, c2, apow, bsz, ncb):
    t, width = u.shape
    no = width // SSM_TILE
    nc = t // SSM_CHUNK
    ncseq = nc // bsz
    parts = ncseq // ncb
    assert SCAN_SEGS % parts == 0
    rows = ncb * SSM_CHUNK
    slab = ncseq * SSM_QPER
    yi, st = pl.pallas_call(
        functools.partial(_s5_local_kernel, parts=parts),
        grid=(no, bsz, parts),
        in_specs=[
            pl.BlockSpec((rows, SSM_TILE), lambda o, b, p: (b * parts + p, o)),
            pl.BlockSpec((None,) + a2.shape[1:], lambda o, b, p: (o, 0, 0)),
            pl.BlockSpec((None,) + b2.shape[1:], lambda o, b, p: (o, 0, 0)),
        ],
        out_specs=[
            pl.BlockSpec((None, ncb, SSM_CW), lambda o, b, p: (o, b * parts + p, 0)),
            pl.BlockSpec((None, None, slab, LANES), lambda o, b, p: (o, b, 0, 0)),
        ],
        out_shape=[jax.ShapeDtypeStruct((no, nc, SSM_CW), F32),
                   jax.ShapeDtypeStruct((no, bsz, slab, LANES), F32)],
        scratch_shapes=[pltpu.VMEM((SSM_CW, SSM_CW), BF16), pltpu.VMEM((SSM_CW, SSM_SW), BF16)],
        compiler_params=_cparams(("arbitrary", "arbitrary", "arbitrary"), 52),
        name="s5_local",
    )(u, a2, b2)
    tiles = slab // SCAN_SEGS
    xin = pl.pallas_call(
        _s5_scan_kernel,
        grid=(no, bsz),
        in_specs=[
            pl.BlockSpec((None, None, tiles, SCAN_SEGS, LANES), lambda o, b: (o, b, 0, 0, 0)),
            pl.BlockSpec((None, 2, 2, SSM_SW // 4), lambda o, b: (o, 0, 0, 0)),
        ],
        out_specs=pl.BlockSpec((None, None, tiles, SCAN_SEGS, LANES), lambda o, b: (o, b, 0, 0, 0)),
        out_shape=jax.ShapeDtypeStruct((no, bsz, tiles, SCAN_SEGS, LANES), F32),
        compiler_params=_cparams(("arbitrary", "arbitrary"), 48),
        name="s5_scan",
    )(st.reshape(no, bsz, tiles, SCAN_SEGS, LANES), apow)
    return pl.pallas_call(
        functools.partial(_s5_out_kernel, parts=parts),
        grid=(no, bsz, parts),
        in_specs=[
            pl.BlockSpec((None, ncb, SSM_CW), lambda o, b, p: (o, b * parts + p, 0)),
            pl.BlockSpec((None, None, slab, LANES), lambda o, b, p: (o, b, 0, 0)),
            pl.BlockSpec((None,) + c2.shape[1:], lambda o, b, p: (o, 0, 0)),
        ],
        out_specs=pl.BlockSpec((rows, SSM_TILE), lambda o, b, p: (b * parts + p, o)),
        out_shape=jax.ShapeDtypeStruct((t, width), F32),
        scratch_shapes=[pltpu.VMEM((SSM_SW, SSM_CW), BF16)],
        compiler_params=_cparams(("arbitrary", "arbitrary", "arbitrary"), 52),
        name="s5_out",
    )(yi, xin.reshape(no, bsz, slab, LANES), c2)


def _merge_kernel(a_ref, y_ref, g0_ref, g1_ref, x_ref, wglu_ref, wb0_ref, wb1_ref, wout_ref, o_ref):
    y = y_ref[...]
    s = 0.5 * y * (1.0 + lax.erf(y * (2.0 ** -0.5)))
    glu = jnp.dot(s.astype(BF16), wglu_ref[...], preferred_element_type=F32)
    s = s * jax.nn.sigmoid(glu)
    pa = jnp.dot(a_ref[...], wb0_ref[...], preferred_element_type=F32)
    ps = jnp.dot(s.astype(BF16), wb1_ref[...], preferred_element_type=F32)
    merged = g0_ref[...].astype(F32) * pa + g1_ref[...].astype(F32) * ps
    o_ref[...] = x_ref[...] + jnp.dot(merged.astype(BF16), wout_ref[...], preferred_element_type=F32)


def _merge(a, y, proj, x, wglu, wbr, wout, layer, bm):
    t, d = x.shape
    aw = a.shape[1]
    g0 = 4 * SEG // d
    return pl.pallas_call(
        _merge_kernel,
        grid=(t // bm,),
        in_specs=[
            pl.BlockSpec((bm, aw), lambda i: (i, 0)),
            pl.BlockSpec((bm, aw), lambda i: (i, 0)),
            pl.BlockSpec((bm, d), lambda i: (i, g0)),
            pl.BlockSpec((bm, d), lambda i: (i, g0 + 1)),
            pl.BlockSpec((bm, d), lambda i: (i, 0)),
            pl.BlockSpec((None,) + wglu.shape[1:], lambda i: (layer, 0, 0)),
            pl.BlockSpec((None, None) + wbr.shape[2:], lambda i: (layer, 0, 0, 0)),
            pl.BlockSpec((None, None) + wbr.shape[2:], lambda i: (layer, 1, 0, 0)),
            pl.BlockSpec((None,) + wout.shape[1:], lambda i: (layer, 0, 0)),
        ],
        out_specs=pl.BlockSpec((bm, d), lambda i: (i, 0)),
        out_shape=jax.ShapeDtypeStruct((t, d), F32),
        compiler_params=_cparams(("arbitrary",), 48),
        name="merge",
    )(a, y, proj, proj, x, wglu, wbr, wbr, wout)


def _swiglu_partial(h, w1_ref, w3_ref, w2_ref, sub, side_work=None):
    ffc = w1_ref.shape[-1]
    acc = None
    for k, s in enumerate(range(0, ffc, sub)):
        e = min(s + sub, ffc)
        if side_work is not None:
            side_work(k, -(-ffc // sub))
        a = jnp.dot(h, w1_ref[:, s:e], preferred_element_type=F32)
        b = jnp.dot(h, w3_ref[:, s:e], preferred_element_type=F32)
        tt = (a * jax.nn.sigmoid(a) * b).astype(BF16)
        c = jnp.dot(tt, w2_ref[s:e, :], preferred_element_type=F32)
        acc = c if acc is None else acc + c
    return acc


def _ffn_kernel(x_ref, g_ref, w1_ref, w3_ref, w2_ref, o_ref, hn_ref, *, sub):
    f = pl.program_id(1)

    @pl.when(f == 0)
    def _():
        x = x_ref[...]
        hn_ref[...] = _rms(x, g_ref[...]).astype(BF16)
        o_ref[...] = x

    o_ref[...] += _swiglu_partial(hn_ref[...], w1_ref, w3_ref, w2_ref, sub)


def _ffn_tile(dff):
    return dff // 2 if (dff // 2) % LANES == 0 else dff


def _ffn(x, g, w1, w3, w2, li, bm):
    t, d = x.shape
    dff = w1.shape[2]
    ffc = _ffn_tile(dff)
    return pl.pallas_call(
        functools.partial(_ffn_kernel, sub=2 * LANES),
        grid=(t // bm, dff // ffc),
        in_specs=[
            pl.BlockSpec((bm, d), lambda i, f: (i, 0)),
            pl.BlockSpec((1, d), lambda i, f: (0, 0)),
            pl.BlockSpec((None, d, ffc), lambda i, f: (li, 0, f)),
            pl.BlockSpec((None, d, ffc), lambda i, f: (li, 0, f)),
            pl.BlockSpec((None, ffc, d), lambda i, f: (li, f, 0)),
        ],
        out_specs=pl.BlockSpec((bm, d), lambda i, f: (i, 0)),
        out_shape=jax.ShapeDtypeStruct((t, d), F32),
        scratch_shapes=[pltpu.VMEM((bm, d), BF16)],
        compiler_params=_cparams(("arbitrary", "arbitrary"), 56),
        name="ffn_dense",
    )(x, g, w1, w3, w2)


def _router_kernel(x_ref, g_ref, wr_ref, hn_ref, r_ref):
    hn = _rms(x_ref[...], g_ref[...])
    _to_row_tiles(hn_ref, hn)
    logits = jnp.dot(hn, wr_ref[...], preferred_element_type=F32, precision=lax.Precision.HIGHEST)
    lane = lax.broadcasted_iota(I32, logits.shape, 1)
    neg = jnp.float32(-1e30)
    logits = jnp.where(lane < N_EXPERTS, logits, neg)
    m1 = jnp.max(logits, axis=-1, keepdims=True)
    i1 = jnp.min(jnp.where(logits == m1, lane, LANES), axis=-1, keepdims=True)
    rest = jnp.where(lane == i1, neg, logits)
    m2 = jnp.max(rest, axis=-1, keepdims=True)
    i2 = jnp.min(jnp.where(rest == m2, lane, LANES), axis=-1, keepdims=True)
    e2 = jnp.exp(m2 - m1)
    g1 = 1.0 / (1.0 + e2)
    g2 = e2 / (1.0 + e2)
    r_ref[...] = jnp.where(lane == 0, i1.astype(F32),
                           jnp.where(lane == 1, i2.astype(F32),
                                     jnp.where(lane == 2, g1, jnp.where(lane == 3, g2, 0.0))))


def _router(x, g, wr_pad, bm):
    t, d = x.shape
    s_per = d // LANES
    return pl.pallas_call(
        _router_kernel,
        grid=(t // bm,),
        in_specs=[
            pl.BlockSpec((bm, d), lambda i: (i, 0)),
            pl.BlockSpec((1, d), lambda i: (0, 0)),
            pl.BlockSpec((d, LANES), lambda i: (0, 0)),
        ],
        out_specs=[pl.BlockSpec((bm * s_per, LANES), lambda i: (i, 0)), pl.BlockSpec((bm, LANES), lambda i: (i, 0))],
        out_shape=[jax.ShapeDtypeStruct((t * s_per, LANES), F32), jax.ShapeDtypeStruct((t, LANES), F32)],
        compiler_params=_cparams(("arbitrary",), 40),
        name="moe_router",
    )(x, g, wr_pad)


def _row_copy(src_hbm, dst_ref, sem, src_row, dst_row, s_per):
    src = pl.multiple_of(src_row * s_per, s_per)
    dst = pl.multiple_of(dst_row * s_per, s_per)
    return pltpu.make_async_copy(src_hbm.at[pl.ds(src, s_per)], dst_ref.at[pl.ds(dst, s_per)], sem)


FF_TILE = 2 * LANES
LOAD_DEPTH = 4


def _expert_kernel(be_ref, tok_ref, hn_hbm, w1_hbm, w3_hbm, w2_hbm, o_ref,
                   rows_ref, w1_ref, w3_ref, w2_ref, sta_ref, stb_ref, rsem, wsem, *, li, s_per):
    i = pl.program_id(0)
    bm = rows_ref.shape[1] // s_per
    dff = w1_ref.shape[1]
    nt = dff // FF_TILE

    def issue(step, slot):
        base = step * bm

        def body(r, c):
            _row_copy(hn_hbm, rows_ref.at[slot], rsem.at[slot], tok_ref[base + r], r, s_per).start()
            return c

        lax.fori_loop(0, bm, body, 0, unroll=8)

    def drain(slot):
        def body(r, c):
            _row_copy(hn_hbm, rows_ref.at[slot], rsem.at[slot], 0, r, s_per).wait()
            return c

        lax.fori_loop(0, bm, body, 0, unroll=8)

    @pl.when(i == 0)
    def _():
        issue(0, 0)

    nblk = pl.num_programs(0)
    active = i < be_ref[nblk]
    e = be_ref[i]
    changed = active & ((i == 0) | (e != be_ref[jnp.maximum(i - 1, 0)]))

    @pl.when(changed)
    def _():
        def stream(tiles, stage_ref):
            depth = stage_ref.shape[0]

            def copy(k):
                src, _ = tiles[k]
                return pltpu.make_async_copy(src, stage_ref.at[k % depth], wsem.at[k % depth])

            for k in range(min(depth, len(tiles))):
                copy(k).start()
            for k in range(len(tiles)):
                copy(k).wait()
                tiles[k][1](stage_ref[k % depth].astype(BF16))
                if k + depth < len(tiles):
                    copy(k + depth).start()

        def row_sink(ref, c, rows):
            def put(v):
                ref[c * rows:(c + 1) * rows, :] = v
            return put

        ra = sta_ref.shape[1]
        up = []
        for c in range(w1_ref.shape[0] // ra):
            up.append((w1_hbm.at[li, e, pl.ds(c * ra, ra), :], row_sink(w1_ref, c, ra)))
            up.append((w3_hbm.at[li, e, pl.ds(c * ra, ra), :], row_sink(w3_ref, c, ra)))
        stream(up, sta_ref)
        stream([(w2_hbm.at[li, e, pl.ds(c * FF_TILE, FF_TILE), :], row_sink(w2_ref, c, FF_TILE))
                for c in range(nt)], stb_ref)

    for slot in range(2):
        @pl.when((i % 2 == slot) & active)
        def _():
            drain(slot)
            xb = _from_row_tiles(rows_ref.at[slot], s_per).astype(BF16)

            def prefetch(k, n):
                per = -(-bm // n)
                for r in range(k * per, min(bm, (k + 1) * per)):
                    _row_copy(hn_hbm, rows_ref.at[1 - slot], rsem.at[1 - slot],
                              tok_ref[(i + 1) * bm + r], r, s_per).start()

            _to_row_tiles(o_ref, _swiglu_partial(xb, w1_ref, w3_ref, w2_ref, FF_TILE, prefetch))

        @pl.when((i % 2 == slot) & jnp.logical_not(active))
        def _():
            drain(slot)
            issue(i + 1, 1 - slot)
            o_ref[...] = jnp.zeros_like(o_ref)

        @pl.when((i % 2 == slot) & (i == nblk - 1))
        def _():
            drain(1 - slot)


def _experts(block_exp, slot_tok, hn, w1, w3, w2, li, bm):
    d = w1.shape[2]
    dff = w1.shape[3]
    s_per = d // LANES
    n = slot_tok.shape[0]
    assert dff % FF_TILE == 0
    slot_tok = jnp.concatenate([slot_tok, jnp.zeros((bm,), slot_tok.dtype)])
    return pl.pallas_call(
        functools.partial(_expert_kernel, li=li, s_per=s_per),
        grid_spec=pltpu.PrefetchScalarGridSpec(
            num_scalar_prefetch=2,
            grid=(n // bm,),
            in_specs=[pl.BlockSpec(memory_space=pl.ANY)] * 4,
            out_specs=pl.BlockSpec((bm * s_per, LANES), lambda i, be, tok: (i, 0)),
            scratch_shapes=[
                pltpu.VMEM((2, bm * s_per, LANES), F32),
                pltpu.VMEM((d, dff), BF16), pltpu.VMEM((d, dff), BF16), pltpu.VMEM((dff, d), BF16),
                pltpu.VMEM((LOAD_DEPTH, LANES, dff), w1.dtype), pltpu.VMEM((LOAD_DEPTH, FF_TILE, d), w2.dtype),
                pltpu.SemaphoreType.DMA((2,)), pltpu.SemaphoreType.DMA((LOAD_DEPTH,)),
            ],
        ),
        out_shape=jax.ShapeDtypeStruct((n * s_per, LANES), F32),
        compiler_params=_cparams(("arbitrary",), 56),
        name="moe_experts",
    )(block_exp, slot_tok, hn, w1, w3, w2)


def _combine_kernel(d0_ref, d1_ref, yb_hbm, x_ref, r_ref, g_ref, o_ref, r0_ref, r1_ref, sem, *, s_per, normalize):
    bm = o_ref.shape[0]
    i = pl.program_id(0)

    def issue(step, slot):
        base = step * bm

        def body(r, c):
            _row_copy(yb_hbm, r0_ref.at[slot], sem.at[slot], d0_ref[base + r], r, s_per).start()
            _row_copy(yb_hbm, r1_ref.at[slot], sem.at[slot], d1_ref[base + r], r, s_per).start()
            return c

        lax.fori_loop(0, bm, body, 0, unroll=8)

    def drain(slot):
        def body(r, c):
            _row_copy(yb_hbm, r0_ref.at[slot], sem.at[slot], 0, r, s_per).wait()
            _row_copy(yb_hbm, r1_ref.at[slot], sem.at[slot], 0, r, s_per).wait()
            return c

        lax.fori_loop(0, bm, body, 0, unroll=8)

    @pl.when(i == 0)
    def _():
        issue(0, 0)

    for slot in range(2):
        @pl.when(i % 2 == slot)
        def _():
            @pl.when(i + 1 < pl.num_programs(0))
            def _():
                issue(i + 1, 1 - slot)

            drain(slot)
            route = r_ref[...]
            out = (x_ref[...] + route[:, 2:3] * _from_row_tiles(r0_ref.at[slot], s_per)
                   + route[:, 3:4] * _from_row_tiles(r1_ref.at[slot], s_per))
            o_ref[...] = _rms(out, g_ref[...]) if normalize else out


def _combine(d0, d1, yb, x, route, g_final, normalize, bm):
    t, d = x.shape
    s_per = d // LANES
    return pl.pallas_call(
        functools.partial(_combine_kernel, s_per=s_per, normalize=normalize),
        grid_spec=pltpu.PrefetchScalarGridSpec(
            num_scalar_prefetch=2,
            grid=(t // bm,),
            in_specs=[
                pl.BlockSpec(memory_space=pl.ANY),
                pl.BlockSpec((bm, d), lambda i, a, b: (i, 0)),
                pl.BlockSpec((bm, LANES), lambda i, a, b: (i, 0)),
                pl.BlockSpec((1, d), lambda i, a, b: (0, 0)),
            ],
            out_specs=pl.BlockSpec((bm, d), lambda i, a, b: (i, 0)),
            scratch_shapes=[pltpu.VMEM((2, bm * s_per, LANES), F32), pltpu.VMEM((2, bm * s_per, LANES), F32),
                            pltpu.SemaphoreType.DMA((2,))],
        ),
        out_shape=jax.ShapeDtypeStruct((t, d), F32),
        compiler_params=_cparams(("arbitrary",), 32),
        name="moe_combine",
    )(d0, d1, yb, x, route, g_final)


def _dispatch_tables(idx, bm):
    t = idx.shape[0]
    flat_e = idx.reshape(-1)
    onehot = (flat_e[:, None] == jnp.arange(N_EXPERTS, dtype=I32)[None, :]).astype(I32)
    csum = jnp.cumsum(onehot, axis=0)
    rank = jnp.take_along_axis(csum, flat_e[:, None], axis=1)[:, 0] - 1
    counts = csum[-1]
    padded = (counts + bm - 1) // bm * bm
    pad_end = jnp.cumsum(padded)
    pad_start = pad_end - padded
    dest = (pad_start[flat_e] + rank).astype(I32)
    n_slots = t * TOP_K + N_EXPERTS * bm
    slot_tok = jnp.zeros((n_slots,), I32).at[dest].set(jnp.arange(t * TOP_K, dtype=I32) // TOP_K)
    block_start = jnp.arange(n_slots // bm, dtype=I32) * bm
    block_exp = jnp.minimum(jnp.searchsorted(pad_end, block_start, side='right'), N_EXPERTS - 1).astype(I32)
    block_exp = jnp.concatenate([block_exp, (pad_end[-1:] // bm).astype(I32)])
    return dest.reshape(t, TOP_K), slot_tok, block_exp


def _moe(x, g, w_router, w1, w3, w2, li, g_final, normalize, bm_tok, bm_slot):
    d = x.shape[1]
    wr_pad = jnp.zeros((d, LANES), F32).at[:, :N_EXPERTS].set(w_router)
    hn, route = _router(x, g, wr_pad, bm_tok)
    idx = route[:, :TOP_K].astype(I32)
    dest, slot_tok, block_exp = _dispatch_tables(idx, bm_slot)
    yb = _experts(block_exp, slot_tok, hn, w1, w3, w2, li, bm_slot)
    return _combine(dest[:, 0], dest[:, 1], yb, x, route, g_final, normalize, bm_tok)


def _norm_kernel(x_ref, g_ref, o_ref):
    o_ref[...] = _rms(x_ref[...], g_ref[...])


def _final_norm(x, g, bm):
    t, d = x.shape
    return pl.pallas_call(
        _norm_kernel,
        grid=(t // bm,),
        in_specs=[pl.BlockSpec((bm, d), lambda i: (i, 0)), pl.BlockSpec((1, d), lambda i: (0, 0))],
        out_specs=pl.BlockSpec((bm, d), lambda i: (i, 0)),
        out_shape=jax.ShapeDtypeStruct((t, d), F32),
        compiler_params=_cparams(("arbitrary",), 32),
        name="final_norm",
    )(x, g)


def _rope_tables(seq):
    half = ATTN_HEAD_DIM // 2
    inv_freq = 1.0 / (ROPE_THETA ** (jnp.arange(half, dtype=F32) / half))
    ang = jnp.arange(seq, dtype=F32)[:, None] * inv_freq[None, :]
    cos = jnp.tile(jnp.cos(ang), (1, 4))
    sin = jnp.sin(ang)
    return cos, jnp.concatenate([-sin, sin, -sin, sin], axis=1)


def _block(n, pref):
    return pref if n % pref == 0 else n


def kernel(x, norm_mix, w_in, attn_lambda, attn_subln, ssm_lam_re, ssm_lam_im, ssm_log_dt, ssm_b_re, ssm_b_im, ssm_c_re, ssm_c_im, ssm_d, w_glu, w_branch, w_out, norm_ffn, ffn_w1, ffn_w3, ffn_w2, moe_router, moe_w1, moe_w3, moe_w2, norm_final):
    bsz, seq, d = x.shape
    depth = w_in.shape[0]
    t = bsz * seq
    cos_t, sin_t = _rope_tables(seq)
    xf = x.reshape(t, d)
    bm = _block(seq, 1024)
    w_all = w_in.astype(BF16)
    a2, b2, c2, apow = jax.vmap(_s5_tables)(ssm_lam_re, ssm_lam_im, ssm_log_dt, ssm_b_re, ssm_b_im,
                                            ssm_c_re, ssm_c_im, ssm_d)
    wglu, wbr, wout = w_glu.astype(BF16), w_branch.astype(BF16), w_out.astype(BF16)
    fw1, fw3, fw2 = ffn_w1.astype(BF16), ffn_w3.astype(BF16), ffn_w2.astype(BF16)
    for layer in range(depth):
        proj, u = _inproj(xf, norm_mix[layer][None], w_all, layer, cos_t, sin_t, seq, _block(seq, 512))

        lam_init = jnp.full((1,), 0.8 - 0.6 * math.exp(-0.3 * layer), F32)
        a = _attention(proj, lam_init, attn_lambda[layer], attn_subln[layer][None], bsz, seq,
                       _block(seq, 1024), _block(seq // 2, 512))

        y = _s5(u, a2[layer], b2[layer], c2[layer], apow[layer], bsz, _block(seq // SSM_CHUNK, 512))

        xf = _merge(a, y, proj, xf, wglu, wbr, wout, layer, _block(seq, 512))

        i = layer // 2
        if layer % 2 == 0:
            xf = _ffn(xf, norm_ffn[layer][None], fw1, fw3, fw2, i, bm)
        else:
            last = layer == depth - 1
            xf = _moe(xf, norm_ffn[layer][None], moe_router[i], moe_w1, moe_w3, moe_w2, i,
                      norm_final[None], last, _block(seq, 256), 512)
    if depth % 2 == 1:
        xf = _final_norm(xf, norm_final[None], bm)
    return xf.reshape(bsz, seq, d)
```

```python
import functools
import math

import jax
import jax.numpy as jnp
from jax import lax
from jax.experimental import pallas as pl
from jax.experimental.pallas import tpu as pltpu

F32 = jnp.float32
BF16 = jnp.bfloat16
I32 = jnp.int32

RMS_EPS = 1e-6
ROPE_THETA = 10000.0
LOG2E = 1.4426950408889634

LANES = 128
SUBLANES = 8
VMEM_BYTES_V7X = 64 * 1024 * 1024

ATTN_HEADS = 4
ATTN_HEAD_DIM = 64
HEAD_SLAB = 2 * ATTN_HEAD_DIM
SSM_GROUP_CH = 16
SSM_STATE = 64
SSM_OCTET = LANES // SSM_GROUP_CH
SSM_CHUNK = 8
N_EXPERTS = 8
TOP_K = 2
SEG = 512


def _cparams(sem, vmem_mb):
    return pltpu.CompilerParams(dimension_semantics=sem, vmem_limit_bytes=vmem_mb * 1024 * 1024)


def _rms(x, g):
    return x * lax.rsqrt(jnp.mean(x * x, axis=-1, keepdims=True) + RMS_EPS) * g


def _sigmoid(x):
    return 0.5 * jnp.tanh(0.5 * x) + 0.5


def _inproj_kernel(x_ref, g_ref, w_ref, cos_ref, sin_ref, o_ref, u_ref, *, qscale):
    xn = _rms(x_ref[...], g_ref[...]).astype(BF16)
    cos = cos_ref[...]
    sin = sin_ref[...]

    half = ATTN_HEAD_DIM // 2
    first = (lax.broadcasted_iota(I32, cos.shape, 1) % ATTN_HEAD_DIM) < half

    def rot(a):
        parts = []
        for s in range(0, a.shape[1], HEAD_SLAB):
            blk = a[:, s:s + HEAD_SLAB]
            partner = jnp.where(first, pltpu.roll(blk, HEAD_SLAB - half, 1), pltpu.roll(blk, half, 1))
            parts.append(blk * cos + partner * sin)
        return jnp.concatenate(parts, axis=1)

    for j in range(w_ref.shape[1] // SEG):
        acc = jnp.dot(xn, w_ref[:, j * SEG:(j + 1) * SEG], preferred_element_type=F32)
        if j == 0:
            acc = rot(acc) * qscale
        elif j == 1:
            acc = rot(acc)
        elif j == 3:
            u_ref[...] = acc
        elif j >= 4:
            acc = _sigmoid(acc)
        o_ref[:, j * SEG:(j + 1) * SEG] = acc.astype(BF16)


def _inproj(x, g, w, layer, cos_t, sin_t, seq, bm):
    t, d = x.shape
    n = w.shape[2]
    qscale = ATTN_HEAD_DIM ** -0.5 * LOG2E
    nseq = seq // bm
    return pl.pallas_call(
        functools.partial(_inproj_kernel, qscale=qscale),
        grid=(t // bm,),
        in_specs=[
            pl.BlockSpec((bm, d), lambda i: (i, 0)),
            pl.BlockSpec((1, d), lambda i: (0, 0)),
            pl.BlockSpec((None, d, n), lambda i: (layer, 0, 0)),
            pl.BlockSpec((bm, HEAD_SLAB), lambda i: (i % nseq, 0)),
            pl.BlockSpec((bm, HEAD_SLAB), lambda i: (i % nseq, 0)),
        ],
        out_specs=[pl.BlockSpec((bm, n), lambda i: (i, 0)), pl.BlockSpec((bm, SEG), lambda i: (i, 0))],
        out_shape=[jax.ShapeDtypeStruct((t, n), BF16), jax.ShapeDtypeStruct((t, SEG), F32)],
        compiler_params=_cparams(("arbitrary",), 52),
        name="inproj",
    )(x, g, w, cos_t, sin_t)


ONES_ROWS = 16


def _attn_kernel(sc_ref, lp_ref, q_ref, k_ref, v_ref, g_ref, o_ref, vt_ref, acc_ref, m_ref,
                 s0_ref, s1_ref, p0_ref, p1_ref, x0_ref, x1_ref, *, bk):
    qi = pl.program_id(2)
    seq = k_ref.shape[0]
    bq = q_ref.shape[0]
    nkv = seq // bk
    assert nkv >= 2 and nkv % 2 == 0

    @pl.when(qi == 0)
    def _():
        for c in range(nkv):
            vt_ref[c, :HEAD_SLAB, :] = v_ref[c * bk:(c + 1) * bk, :].astype(F32).T.astype(BF16)
            vt_ref[c, HEAD_SLAB:, :] = jnp.ones((ONES_ROWS, bk), BF16)

    qt = q_ref[...].astype(F32).T
    row = lax.broadcasted_iota(I32, qt.shape, 0)
    half0 = row < ATTN_HEAD_DIM
    qst = jnp.concatenate([jnp.where(half0, qt, 0.0), jnp.where(half0, 0.0, qt)], axis=1).astype(BF16)

    acc_ref[...] = jnp.zeros_like(acc_ref)
    m_ref[...] = jnp.full_like(m_ref, -1e30)
    s_refs = (s0_ref, s1_ref)
    p_refs = (p0_ref, p1_ref)
    x_refs = (x0_ref, x1_ref)

    def qk(j, cur):
        off = pl.multiple_of(j * bk, bk)
        st = jnp.dot(k_ref[pl.ds(off, bk), :], qst, preferred_element_type=F32)
        s_refs[cur][...] = st
        x_refs[cur][...] = jnp.max(st, axis=0, keepdims=True)

    def pv(j, cur, alpha):
        acc_ref[...] = acc_ref[...] * alpha + jnp.dot(vt_ref[j], p_refs[cur][...],
                                                      preferred_element_type=F32)

    def softmax(cur):
        m_prev = m_ref[...]
        m_new = jnp.maximum(m_prev, x_refs[cur][...])
        p_refs[cur][...] = jnp.exp2((s_refs[cur][...] - m_new).astype(BF16))
        m_ref[...] = m_new
        return jnp.exp2(m_prev - m_new)

    def stage(j, cur, alpha):
        qk(j + 1, 1 - cur)
        pv(j - 1, 1 - cur, alpha)
        return softmax(cur)

    qk(0, 0)
    qk(1, 1)
    alpha = softmax(0)

    def pair(jj, alpha):
        j = 1 + 2 * jj
        return stage(j + 1, 0, stage(j, 1, alpha))

    alpha = lax.fori_loop(0, (nkv - 2) // 2, pair, alpha)
    pv(nkv - 2, 0, alpha)
    alpha = softmax(1)
    pv(nkv - 1, 1, alpha)

    lam_init = sc_ref[0]
    lp = lp_ref[...]
    lam = (jnp.exp(jnp.sum(lp[0:1] * lp[1:2], axis=-1, keepdims=True))
           - jnp.exp(jnp.sum(lp[2:3] * lp[3:4], axis=-1, keepdims=True)) + lam_init)
    acc = acc_ref[...]
    o = acc[:HEAD_SLAB] / acc[HEAD_SLAB:HEAD_SLAB + 1]
    a = o[:, :bq] - lam * o[:, bq:]
    a = a * lax.rsqrt(jnp.mean(a * a, axis=0, keepdims=True) + RMS_EPS) * (1.0 - lam_init)
    o_ref[...] = (a.T * g_ref[...]).astype(BF16)


def _attention(proj, lam_init, lam_p, subln, bsz, seq, bq, bk):
    t = proj.shape[0]
    nq = seq // bq
    koff = SEG // HEAD_SLAB
    return pl.pallas_call(
        functools.partial(_attn_kernel, bk=bk),
        grid=(bsz, ATTN_HEADS, nq),
        in_specs=[
            pl.BlockSpec(memory_space=pltpu.SMEM),
            pl.BlockSpec((4, ATTN_HEAD_DIM), lambda b, h, i: (0, 0)),
            pl.BlockSpec((bq, HEAD_SLAB), lambda b, h, i: (b * nq + i, h)),
            pl.BlockSpec((seq, HEAD_SLAB), lambda b, h, i: (b, koff + h)),
            pl.BlockSpec((seq, HEAD_SLAB), lambda b, h, i: (b, 2 * koff + h)),
            pl.BlockSpec((1, HEAD_SLAB), lambda b, h, i: (0, 0)),
        ],
        out_specs=pl.BlockSpec((bq, HEAD_SLAB), lambda b, h, i: (b * nq + i, h)),
        out_shape=jax.ShapeDtypeStruct((t, ATTN_HEADS * HEAD_SLAB), BF16),
        scratch_shapes=[
            pltpu.VMEM((seq // bk, HEAD_SLAB + ONES_ROWS, bk), BF16),
            pltpu.VMEM((HEAD_SLAB + ONES_ROWS, 2 * bq), F32),
            pltpu.VMEM((1, 2 * bq), F32),
            pltpu.VMEM((bk, 2 * bq), F32),
            pltpu.VMEM((bk, 2 * bq), F32),
            pltpu.VMEM((bk, 2 * bq), BF16),
            pltpu.VMEM((bk, 2 * bq), BF16),
            pltpu.VMEM((1, 2 * bq), F32),
            pltpu.VMEM((1, 2 * bq), F32),
        ],
        compiler_params=_cparams(("arbitrary", "arbitrary", "arbitrary"), 56),
        name="diff_attention",
    )(lam_init, lam_p, proj, proj, proj, subln)


SSM_TILE = SSM_OCTET * SSM_GROUP_CH
SSM_CW = SSM_CHUNK * SSM_TILE
SSM_SW = 4 * SSM_OCTET * SSM_STATE
SSM_QPER = SSM_SW // LANES
SCAN_SEGS = SUBLANES


def _s5_tables(lam_re, lam_im, log_dt, b_re, b_im, c_re, c_im, d_skip):
    hp = lax.Precision.HIGHEST
    ll = SSM_CHUNK
    g = lam_re.shape[1]
    no = g // SSM_OCTET
    lam = lax.complex(lam_re, lam_im)
    ldt = lam * jnp.exp(log_dt)[..., None]
    a = jnp.exp(ldt)
    bbar = ((a - 1.0) / lam)[..., None] * lax.complex(b_re, b_im)
    cmat = lax.complex(c_re, c_im)
    n = jnp.arange(ll + 1, dtype=F32)
    pw = jnp.exp(ldt[:, None] * n[None, :, None, None])

    def kern(d):
        return jnp.real(jnp.einsum('ghp,dgp,gpk->dghk', cmat[d], pw[d, :ll], bbar[d], precision=hp))

    kf, kb = kern(0), kern(1)
    tt = jnp.arange(ll)
    lag = tt[None, :] - tt[:, None]
    mf = jnp.where((lag >= 0)[:, :, None, None, None], kf[jnp.clip(lag, 0, ll - 1)], 0.0)
    mb = jnp.where((lag <= 0)[:, :, None, None, None], kb[jnp.clip(-lag, 0, ll - 1)], 0.0)
    skip = (jnp.eye(ll, dtype=F32)[:, :, None, None, None]
            * (d_skip.reshape(g, SSM_GROUP_CH)[None, None, :, :, None] * jnp.eye(SSM_GROUP_CH, dtype=F32)[None, None, None]))
    m = (mf + mb + skip).reshape(ll, ll, no, SSM_OCTET, SSM_GROUP_CH, SSM_GROUP_CH)
    a2 = m.transpose(2, 0, 3, 5, 1, 4).reshape(no, SSM_CW, ll * SSM_GROUP_CH)

    ein = jnp.stack([pw[0, ll - 1 - tt][..., None] * bbar[0][None],
                     pw[1, tt][..., None] * bbar[1][None]])
    ein = jnp.stack([jnp.real(ein), jnp.imag(ein)], axis=1)
    ein = ein.reshape(2, 2, ll, no, SSM_OCTET, SSM_STATE, SSM_GROUP_CH)
    b2 = ein.transpose(3, 2, 4, 6, 0, 1, 5).reshape(no, SSM_CW, 4 * SSM_STATE)

    eout = jnp.stack([cmat[0][None] * pw[0, tt + 1][:, :, None, :],
                      cmat[1][None] * pw[1, ll - tt][:, :, None, :]])
    eout = jnp.stack([jnp.real(eout), -jnp.imag(eout)], axis=1)
    eout = eout.reshape(2, 2, ll, no, SSM_OCTET, SSM_GROUP_CH, SSM_STATE)
    c2 = eout.transpose(3, 0, 1, 4, 6, 2, 5).reshape(no, SSM_SW, ll * SSM_GROUP_CH)

    al = pw[:, ll]
    apow = jnp.stack([jnp.real(al), jnp.imag(al)], axis=1)
    apow = apow.reshape(2, 2, no, SSM_OCTET * SSM_STATE).transpose(2, 0, 1, 3)
    return a2.astype(BF16), b2.astype(BF16), c2.astype(BF16), apow.astype(F32)


def _to_row_tiles(ref, x):
    n = x.shape[0]
    s_per = x.shape[1] // LANES
    for s in range(s_per):
        ref[pl.ds(s, n, stride=s_per), :] = x[:, s * LANES:(s + 1) * LANES]


def _from_row_tiles(ref, s_per):
    n = ref.shape[0] // s_per
    return jnp.concatenate([ref[pl.ds(s, n, stride=s_per), :] for s in range(s_per)], axis=1)


def _shr(idx, n):
    assert n & (n - 1) == 0
    return idx >> (n.bit_length() - 1)


def _group_of(idx, width):
    return _shr(idx, width) & (SSM_OCTET - 1)


def _spread(compact, rep_rows, rep_cols, inner, row_w, col_w):
    q = lax.broadcasted_iota(I32, (rep_rows, rep_cols), 0)
    c = lax.broadcasted_iota(I32, (rep_rows, rep_cols), 1)
    rep = ((_shr(q, inner) == _shr(c, inner * SSM_OCTET)) & ((q & (inner - 1)) == (c & (inner - 1)))).astype(BF16)
    full = jnp.dot(compact, rep, preferred_element_type=F32)
    r = lax.broadcasted_iota(I32, full.shape, 0)
    cc = lax.broadcasted_iota(I32, full.shape, 1)
    return jnp.where(_group_of(r, row_w) == _group_of(cc, col_w), full, 0.0).astype(BF16)


def _seg_rows(q, k, seglen):
    return pl.ds(q * SCAN_SEGS + k, seglen, stride=SSM_QPER * SCAN_SEGS)


def _s5_local_kernel(u_ref, a2_ref, b2_ref, yi_ref, st_ref, toe_ref, sin_ref, *, parts):
    ncb = yi_ref.shape[0]
    part = pl.program_id(2)
    segs = SCAN_SEGS // parts
    seglen = ncb // segs

    @pl.when((pl.program_id(1) == 0) & (part == 0))
    def _():
        toe_ref[...] = _spread(a2_ref[...], a2_ref.shape[1], SSM_CW, SSM_GROUP_CH, SSM_GROUP_CH, SSM_GROUP_CH)
        sin_ref[...] = _spread(b2_ref[...], b2_ref.shape[1], SSM_SW, SSM_STATE, SSM_GROUP_CH, SSM_STATE)

    u = jnp.concatenate([u_ref[pl.ds(j, ncb, stride=SSM_CHUNK), :] for j in range(SSM_CHUNK)],
                        axis=1).astype(BF16)
    yi_ref[...] = jnp.dot(u, toe_ref[...], preferred_element_type=F32)
    st = jnp.dot(u, sin_ref[...], preferred_element_type=F32)
    for pv in range(parts):
        @pl.when(part == pv)
        def _():
            for kk in range(segs):
                for q in range(SSM_QPER):
                    st_ref[_seg_rows(q, pv * segs + kk, seglen), :] = (
                        st[kk * seglen:(kk + 1) * seglen, q * LANES:(q + 1) * LANES])


def _cmul(ar, ai, xr, xi):
    return ar * xr - ai * xi, ar * xi + ai * xr


def _s5_scan_kernel(st_ref, ap_ref, x_ref):
    wt = SSM_QPER // 4
    seglen = st_ref.shape[0] // SSM_QPER
    assert seglen & (seglen - 1) == 0
    ap = ap_ref[...]

    def coef(d, r):
        return [jnp.broadcast_to(ap[d, r:r + 1, t * LANES:(t + 1) * LANES], (SCAN_SEGS, LANES)) for t in range(wt)]

    far, fai, bar, bai = coef(0, 0), coef(0, 1), coef(1, 0), coef(1, 1)

    def scan(init, store):
        def step(i, carry):
            fr, fi, br, bi = [list(c) for c in carry]
            rf = i * SSM_QPER
            rb = (seglen - 1 - i) * SSM_QPER
            for t in range(wt):
                if store:
                    x_ref[rf + t] = fr[t]
                    x_ref[rf + wt + t] = fi[t]
                    x_ref[rb + 2 * wt + t] = br[t]
                    x_ref[rb + 3 * wt + t] = bi[t]
                pr, pi = _cmul(far[t], fai[t], fr[t], fi[t])
                fr[t] = pr + st_ref[rf + t]
                fi[t] = pi + st_ref[rf + wt + t]
                pr, pi = _cmul(bar[t], bai[t], br[t], bi[t])
                br[t] = pr + st_ref[rb + 2 * wt + t]
                bi[t] = pi + st_ref[rb + 3 * wt + t]
            return tuple(fr), tuple(fi), tuple(br), tuple(bi)

        return lax.fori_loop(0, seglen, step, init)

    z = tuple(jnp.zeros((SCAN_SEGS, LANES), F32) for _ in range(wt))
    fr, fi, br, bi = scan((z, z, z, z), store=False)

    z1 = jnp.zeros((1, LANES), F32)
    fcr, fci, bcr, bci = [], [], [], []
    for t in range(wt):
        fsr, fsi, bsr, bsi = far[t][0:1], fai[t][0:1], bar[t][0:1], bai[t][0:1]
        for _ in range(seglen.bit_length() - 1):
            fsr, fsi = _cmul(fsr, fsi, fsr, fsi)
            bsr, bsi = _cmul(bsr, bsi, bsr, bsi)
        cr, ci = z1, z1
        rs, is_ = [], []
        for k in range(SCAN_SEGS):
            rs.append(cr)
            is_.append(ci)
            cr, ci = _cmul(fsr, fsi, cr, ci)
            cr, ci = cr + fr[t][k:k + 1], ci + fi[t][k:k + 1]
        fcr.append(jnp.concatenate(rs, axis=0))
        fci.append(jnp.concatenate(is_, axis=0))
        cr, ci = z1, z1
        rs, is_ = [None] * SCAN_SEGS, [None] * SCAN_SEGS
        for k in reversed(range(SCAN_SEGS)):
            rs[k] = cr
            is_[k] = ci
            cr, ci = _cmul(bsr, bsi, cr, ci)
            cr, ci = cr + br[t][k:k + 1], ci + bi[t][k:k + 1]
        bcr.append(jnp.concatenate(rs, axis=0))
        bci.append(jnp.concatenate(is_, axis=0))

    scan((tuple(fcr), tuple(fci), tuple(bcr), tuple(bci)), store=True)


def _s5_out_kernel(yi_ref, x_ref, c2_ref, y_ref, sout_ref, *, parts):
    ncb = yi_ref.shape[0]
    part = pl.program_id(2)
    segs = SCAN_SEGS // parts
    seglen = ncb // segs

    @pl.when((pl.program_id(1) == 0) & (part == 0))
    def _():
        sout_ref[...] = _spread(c2_ref[...], c2_ref.shape[1], SSM_CW, SSM_GROUP_CH, SSM_STATE, SSM_GROUP_CH)

    for pv in range(parts):
        @pl.when(part == pv)
        def _():
            x = jnp.concatenate(
                [jnp.concatenate([x_ref[_seg_rows(q, pv * segs + kk, seglen), :] for q in range(SSM_QPER)], axis=1)
                 for kk in range(segs)], axis=0).astype(BF16)
            y = yi_ref[...] + jnp.dot(x, sout_ref[...], preferred_element_type=F32)
            for t in range(SSM_CHUNK):
                y_ref[pl.ds(t, ncb, stride=SSM_CHUNK), :] = y[:, t * SSM_TILE:(t + 1) * SSM_TILE]


def _s5(u, a2, b2, c2, apow, bsz, ncb):
    t, width = u.shape
    no = width // SSM_TILE
    nc = t // SSM_CHUNK
    ncseq = nc // bsz
    parts = ncseq // ncb
    assert SCAN_SEGS % parts == 0
    rows = ncb * SSM_CHUNK
    slab = ncseq * SSM_QPER
    yi, st = pl.pallas_call(
        functools.partial(_s5_local_kernel, parts=parts),
        grid=(no, bsz, parts),
        in_specs=[
            pl.BlockSpec((rows, SSM_TILE), lambda o, b, p: (b * parts + p, o)),
            pl.BlockSpec((None,) + a2.shape[1:], lambda o, b, p: (o, 0, 0)),
            pl.BlockSpec((None,) + b2.shape[1:], lambda o, b, p: (o, 0, 0)),
        ],
        out_specs=[
            pl.BlockSpec((None, ncb, SSM_CW), lambda o, b, p: (o, b * parts + p, 0)),
            pl.BlockSpec((None, None, slab, LANES), lambda o, b, p: (o, b, 0, 0)),
        ],
        out_shape=[jax.ShapeDtypeStruct((no, nc, SSM_CW), F32),
                   jax.ShapeDtypeStruct((no, bsz, slab, LANES), F32)],
        scratch_shapes=[pltpu.VMEM((SSM_CW, SSM_CW), BF16), pltpu.VMEM((SSM_CW, SSM_SW), BF16)],
        compiler_params=_cparams(("arbitrary", "arbitrary", "arbitrary"), 52),
        name="s5_local",
    )(u, a2, b2)
    tiles = slab // SCAN_SEGS
    xin = pl.pallas_call(
        _s5_scan_kernel,
        grid=(no, bsz),
        in_specs=[
            pl.BlockSpec((None, None, tiles, SCAN_SEGS, LANES), lambda o, b: (o, b, 0, 0, 0)),
            pl.BlockSpec((None, 2, 2, SSM_SW // 4), lambda o, b: (o, 0, 0, 0)),
        ],
        out_specs=pl.BlockSpec((None, None, tiles, SCAN_SEGS, LANES), lambda o, b: (o, b, 0, 0, 0)),
        out_shape=jax.ShapeDtypeStruct((no, bsz, tiles, SCAN_SEGS, LANES), F32),
        compiler_params=_cparams(("arbitrary", "arbitrary"), 48),
        name="s5_scan",
    )(st.reshape(no, bsz, tiles, SCAN_SEGS, LANES), apow)
    return pl.pallas_call(
        functools.partial(_s5_out_kernel, parts=parts),
        grid=(no, bsz, parts),
        in_specs=[
            pl.BlockSpec((None, ncb, SSM_CW), lambda o, b, p: (o, b * parts + p, 0)),
            pl.BlockSpec((None, None, slab, LANES), lambda o, b, p: (o, b, 0, 0)),
            pl.BlockSpec((None,) + c2.shape[1:], lambda o, b, p: (o, 0, 0)),
        ],
        out_specs=pl.BlockSpec((rows, SSM_TILE), lambda o, b, p: (b * parts + p, o)),
        out_shape=jax.ShapeDtypeStruct((t, width), F32),
        scratch_shapes=[pltpu.VMEM((SSM_SW, SSM_CW), BF16)],
        compiler_params=_cparams(("arbitrary", "arbitrary", "arbitrary"), 52),
        name="s5_out",
    )(yi, xin.reshape(no, bsz, slab, LANES), c2)


def _merge_kernel(a_ref, y_ref, g0_ref, g1_ref, x_ref, wglu_ref, wb0_ref, wb1_ref, wout_ref, o_ref):
    y = y_ref[...]
    s = 0.5 * y * (1.0 + lax.erf(y * (2.0 ** -0.5)))
    glu = jnp.dot(s.astype(BF16), wglu_ref[...], preferred_element_type=F32)
    s = s * jax.nn.sigmoid(glu)
    pa = jnp.dot(a_ref[...], wb0_ref[...], preferred_element_type=F32)
    ps = jnp.dot(s.astype(BF16), wb1_ref[...], preferred_element_type=F32)
    merged = g0_ref[...].astype(F32) * pa + g1_ref[...].astype(F32) * ps
    o_ref[...] = x_ref[...] + jnp.dot(merged.astype(BF16), wout_ref[...], preferred_element_type=F32)


def _merge(a, y, proj, x, wglu, wbr, wout, layer, bm):
    t, d = x.shape
    aw = a.shape[1]
    g0 = 4 * SEG // d
    return pl.pallas_call(
        _merge_kernel,
        grid=(t // bm,),
        in_specs=[
            pl.BlockSpec((bm, aw), lambda i: (i, 0)),
            pl.BlockSpec((bm, aw), lambda i: (i, 0)),
            pl.BlockSpec((bm, d), lambda i: (i, g0)),
            pl.BlockSpec((bm, d), lambda i: (i, g0 + 1)),
            pl.BlockSpec((bm, d), lambda i: (i, 0)),
            pl.BlockSpec((None,) + wglu.shape[1:], lambda i: (layer, 0, 0)),
            pl.BlockSpec((None, None) + wbr.shape[2:], lambda i: (layer, 0, 0, 0)),
            pl.BlockSpec((None, None) + wbr.shape[2:], lambda i: (layer, 1, 0, 0)),
            pl.BlockSpec((None,) + wout.shape[1:], lambda i: (layer, 0, 0)),
        ],
        out_specs=pl.BlockSpec((bm, d), lambda i: (i, 0)),
        out_shape=jax.ShapeDtypeStruct((t, d), F32),
        compiler_params=_cparams(("arbitrary",), 48),
        name="merge",
    )(a, y, proj, proj, x, wglu, wbr, wbr, wout)


def _swiglu_partial(h, w1_ref, w3_ref, w2_ref, sub, side_work=None):
    ffc = w1_ref.shape[-1]
    acc = None
    for k, s in enumerate(range(0, ffc, sub)):
        e = min(s + sub, ffc)
        if side_work is not None:
            side_work(k, -(-ffc // sub))
        a = jnp.dot(h, w1_ref[:, s:e], preferred_element_type=F32)
        b = jnp.dot(h, w3_ref[:, s:e], preferred_element_type=F32)
        tt = (a * jax.nn.sigmoid(a) * b).astype(BF16)
        c = jnp.dot(tt, w2_ref[s:e, :], preferred_element_type=F32)
        acc = c if acc is None else acc + c
    return acc


FF_TILE = 2 * LANES
LOAD_DEPTH = 4


def _load_weights_bf16(w1_src, w3_src, w2_src, w1_ref, w3_ref, w2_ref, sta_ref, stb_ref, wsem):
    def stream(tiles, stage_ref):
        depth = stage_ref.shape[0]

        def copy(k):
            src, _ = tiles[k]
            return pltpu.make_async_copy(src, stage_ref.at[k % depth], wsem.at[k % depth])

        for k in range(min(depth, len(tiles))):
            copy(k).start()
        for k in range(len(tiles)):
            copy(k).wait()
            tiles[k][1](stage_ref[k % depth].astype(BF16))
            if k + depth < len(tiles):
                copy(k + depth).start()

    def row_sink(ref, c, rows):
        def put(v):
            ref[c * rows:(c + 1) * rows, :] = v
        return put

    ra, rb = sta_ref.shape[1], stb_ref.shape[1]
    up = []
    for c in range(w1_ref.shape[0] // ra):
        up.append((w1_src.at[pl.ds(c * ra, ra), :], row_sink(w1_ref, c, ra)))
        up.append((w3_src.at[pl.ds(c * ra, ra), :], row_sink(w3_ref, c, ra)))
    stream(up, sta_ref)
    stream([(w2_src.at[pl.ds(c * rb, rb), :], row_sink(w2_ref, c, rb)) for c in range(w2_ref.shape[0] // rb)],
           stb_ref)


def _ffn_weight_scratch(d, dff, dtype):
    return [pltpu.VMEM((d, dff), BF16), pltpu.VMEM((d, dff), BF16), pltpu.VMEM((dff, d), BF16),
            pltpu.VMEM((LOAD_DEPTH, LANES, dff), dtype), pltpu.VMEM((LOAD_DEPTH, FF_TILE, d), dtype),
            pltpu.SemaphoreType.DMA((LOAD_DEPTH,))]


def _ffn_kernel(x_ref, g_ref, w1_hbm, w3_hbm, w2_hbm, o_ref, w1_ref, w3_ref, w2_ref, sta_ref, stb_ref, wsem, *, li):
    @pl.when(pl.program_id(0) == 0)
    def _():
        _load_weights_bf16(w1_hbm.at[li], w3_hbm.at[li], w2_hbm.at[li], w1_ref, w3_ref, w2_ref,
                           sta_ref, stb_ref, wsem)

    x = x_ref[...]
    hn = _rms(x, g_ref[...]).astype(BF16)
    o_ref[...] = x + _swiglu_partial(hn, w1_ref, w3_ref, w2_ref, FF_TILE)


def _ffn(x, g, w1, w3, w2, li, bm):
    t, d = x.shape
    dff = w1.shape[2]
    assert dff % FF_TILE == 0
    return pl.pallas_call(
        functools.partial(_ffn_kernel, li=li),
        grid=(t // bm,),
        in_specs=[
            pl.BlockSpec((bm, d), lambda i: (i, 0)),
            pl.BlockSpec((1, d), lambda i: (0, 0)),
            pl.BlockSpec(memory_space=pl.ANY),
            pl.BlockSpec(memory_space=pl.ANY),
            pl.BlockSpec(memory_space=pl.ANY),
        ],
        out_specs=pl.BlockSpec((bm, d), lambda i: (i, 0)),
        out_shape=jax.ShapeDtypeStruct((t, d), F32),
        scratch_shapes=_ffn_weight_scratch(d, dff, w1.dtype),
        compiler_params=_cparams(("arbitrary",), 56),
        name="ffn_dense",
    )(x, g, w1, w3, w2)


def _router_kernel(x_ref, g_ref, wr_ref, hn_ref, r_ref):
    hn = _rms(x_ref[...], g_ref[...])
    _to_row_tiles(hn_ref, hn)
    logits = jnp.dot(hn, wr_ref[...], preferred_element_type=F32, precision=lax.Precision.HIGHEST)
    lane = lax.broadcasted_iota(I32, logits.shape, 1)
    neg = jnp.float32(-1e30)
    logits = jnp.where(lane < N_EXPERTS, logits, neg)
    m1 = jnp.max(logits, axis=-1, keepdims=True)
    i1 = jnp.min(jnp.where(logits == m1, lane, LANES), axis=-1, keepdims=True)
    rest = jnp.where(lane == i1, neg, logits)
    m2 = jnp.max(rest, axis=-1, keepdims=True)
    i2 = jnp.min(jnp.where(rest == m2, lane, LANES), axis=-1, keepdims=True)
    e2 = jnp.exp(m2 - m1)
    g1 = 1.0 / (1.0 + e2)
    g2 = e2 / (1.0 + e2)
    r_ref[...] = jnp.where(lane == 0, i1.astype(F32),
                           jnp.where(lane == 1, i2.astype(F32),
                                     jnp.where(lane == 2, g1, jnp.where(lane == 3, g2, 0.0))))


def _router(x, g, wr_pad, bm):
    t, d = x.shape
    s_per = d // LANES
    return pl.pallas_call(
        _router_kernel,
        grid=(t // bm,),
        in_specs=[
            pl.BlockSpec((bm, d), lambda i: (i, 0)),
            pl.BlockSpec((1, d), lambda i: (0, 0)),
            pl.BlockSpec((d, LANES), lambda i: (0, 0)),
        ],
        out_specs=[pl.BlockSpec((bm * s_per, LANES), lambda i: (i, 0)), pl.BlockSpec((bm, LANES), lambda i: (i, 0))],
        out_shape=[jax.ShapeDtypeStruct((t * s_per, LANES), F32), jax.ShapeDtypeStruct((t, LANES), F32)],
        compiler_params=_cparams(("arbitrary",), 40),
        name="moe_router",
    )(x, g, wr_pad)


def _row_copy(src_hbm, dst_ref, sem, src_row, dst_row, s_per):
    src = pl.multiple_of(src_row * s_per, s_per)
    dst = pl.multiple_of(dst_row * s_per, s_per)
    return pltpu.make_async_copy(src_hbm.at[pl.ds(src, s_per)], dst_ref.at[pl.ds(dst, s_per)], sem)


def _expert_kernel(be_ref, tok_ref, hn_hbm, w1_hbm, w3_hbm, w2_hbm, o_ref,
                   rows_ref, rsem, w1_ref, w3_ref, w2_ref, sta_ref, stb_ref, wsem, *, li, s_per):
    i = pl.program_id(0)
    bm = rows_ref.shape[1] // s_per

    def issue(step, slot):
        base = step * bm

        def body(r, c):
            _row_copy(hn_hbm, rows_ref.at[slot], rsem.at[slot], tok_ref[base + r], r, s_per).start()
            return c

        lax.fori_loop(0, bm, body, 0, unroll=8)

    def drain(slot):
        def body(r, c):
            _row_copy(hn_hbm, rows_ref.at[slot], rsem.at[slot], 0, r, s_per).wait()
            return c

        lax.fori_loop(0, bm, body, 0, unroll=8)

    @pl.when(i == 0)
    def _():
        issue(0, 0)

    nblk = pl.num_programs(0)
    active = i < be_ref[nblk]
    e = be_ref[i]
    changed = active & ((i == 0) | (e != be_ref[jnp.maximum(i - 1, 0)]))

    @pl.when(changed)
    def _():
        _load_weights_bf16(w1_hbm.at[li, e], w3_hbm.at[li, e], w2_hbm.at[li, e], w1_ref, w3_ref, w2_ref,
                           sta_ref, stb_ref, wsem)

    for slot in range(2):
        @pl.when((i % 2 == slot) & active)
        def _():
            drain(slot)
            xb = _from_row_tiles(rows_ref.at[slot], s_per).astype(BF16)

            def prefetch(k, n):
                per = -(-bm // n)
                for r in range(k * per, min(bm, (k + 1) * per)):
                    _row_copy(hn_hbm, rows_ref.at[1 - slot], rsem.at[1 - slot],
                              tok_ref[(i + 1) * bm + r], r, s_per).start()

            _to_row_tiles(o_ref, _swiglu_partial(xb, w1_ref, w3_ref, w2_ref, FF_TILE, prefetch))

        @pl.when((i % 2 == slot) & jnp.logical_not(active))
        def _():
            drain(slot)
            issue(i + 1, 1 - slot)
            o_ref[...] = jnp.zeros_like(o_ref)

        @pl.when((i % 2 == slot) & (i == nblk - 1))
        def _():
            drain(1 - slot)


def _experts(block_exp, slot_tok, hn, w1, w3, w2, li, bm):
    d = w1.shape[2]
    dff = w1.shape[3]
    s_per = d // LANES
    n = slot_tok.shape[0]
    assert dff % FF_TILE == 0
    slot_tok = jnp.concatenate([slot_tok, jnp.zeros((bm,), slot_tok.dtype)])
    return pl.pallas_call(
        functools.partial(_expert_kernel, li=li, s_per=s_per),
        grid_spec=pltpu.PrefetchScalarGridSpec(
            num_scalar_prefetch=2,
            grid=(n // bm,),
            in_specs=[pl.BlockSpec(memory_space=pl.ANY)] * 4,
            out_specs=pl.BlockSpec((bm * s_per, LANES), lambda i, be, tok: (i, 0)),
            scratch_shapes=[pltpu.VMEM((2, bm * s_per, LANES), F32), pltpu.SemaphoreType.DMA((2,))]
            + _ffn_weight_scratch(d, dff, w1.dtype),
        ),
        out_shape=jax.ShapeDtypeStruct((n * s_per, LANES), F32),
        compiler_params=_cparams(("arbitrary",), 56),
        name="moe_experts",
    )(block_exp, slot_tok, hn, w1, w3, w2)


def _combine_kernel(d0_ref, d1_ref, yb_hbm, x_ref, r_ref, g_ref, o_ref, r0_ref, r1_ref, sem, *, s_per, normalize):
    bm = o_ref.shape[0]
    i = pl.program_id(0)

    def issue(step, slot):
        base = step * bm

        def body(r, c):
            _row_copy(yb_hbm, r0_ref.at[slot], sem.at[slot], d0_ref[base + r], r, s_per).start()
            _row_copy(yb_hbm, r1_ref.at[slot], sem.at[slot], d1_ref[base + r], r, s_per).start()
            return c

        lax.fori_loop(0, bm, body, 0, unroll=8)

    def drain(slot):
        def body(r, c):
            _row_copy(yb_hbm, r0_ref.at[slot], sem.at[slot], 0, r, s_per).wait()
            _row_copy(yb_hbm, r1_ref.at[slot], sem.at[slot], 0, r, s_per).wait()
            return c

        lax.fori_loop(0, bm, body, 0, unroll=8)

    @pl.when(i == 0)
    def _():
        issue(0, 0)

    for slot in range(2):
        @pl.when(i % 2 == slot)
        def _():
            @pl.when(i + 1 < pl.num_programs(0))
            def _():
                issue(i + 1, 1 - slot)

            drain(slot)
            route = r_ref[...]
            out = (x_ref[...] + route[:, 2:3] * _from_row_tiles(r0_ref.at[slot], s_per)
                   + route[:, 3:4] * _from_row_tiles(r1_ref.at[slot], s_per))
            o_ref[...] = _rms(out, g_ref[...]) if normalize else out


def _combine(d0, d1, yb, x, route, g_final, normalize, bm):
    t, d = x.shape
    s_per = d // LANES
    return pl.pallas_call(
        functools.partial(_combine_kernel, s_per=s_per, normalize=normalize),
        grid_spec=pltpu.PrefetchScalarGridSpec(
            num_scalar_prefetch=2,
            grid=(t // bm,),
            in_specs=[
                pl.BlockSpec(memory_space=pl.ANY),
                pl.BlockSpec((bm, d), lambda i, a, b: (i, 0)),
                pl.BlockSpec((bm, LANES), lambda i, a, b: (i, 0)),
                pl.BlockSpec((1, d), lambda i, a, b: (0, 0)),
            ],
            out_specs=pl.BlockSpec((bm, d), lambda i, a, b: (i, 0)),
            scratch_shapes=[pltpu.VMEM((2, bm * s_per, LANES), F32), pltpu.VMEM((2, bm * s_per, LANES), F32),
                            pltpu.SemaphoreType.DMA((2,))],
        ),
        out_shape=jax.ShapeDtypeStruct((t, d), F32),
        compiler_params=_cparams(("arbitrary",), 32),
        name="moe_combine",
    )(d0, d1, yb, x, route, g_final)


def _dispatch_tables(idx, bm):
    t = idx.shape[0]
    flat_e = idx.reshape(-1)
    onehot = (flat_e[:, None] == jnp.arange(N_EXPERTS, dtype=I32)[None, :]).astype(I32)
    csum = jnp.cumsum(onehot, axis=0)
    rank = jnp.take_along_axis(csum, flat_e[:, None], axis=1)[:, 0] - 1
    counts = csum[-1]
    padded = (counts + bm - 1) // bm * bm
    pad_end = jnp.cumsum(padded)
    pad_start = pad_end - padded
    dest = (pad_start[flat_e] + rank).astype(I32)
    n_slots = t * TOP_K + N_EXPERTS * bm
    slot_tok = jnp.zeros((n_slots,), I32).at[dest].set(jnp.arange(t * TOP_K, dtype=I32) // TOP_K)
    block_start = jnp.arange(n_slots // bm, dtype=I32) * bm
    block_exp = jnp.minimum(jnp.searchsorted(pad_end, block_start, side='right'), N_EXPERTS - 1).astype(I32)
    block_exp = jnp.concatenate([block_exp, (pad_end[-1:] // bm).astype(I32)])
    return dest.reshape(t, TOP_K), slot_tok, block_exp


def _moe(x, g, w_router, w1, w3, w2, li, g_final, normalize, bm_tok, bm_slot):
    d = x.shape[1]
    wr_pad = jnp.zeros((d, LANES), F32).at[:, :N_EXPERTS].set(w_router)
    hn, route = _router(x, g, wr_pad, bm_tok)
    idx = route[:, :TOP_K].astype(I32)
    dest, slot_tok, block_exp = _dispatch_tables(idx, bm_slot)
    yb = _experts(block_exp, slot_tok, hn, w1, w3, w2, li, bm_slot)
    return _combine(dest[:, 0], dest[:, 1], yb, x, route, g_final, normalize, bm_tok)


def _norm_kernel(x_ref, g_ref, o_ref):
    o_ref[...] = _rms(x_ref[...], g_ref[...])


def _final_norm(x, g, bm):
    t, d = x.shape
    return pl.pallas_call(
        _norm_kernel,
        grid=(t // bm,),
        in_specs=[pl.BlockSpec((bm, d), lambda i: (i, 0)), pl.BlockSpec((1, d), lambda i: (0, 0))],
        out_specs=pl.BlockSpec((bm, d), lambda i: (i, 0)),
        out_shape=jax.ShapeDtypeStruct((t, d), F32),
        compiler_params=_cparams(("arbitrary",), 32),
        name="final_norm",
    )(x, g)


def _rope_tables(seq):
    half = ATTN_HEAD_DIM // 2
    inv_freq = 1.0 / (ROPE_THETA ** (jnp.arange(half, dtype=F32) / half))
    ang = jnp.arange(seq, dtype=F32)[:, None] * inv_freq[None, :]
    cos = jnp.tile(jnp.cos(ang), (1, 4))
    sin = jnp.sin(ang)
    return cos, jnp.concatenate([-sin, sin, -sin, sin], axis=1)


def _block(n, pref):
    return pref if n % pref == 0 else n


def kernel(x, norm_mix, w_in, attn_lambda, attn_subln, ssm_lam_re, ssm_lam_im, ssm_log_dt, ssm_b_re, ssm_b_im, ssm_c_re, ssm_c_im, ssm_d, w_glu, w_branch, w_out, norm_ffn, ffn_w1, ffn_w3, ffn_w2, moe_router, moe_w1, moe_w3, moe_w2, norm_final):
    bsz, seq, d = x.shape
    depth = w_in.shape[0]
    t = bsz * seq
    cos_t, sin_t = _rope_tables(seq)
    xf = x.reshape(t, d)
    bm = _block(seq, 1024)
    w_all = w_in.astype(BF16)
    a2, b2, c2, apow = jax.vmap(_s5_tables)(ssm_lam_re, ssm_lam_im, ssm_log_dt, ssm_b_re, ssm_b_im,
                                            ssm_c_re, ssm_c_im, ssm_d)
    wglu, wbr, wout = w_glu.astype(BF16), w_branch.astype(BF16), w_out.astype(BF16)
    for layer in range(depth):
        proj, u = _inproj(xf, norm_mix[layer][None], w_all, layer, cos_t, sin_t, seq, _block(seq, 512))

        lam_init = jnp.full((1,), 0.8 - 0.6 * math.exp(-0.3 * layer), F32)
        a = _attention(proj, lam_init, attn_lambda[layer], attn_subln[layer][None], bsz, seq,
                       _block(seq, 1024), _block(seq // 2, 512))

        y = _s5(u, a2[layer], b2[layer], c2[layer], apow[layer], bsz, _block(seq // SSM_CHUNK, 512))

        xf = _merge(a, y, proj, xf, wglu, wbr, wout, layer, _block(seq, 512))

        i = layer // 2
        if layer % 2 == 0:
            xf = _ffn(xf, norm_ffn[layer][None], ffn_w1, ffn_w3, ffn_w2, i, _block(seq, 512))
        else:
            last = layer == depth - 1
            xf = _moe(xf, norm_ffn[layer][None], moe_router[i], moe_w1, moe_w3, moe_w2, i,
                      norm_final[None], last, _block(seq, 256), 512)
    if depth % 2 == 1:
        xf = _final_norm(xf, norm_final[None], bm)
    return xf.reshape(bsz, seq, d)
```

```python
import functools
import math

import jax
import jax.numpy as jnp
from jax import lax
from jax.experimental import pallas as pl
from jax.experimental.pallas import tpu as pltpu

F32 = jnp.float32
BF16 = jnp.bfloat16
I32 = jnp.int32

RMS_EPS = 1e-6
ROPE_THETA = 10000.0
LOG2E = 1.4426950408889634

LANES = 128
SUBLANES = 8
VMEM_BYTES_V7X = 64 * 1024 * 1024

ATTN_HEADS = 4
ATTN_HEAD_DIM = 64
HEAD_SLAB = 2 * ATTN_HEAD_DIM
SSM_GROUP_CH = 16
SSM_STATE = 64
SSM_OCTET = LANES // SSM_GROUP_CH
SSM_CHUNK = 8
N_EXPERTS = 8
TOP_K = 2
SEG = 512


def _cparams(sem, vmem_mb):
    return pltpu.CompilerParams(dimension_semantics=sem, vmem_limit_bytes=vmem_mb * 1024 * 1024)


def _rms(x, g):
    return x * lax.rsqrt(jnp.mean(x * x, axis=-1, keepdims=True) + RMS_EPS) * g


def _sigmoid(x):
    return 0.5 * jnp.tanh(0.5 * x) + 0.5


def _inproj_kernel(x_ref, g_ref, w_ref, cos_ref, sin_ref, o_ref, u_ref, *, qscale):
    xn = _rms(x_ref[...], g_ref[...]).astype(BF16)
    cos = cos_ref[...]
    sin = sin_ref[...]

    half = ATTN_HEAD_DIM // 2
    first = (lax.broadcasted_iota(I32, cos.shape, 1) % ATTN_HEAD_DIM) < half

    def rot(a):
        parts = []
        for s in range(0, a.shape[1], HEAD_SLAB):
            blk = a[:, s:s + HEAD_SLAB]
            partner = jnp.where(first, pltpu.roll(blk, HEAD_SLAB - half, 1), pltpu.roll(blk, half, 1))
            parts.append(blk * cos + partner * sin)
        return jnp.concatenate(parts, axis=1)

    for j in range(w_ref.shape[1] // SEG):
        acc = jnp.dot(xn, w_ref[:, j * SEG:(j + 1) * SEG], preferred_element_type=F32)
        if j == 0:
            acc = rot(acc) * qscale
        elif j == 1:
            acc = rot(acc)
        elif j == 3:
            u_ref[...] = acc
        elif j >= 4:
            acc = _sigmoid(acc)
        o_ref[:, j * SEG:(j + 1) * SEG] = acc.astype(BF16)


def _inproj(x, g, w, layer, cos_t, sin_t, seq, bm):
    t, d = x.shape
    n = w.shape[2]
    qscale = ATTN_HEAD_DIM ** -0.5 * LOG2E
    nseq = seq // bm
    return pl.pallas_call(
        functools.partial(_inproj_kernel, qscale=qscale),
        grid=(t // bm,),
        in_specs=[
            pl.BlockSpec((bm, d), lambda i: (i, 0)),
            pl.BlockSpec((1, d), lambda i: (0, 0)),
            pl.BlockSpec((None, d, n), lambda i: (layer, 0, 0)),
            pl.BlockSpec((bm, HEAD_SLAB), lambda i: (i % nseq, 0)),
            pl.BlockSpec((bm, HEAD_SLAB), lambda i: (i % nseq, 0)),
        ],
        out_specs=[pl.BlockSpec((bm, n), lambda i: (i, 0)), pl.BlockSpec((bm, SEG), lambda i: (i, 0))],
        out_shape=[jax.ShapeDtypeStruct((t, n), BF16), jax.ShapeDtypeStruct((t, SEG), F32)],
        compiler_params=_cparams(("arbitrary",), 52),
        name="inproj",
    )(x, g, w, cos_t, sin_t)


ONES_ROWS = 16


def _attn_kernel(sc_ref, lp_ref, q_ref, k_ref, v_ref, g_ref, o_ref, vt_ref, acc_ref, m_ref,
                 s0_ref, s1_ref, p0_ref, p1_ref, x0_ref, x1_ref, *, bk):
    qi = pl.program_id(2)
    seq = k_ref.shape[0]
    bq = q_ref.shape[0]
    nkv = seq // bk
    assert nkv >= 2 and nkv % 2 == 0

    @pl.when(qi == 0)
    def _():
        for c in range(nkv):
            vt_ref[c, :HEAD_SLAB, :] = v_ref[c * bk:(c + 1) * bk, :].astype(F32).T.astype(BF16)
            vt_ref[c, HEAD_SLAB:, :] = jnp.ones((ONES_ROWS, bk), BF16)

    qt = q_ref[...].astype(F32).T
    row = lax.broadcasted_iota(I32, qt.shape, 0)
    half0 = row < ATTN_HEAD_DIM
    qst = jnp.concatenate([jnp.where(half0, qt, 0.0), jnp.where(half0, 0.0, qt)], axis=1).astype(BF16)

    acc_ref[...] = jnp.zeros_like(acc_ref)
    m_ref[...] = jnp.full_like(m_ref, -1e30)
    s_refs = (s0_ref, s1_ref)
    p_refs = (p0_ref, p1_ref)
    x_refs = (x0_ref, x1_ref)

    def qk(j, cur):
        off = pl.multiple_of(j * bk, bk)
        st = jnp.dot(k_ref[pl.ds(off, bk), :], qst, preferred_element_type=F32)
        s_refs[cur][...] = st
        x_refs[cur][...] = jnp.max(st, axis=0, keepdims=True)

    def pv(j, cur, alpha):
        acc_ref[...] = acc_ref[...] * alpha + jnp.dot(vt_ref[j], p_refs[cur][...],
                                                      preferred_element_type=F32)

    def softmax(cur):
        m_prev = m_ref[...]
        m_new = jnp.maximum(m_prev, x_refs[cur][...])
        p_refs[cur][...] = jnp.exp2((s_refs[cur][...] - m_new).astype(BF16))
        m_ref[...] = m_new
        return jnp.exp2(m_prev - m_new)

    def stage(j, cur, alpha):
        qk(j + 1, 1 - cur)
        pv(j - 1, 1 - cur, alpha)
        return softmax(cur)

    qk(0, 0)
    qk(1, 1)
    alpha = softmax(0)

    def pair(jj, alpha):
        j = 1 + 2 * jj
        return stage(j + 1, 0, stage(j, 1, alpha))

    alpha = lax.fori_loop(0, (nkv - 2) // 2, pair, alpha)
    pv(nkv - 2, 0, alpha)
    alpha = softmax(1)
    pv(nkv - 1, 1, alpha)

    lam_init = sc_ref[0]
    lp = lp_ref[...]
    lam = (jnp.exp(jnp.sum(lp[0:1] * lp[1:2], axis=-1, keepdims=True))
           - jnp.exp(jnp.sum(lp[2:3] * lp[3:4], axis=-1, keepdims=True)) + lam_init)
    acc = acc_ref[...]
    o = acc[:HEAD_SLAB] / acc[HEAD_SLAB:HEAD_SLAB + 1]
    a = o[:, :bq] - lam * o[:, bq:]
    a = a * lax.rsqrt(jnp.mean(a * a, axis=0, keepdims=True) + RMS_EPS) * (1.0 - lam_init)
    o_ref[...] = (a.T * g_ref[...]).astype(BF16)


def _attention(proj, lam_init, lam_p, subln, bsz, seq, bq, bk):
    t = proj.shape[0]
    nq = seq // bq
    koff = SEG // HEAD_SLAB
    return pl.pallas_call(
        functools.partial(_attn_kernel, bk=bk),
        grid=(bsz, ATTN_HEADS, nq),
        in_specs=[
            pl.BlockSpec(memory_space=pltpu.SMEM),
            pl.BlockSpec((4, ATTN_HEAD_DIM), lambda b, h, i: (0, 0)),
            pl.BlockSpec((bq, HEAD_SLAB), lambda b, h, i: (b * nq + i, h)),
            pl.BlockSpec((seq, HEAD_SLAB), lambda b, h, i: (b, koff + h)),
            pl.BlockSpec((seq, HEAD_SLAB), lambda b, h, i: (b, 2 * koff + h)),
            pl.BlockSpec((1, HEAD_SLAB), lambda b, h, i: (0, 0)),
        ],
        out_specs=pl.BlockSpec((bq, HEAD_SLAB), lambda b, h, i: (b * nq + i, h)),
        out_shape=jax.ShapeDtypeStruct((t, ATTN_HEADS * HEAD_SLAB), BF16),
        scratch_shapes=[
            pltpu.VMEM((seq // bk, HEAD_SLAB + ONES_ROWS, bk), BF16),
            pltpu.VMEM((HEAD_SLAB + ONES_ROWS, 2 * bq), F32),
            pltpu.VMEM((1, 2 * bq), F32),
            pltpu.VMEM((bk, 2 * bq), F32),
            pltpu.VMEM((bk, 2 * bq), F32),
            pltpu.VMEM((bk, 2 * bq), BF16),
            pltpu.VMEM((bk, 2 * bq), BF16),
            pltpu.VMEM((1, 2 * bq), F32),
            pltpu.VMEM((1, 2 * bq), F32),
        ],
        compiler_params=_cparams(("arbitrary", "arbitrary", "arbitrary"), 56),
        name="diff_attention",
    )(lam_init, lam_p, proj, proj, proj, subln)


SSM_TILE = SSM_OCTET * SSM_GROUP_CH
SSM_CW = SSM_CHUNK * SSM_TILE
SSM_SW = 4 * SSM_OCTET * SSM_STATE
SSM_QPER = SSM_SW // LANES
SCAN_SEGS = SUBLANES


def _s5_tables(lam_re, lam_im, log_dt, b_re, b_im, c_re, c_im, d_skip):
    hp = lax.Precision.HIGHEST
    ll = SSM_CHUNK
    g = lam_re.shape[1]
    no = g // SSM_OCTET
    lam = lax.complex(lam_re, lam_im)
    ldt = lam * jnp.exp(log_dt)[..., None]
    a = jnp.exp(ldt)
    bbar = ((a - 1.0) / lam)[..., None] * lax.complex(b_re, b_im)
    cmat = lax.complex(c_re, c_im)
    n = jnp.arange(ll + 1, dtype=F32)
    pw = jnp.exp(ldt[:, None] * n[None, :, None, None])

    def kern(d):
        return jnp.real(jnp.einsum('ghp,dgp,gpk->dghk', cmat[d], pw[d, :ll], bbar[d], precision=hp))

    kf, kb = kern(0), kern(1)
    tt = jnp.arange(ll)
    lag = tt[None, :] - tt[:, None]
    mf = jnp.where((lag >= 0)[:, :, None, None, None], kf[jnp.clip(lag, 0, ll - 1)], 0.0)
    mb = jnp.where((lag <= 0)[:, :, None, None, None], kb[jnp.clip(-lag, 0, ll - 1)], 0.0)
    skip = (jnp.eye(ll, dtype=F32)[:, :, None, None, None]
            * (d_skip.reshape(g, SSM_GROUP_CH)[None, None, :, :, None] * jnp.eye(SSM_GROUP_CH, dtype=F32)[None, None, None]))
    m = (mf + mb + skip).reshape(ll, ll, no, SSM_OCTET, SSM_GROUP_CH, SSM_GROUP_CH)
    a2 = m.transpose(2, 0, 3, 5, 1, 4).reshape(no, SSM_CW, ll * SSM_GROUP_CH)

    ein = jnp.stack([pw[0, ll - 1 - tt][..., None] * bbar[0][None],
                     pw[1, tt][..., None] * bbar[1][None]])
    ein = jnp.stack([jnp.real(ein), jnp.imag(ein)], axis=1)
    ein = ein.reshape(2, 2, ll, no, SSM_OCTET, SSM_STATE, SSM_GROUP_CH)
    b2 = ein.transpose(3, 2, 4, 6, 0, 1, 5).reshape(no, SSM_CW, 4 * SSM_STATE)

    eout = jnp.stack([cmat[0][None] * pw[0, tt + 1][:, :, None, :],
                      cmat[1][None] * pw[1, ll - tt][:, :, None, :]])
    eout = jnp.stack([jnp.real(eout), -jnp.imag(eout)], axis=1)
    eout = eout.reshape(2, 2, ll, no, SSM_OCTET, SSM_GROUP_CH, SSM_STATE)
    c2 = eout.transpose(3, 0, 1, 4, 6, 2, 5).reshape(no, SSM_SW, ll * SSM_GROUP_CH)

    al = pw[:, ll]
    apow = jnp.stack([jnp.real(al), jnp.imag(al)], axis=1)
    apow = apow.reshape(2, 2, no, SSM_OCTET * SSM_STATE).transpose(2, 0, 1, 3)
    return a2.astype(BF16), b2.astype(BF16), c2.astype(BF16), apow.astype(F32)


def _to_row_tiles(ref, x):
    n = x.shape[0]
    s_per = x.shape[1] // LANES
    for s in range(s_per):
        ref[pl.ds(s, n, stride=s_per), :] = x[:, s * LANES:(s + 1) * LANES]


def _from_row_tiles(ref, s_per):
    n = ref.shape[0] // s_per
    return jnp.concatenate([ref[pl.ds(s, n, stride=s_per), :] for s in range(s_per)], axis=1)


def _shr(idx, n):
    assert n & (n - 1) == 0
    return idx >> (n.bit_length() - 1)


def _group_of(idx, width):
    return _shr(idx, width) & (SSM_OCTET - 1)


def _spread(compact, rep_rows, rep_cols, inner, row_w, col_w):
    q = lax.broadcasted_iota(I32, (rep_rows, rep_cols), 0)
    c = lax.broadcasted_iota(I32, (rep_rows, rep_cols), 1)
    rep = ((_shr(q, inner) == _shr(c, inner * SSM_OCTET)) & ((q & (inner - 1)) == (c & (inner - 1)))).astype(BF16)
    full = jnp.dot(compact, rep, preferred_element_type=F32)
    r = lax.broadcasted_iota(I32, full.shape, 0)
    cc = lax.broadcasted_iota(I32, full.shape, 1)
    return jnp.where(_group_of(r, row_w) == _group_of(cc, col_w), full, 0.0).astype(BF16)


def _seg_rows(q, k, seglen):
    return pl.ds(q * SCAN_SEGS + k, seglen, stride=SSM_QPER * SCAN_SEGS)


def _s5_local_kernel(u_ref, a2_ref, b2_ref, yi_ref, st_ref, toe_ref, sin_ref, *, parts):
    ncb = yi_ref.shape[0]
    part = pl.program_id(2)
    segs = SCAN_SEGS // parts
    seglen = ncb // segs

    @pl.when((pl.program_id(1) == 0) & (part == 0))
    def _():
        toe_ref[...] = _spread(a2_ref[...], a2_ref.shape[1], SSM_CW, SSM_GROUP_CH, SSM_GROUP_CH, SSM_GROUP_CH)
        sin_ref[...] = _spread(b2_ref[...], b2_ref.shape[1], SSM_SW, SSM_STATE, SSM_GROUP_CH, SSM_STATE)

    u = jnp.concatenate([u_ref[pl.ds(j, ncb, stride=SSM_CHUNK), :] for j in range(SSM_CHUNK)],
                        axis=1).astype(BF16)
    yi_ref[...] = jnp.dot(u, toe_ref[...], preferred_element_type=F32)
    st = jnp.dot(u, sin_ref[...], preferred_element_type=F32)
    for pv in range(parts):
        @pl.when(part == pv)
        def _():
            for kk in range(segs):
                for q in range(SSM_QPER):
                    st_ref[_seg_rows(q, pv * segs + kk, seglen), :] = (
                        st[kk * seglen:(kk + 1) * seglen, q * LANES:(q + 1) * LANES])


def _cmul(ar, ai, xr, xi):
    return ar * xr - ai * xi, ar * xi + ai * xr


def _s5_scan_kernel(st_ref, ap_ref, x_ref):
    wt = SSM_QPER // 4
    seglen = st_ref.shape[0] // SSM_QPER
    assert seglen & (seglen - 1) == 0
    ap = ap_ref[...]

    def coef(d, r):
        return [jnp.broadcast_to(ap[d, r:r + 1, t * LANES:(t + 1) * LANES], (SCAN_SEGS, LANES)) for t in range(wt)]

    far, fai, bar, bai = coef(0, 0), coef(0, 1), coef(1, 0), coef(1, 1)

    def scan(init, store):
        def step(i, carry):
            fr, fi, br, bi = [list(c) for c in carry]
            rf = i * SSM_QPER
            rb = (seglen - 1 - i) * SSM_QPER
            for t in range(wt):
                if store:
                    x_ref[rf + t] = fr[t]
                    x_ref[rf + wt + t] = fi[t]
                    x_ref[rb + 2 * wt + t] = br[t]
                    x_ref[rb + 3 * wt + t] = bi[t]
                pr, pi = _cmul(far[t], fai[t], fr[t], fi[t])
                fr[t] = pr + st_ref[rf + t]
                fi[t] = pi + st_ref[rf + wt + t]
                pr, pi = _cmul(bar[t], bai[t], br[t], bi[t])
                br[t] = pr + st_ref[rb + 2 * wt + t]
                bi[t] = pi + st_ref[rb + 3 * wt + t]
            return tuple(fr), tuple(fi), tuple(br), tuple(bi)

        return lax.fori_loop(0, seglen, step, init)

    z = tuple(jnp.zeros((SCAN_SEGS, LANES), F32) for _ in range(wt))
    fr, fi, br, bi = scan((z, z, z, z), store=False)

    z1 = jnp.zeros((1, LANES), F32)
    fcr, fci, bcr, bci = [], [], [], []
    for t in range(wt):
        fsr, fsi, bsr, bsi = far[t][0:1], fai[t][0:1], bar[t][0:1], bai[t][0:1]
        for _ in range(seglen.bit_length() - 1):
            fsr, fsi = _cmul(fsr, fsi, fsr, fsi)
            bsr, bsi = _cmul(bsr, bsi, bsr, bsi)
        cr, ci = z1, z1
        rs, is_ = [], []
        for k in range(SCAN_SEGS):
            rs.append(cr)
            is_.append(ci)
            cr, ci = _cmul(fsr, fsi, cr, ci)
            cr, ci = cr + fr[t][k:k + 1], ci + fi[t][k:k + 1]
        fcr.append(jnp.concatenate(rs, axis=0))
        fci.append(jnp.concatenate(is_, axis=0))
        cr, ci = z1, z1
        rs, is_ = [None] * SCAN_SEGS, [None] * SCAN_SEGS
        for k in reversed(range(SCAN_SEGS)):
            rs[k] = cr
            is_[k] = ci
            cr, ci = _cmul(bsr, bsi, cr, ci)
            cr, ci = cr + br[t][k:k + 1], ci + bi[t][k:k + 1]
        bcr.append(jnp.concatenate(rs, axis=0))
        bci.append(jnp.concatenate(is_, axis=0))

    scan((tuple(fcr), tuple(fci), tuple(bcr), tuple(bci)), store=True)


def _s5_out_kernel(yi_ref, x_ref, c2_ref, y_ref, sout_ref, *, parts):
    ncb = yi_ref.shape[0]
    part = pl.program_id(2)
    segs = SCAN_SEGS // parts
    seglen = ncb // segs

    @pl.when((pl.program_id(1) == 0) & (part == 0))
    def _():
        sout_ref[...] = _spread(c2_ref[...], c2_ref.shape[1], SSM_CW, SSM_GROUP_CH, SSM_STATE, SSM_GROUP_CH)

    for pv in range(parts):
        @pl.when(part == pv)
        def _():
            x = jnp.concatenate(
                [jnp.concatenate([x_ref[_seg_rows(q, pv * segs + kk, seglen), :] for q in range(SSM_QPER)], axis=1)
                 for kk in range(segs)], axis=0).astype(BF16)
            y = yi_ref[...] + jnp.dot(x, sout_ref[...], preferred_element_type=F32)
            for t in range(SSM_CHUNK):
                y_ref[pl.ds(t, ncb, stride=SSM_CHUNK), :] = y[:, t * SSM_TILE:(t + 1) * SSM_TILE]


def _s5(u, a2, b2, c2, apow, bsz, ncb):
    t, width = u.shape
    no = width // SSM_TILE
    nc = t // SSM_CHUNK
    ncseq = nc // bsz
    parts = ncseq // ncb
    assert SCAN_SEGS % parts == 0
    rows = ncb * SSM_CHUNK
    slab = ncseq * SSM_QPER
    yi, st = pl.pallas_call(
        functools.partial(_s5_local_kernel, parts=parts),
        grid=(no, bsz, parts),
        in_specs=[
            pl.BlockSpec((rows, SSM_TILE), lambda o, b, p: (b * parts + p, o)),
            pl.BlockSpec((None,) + a2.shape[1:], lambda o, b, p: (o, 0, 0)),
            pl.BlockSpec((None,) + b2.shape[1:], lambda o, b, p: (o, 0, 0)),
        ],
        out_specs=[
            pl.BlockSpec((None, ncb, SSM_CW), lambda o, b, p: (o, b * parts + p, 0)),
            pl.BlockSpec((None, None, slab, LANES), lambda o, b, p: (o, b, 0, 0)),
        ],
        out_shape=[jax.ShapeDtypeStruct((no, nc, SSM_CW), F32),
                   jax.ShapeDtypeStruct((no, bsz, slab, LANES), F32)],
        scratch_shapes=[pltpu.VMEM((SSM_CW, SSM_CW), BF16), pltpu.VMEM((SSM_CW, SSM_SW), BF16)],
        compiler_params=_cparams(("arbitrary", "arbitrary", "arbitrary"), 52),
        name="s5_local",
    )(u, a2, b2)
    tiles = slab // SCAN_SEGS
    xin = pl.pallas_call(
        _s5_scan_kernel,
        grid=(no, bsz),
        in_specs=[
            pl.BlockSpec((None, None, tiles, SCAN_SEGS, LANES), lambda o, b: (o, b, 0, 0, 0)),
            pl.BlockSpec((None, 2, 2, SSM_SW // 4), lambda o, b: (o, 0, 0, 0)),
        ],
        out_specs=pl.BlockSpec((None, None, tiles, SCAN_SEGS, LANES), lambda o, b: (o, b, 0, 0, 0)),
        out_shape=jax.ShapeDtypeStruct((no, bsz, tiles, SCAN_SEGS, LANES), F32),
        compiler_params=_cparams(("arbitrary", "arbitrary"), 48),
        name="s5_scan",
    )(st.reshape(no, bsz, tiles, SCAN_SEGS, LANES), apow)
    return pl.pallas_call(
        functools.partial(_s5_out_kernel, parts=parts),
        grid=(no, bsz, parts),
        in_specs=[
            pl.BlockSpec((None, ncb, SSM_CW), lambda o, b, p: (o, b * parts + p, 0)),
            pl.BlockSpec((None, None, slab, LANES), lambda o, b, p: (o, b, 0, 0)),
            pl.BlockSpec((None,) + c2.shape[1:], lambda o, b, p: (o, 0, 0)),
        ],
        out_specs=pl.BlockSpec((rows, SSM_TILE), lambda o, b, p: (b * parts + p, o)),
        out_shape=jax.ShapeDtypeStruct((t, width), F32),
        scratch_shapes=[pltpu.VMEM((SSM_SW, SSM_CW), BF16)],
        compiler_params=_cparams(("arbitrary", "arbitrary", "arbitrary"), 52),
        name="s5_out",
    )(yi, xin.reshape(no, bsz, slab, LANES), c2)


def _merge_kernel(a_ref, y_ref, g0_ref, g1_ref, x_ref, wglu_ref, wb0_ref, wb1_ref, wout_ref, o_ref):
    y = y_ref[...]
    s = 0.5 * y * (1.0 + lax.erf(y * (2.0 ** -0.5)))
    glu = jnp.dot(s.astype(BF16), wglu_ref[...], preferred_element_type=F32)
    s = s * jax.nn.sigmoid(glu)
    pa = jnp.dot(a_ref[...], wb0_ref[...], preferred_element_type=F32)
    ps = jnp.dot(s.astype(BF16), wb1_ref[...], preferred_element_type=F32)
    merged = g0_ref[...].astype(F32) * pa + g1_ref[...].astype(F32) * ps
    o_ref[...] = x_ref[...] + jnp.dot(merged.astype(BF16), wout_ref[...], preferred_element_type=F32)


def _merge(a, y, proj, x, wglu, wbr, wout, layer, bm):
    t, d = x.shape
    aw = a.shape[1]
    g0 = 4 * SEG // d
    return pl.pallas_call(
        _merge_kernel,
        grid=(t // bm,),
        in_specs=[
            pl.BlockSpec((bm, aw), lambda i: (i, 0)),
            pl.BlockSpec((bm, aw), lambda i: (i, 0)),
            pl.BlockSpec((bm, d), lambda i: (i, g0)),
            pl.BlockSpec((bm, d), lambda i: (i, g0 + 1)),
            pl.BlockSpec((bm, d), lambda i: (i, 0)),
            pl.BlockSpec((None,) + wglu.shape[1:], lambda i: (layer, 0, 0)),
            pl.BlockSpec((None, None) + wbr.shape[2:], lambda i: (layer, 0, 0, 0)),
            pl.BlockSpec((None, None) + wbr.shape[2:], lambda i: (layer, 1, 0, 0)),
            pl.BlockSpec((None,) + wout.shape[1:], lambda i: (layer, 0, 0)),
        ],
        out_specs=pl.BlockSpec((bm, d), lambda i: (i, 0)),
        out_shape=jax.ShapeDtypeStruct((t, d), F32),
        compiler_params=_cparams(("arbitrary",), 48),
        name="merge",
    )(a, y, proj, proj, x, wglu, wbr, wbr, wout)


def _swiglu_partial(h, w1_ref, w3_ref, w2_ref, sub, side_work=None):
    ffc = w1_ref.shape[-1]
    acc = None
    for k, s in enumerate(range(0, ffc, sub)):
        e = min(s + sub, ffc)
        if side_work is not None:
            side_work(k, -(-ffc // sub))
        a = jnp.dot(h, w1_ref[:, s:e], preferred_element_type=F32)
        b = jnp.dot(h, w3_ref[:, s:e], preferred_element_type=F32)
        tt = (a * jax.nn.sigmoid(a) * b).astype(BF16)
        c = jnp.dot(tt, w2_ref[s:e, :], preferred_element_type=F32)
        acc = c if acc is None else acc + c
    return acc


FF_TILE = 2 * LANES
LOAD_DEPTH = 8


def _load_weights_bf16(w1_src, w3_src, w2_src, w1_ref, w3_ref, w2_ref, sta_ref, stb_ref, wsem):
    def stream(tiles, stage_ref):
        depth = stage_ref.shape[0]

        def copy(k):
            src, _ = tiles[k]
            return pltpu.make_async_copy(src, stage_ref.at[k % depth], wsem.at[k % depth])

        for k in range(min(depth, len(tiles))):
            copy(k).start()
        for k in range(len(tiles)):
            copy(k).wait()
            tiles[k][1](stage_ref[k % depth].astype(BF16))
            if k + depth < len(tiles):
                copy(k + depth).start()

    def row_sink(ref, c, rows):
        def put(v):
            ref[c * rows:(c + 1) * rows, :] = v
        return put

    ra, rb = sta_ref.shape[1], stb_ref.shape[1]
    up = []
    for c in range(w1_ref.shape[0] // ra):
        up.append((w1_src.at[pl.ds(c * ra, ra), :], row_sink(w1_ref, c, ra)))
        up.append((w3_src.at[pl.ds(c * ra, ra), :], row_sink(w3_ref, c, ra)))
    stream(up, sta_ref)
    stream([(w2_src.at[pl.ds(c * rb, rb), :], row_sink(w2_ref, c, rb)) for c in range(w2_ref.shape[0] // rb)],
           stb_ref)


def _ffn_weight_scratch(d, dff, dtype):
    return [pltpu.VMEM((d, dff), BF16), pltpu.VMEM((d, dff), BF16), pltpu.VMEM((dff, d), BF16),
            pltpu.VMEM((LOAD_DEPTH, LANES // 2, dff), dtype), pltpu.VMEM((LOAD_DEPTH, FF_TILE // 2, d), dtype),
            pltpu.SemaphoreType.DMA((LOAD_DEPTH,))]


def _ffn_kernel(x_ref, g_ref, w1_hbm, w3_hbm, w2_hbm, o_ref, w1_ref, w3_ref, w2_ref, sta_ref, stb_ref, wsem, *, li):
    @pl.when(pl.program_id(0) == 0)
    def _():
        _load_weights_bf16(w1_hbm.at[li], w3_hbm.at[li], w2_hbm.at[li], w1_ref, w3_ref, w2_ref,
                           sta_ref, stb_ref, wsem)

    x = x_ref[...]
    hn = _rms(x, g_ref[...]).astype(BF16)
    o_ref[...] = x + _swiglu_partial(hn, w1_ref, w3_ref, w2_ref, FF_TILE)


def _ffn(x, g, w1, w3, w2, li, bm):
    t, d = x.shape
    dff = w1.shape[2]
    assert dff % FF_TILE == 0
    return pl.pallas_call(
        functools.partial(_ffn_kernel, li=li),
        grid=(t // bm,),
        in_specs=[
            pl.BlockSpec((bm, d), lambda i: (i, 0)),
            pl.BlockSpec((1, d), lambda i: (0, 0)),
            pl.BlockSpec(memory_space=pl.ANY),
            pl.BlockSpec(memory_space=pl.ANY),
            pl.BlockSpec(memory_space=pl.ANY),
        ],
        out_specs=pl.BlockSpec((bm, d), lambda i: (i, 0)),
        out_shape=jax.ShapeDtypeStruct((t, d), F32),
        scratch_shapes=_ffn_weight_scratch(d, dff, w1.dtype),
        compiler_params=_cparams(("arbitrary",), 56),
        name="ffn_dense",
    )(x, g, w1, w3, w2)


def _router_kernel(x_ref, g_ref, wr_ref, hn_ref, r_ref):
    hn = _rms(x_ref[...], g_ref[...])
    _to_row_tiles(hn_ref, hn)
    logits = jnp.dot(hn, wr_ref[...], preferred_element_type=F32, precision=lax.Precision.HIGHEST)
    lane = lax.broadcasted_iota(I32, logits.shape, 1)
    neg = jnp.float32(-1e30)
    logits = jnp.where(lane < N_EXPERTS, logits, neg)
    m1 = jnp.max(logits, axis=-1, keepdims=True)
    i1 = jnp.min(jnp.where(logits == m1, lane, LANES), axis=-1, keepdims=True)
    rest = jnp.where(lane == i1, neg, logits)
    m2 = jnp.max(rest, axis=-1, keepdims=True)
    i2 = jnp.min(jnp.where(rest == m2, lane, LANES), axis=-1, keepdims=True)
    e2 = jnp.exp(m2 - m1)
    g1 = 1.0 / (1.0 + e2)
    g2 = e2 / (1.0 + e2)
    r_ref[...] = jnp.where(lane == 0, i1.astype(F32),
                           jnp.where(lane == 1, i2.astype(F32),
                                     jnp.where(lane == 2, g1, jnp.where(lane == 3, g2, 0.0))))


def _router(x, g, wr_pad, bm):
    t, d = x.shape
    s_per = d // LANES
    return pl.pallas_call(
        _router_kernel,
        grid=(t // bm,),
        in_specs=[
            pl.BlockSpec((bm, d), lambda i: (i, 0)),
            pl.BlockSpec((1, d), lambda i: (0, 0)),
            pl.BlockSpec((d, LANES), lambda i: (0, 0)),
        ],
        out_specs=[pl.BlockSpec((bm * s_per, LANES), lambda i: (i, 0)), pl.BlockSpec((bm, LANES), lambda i: (i, 0))],
        out_shape=[jax.ShapeDtypeStruct((t * s_per, LANES), F32), jax.ShapeDtypeStruct((t, LANES), F32)],
        compiler_params=_cparams(("arbitrary",), 40),
        name="moe_router",
    )(x, g, wr_pad)


def _row_copy(src_hbm, dst_ref, sem, src_row, dst_row, s_per):
    src = pl.multiple_of(src_row * s_per, s_per)
    dst = pl.multiple_of(dst_row * s_per, s_per)
    return pltpu.make_async_copy(src_hbm.at[pl.ds(src, s_per)], dst_ref.at[pl.ds(dst, s_per)], sem)


def _expert_kernel(be_ref, tok_ref, hn_hbm, w1_hbm, w3_hbm, w2_hbm, o_ref,
                   rows_ref, rsem, w1_ref, w3_ref, w2_ref, sta_ref, stb_ref, wsem, *, li, s_per):
    i = pl.program_id(0)
    bm = rows_ref.shape[1] // s_per

    def issue(step, slot):
        base = step * bm

        def body(r, c):
            _row_copy(hn_hbm, rows_ref.at[slot], rsem.at[slot], tok_ref[base + r], r, s_per).start()
            return c

        lax.fori_loop(0, bm, body, 0, unroll=8)

    def drain(slot):
        def body(r, c):
            _row_copy(hn_hbm, rows_ref.at[slot], rsem.at[slot], 0, r, s_per).wait()
            return c

        lax.fori_loop(0, bm, body, 0, unroll=8)

    @pl.when(i == 0)
    def _():
        issue(0, 0)

    nblk = pl.num_programs(0)
    active = i < be_ref[nblk]
    e = be_ref[i]
    changed = active & ((i == 0) | (e != be_ref[jnp.maximum(i - 1, 0)]))

    @pl.when(changed)
    def _():
        _load_weights_bf16(w1_hbm.at[li, e], w3_hbm.at[li, e], w2_hbm.at[li, e], w1_ref, w3_ref, w2_ref,
                           sta_ref, stb_ref, wsem)

    for slot in range(2):
        @pl.when((i % 2 == slot) & active)
        def _():
            drain(slot)
            xb = _from_row_tiles(rows_ref.at[slot], s_per).astype(BF16)

            def prefetch(k, n):
                per = -(-bm // n)
                for r in range(k * per, min(bm, (k + 1) * per)):
                    _row_copy(hn_hbm, rows_ref.at[1 - slot], rsem.at[1 - slot],
                              tok_ref[(i + 1) * bm + r], r, s_per).start()

            _to_row_tiles(o_ref, _swiglu_partial(xb, w1_ref, w3_ref, w2_ref, FF_TILE, prefetch))

        @pl.when((i % 2 == slot) & jnp.logical_not(active))
        def _():
            drain(slot)
            issue(i + 1, 1 - slot)
            o_ref[...] = jnp.zeros_like(o_ref)

        @pl.when((i % 2 == slot) & (i == nblk - 1))
        def _():
            drain(1 - slot)


def _experts(block_exp, slot_tok, hn, w1, w3, w2, li, bm):
    d = w1.shape[2]
    dff = w1.shape[3]
    s_per = d // LANES
    n = slot_tok.shape[0]
    assert dff % FF_TILE == 0
    slot_tok = jnp.concatenate([slot_tok, jnp.zeros((bm,), slot_tok.dtype)])
    return pl.pallas_call(
        functools.partial(_expert_kernel, li=li, s_per=s_per),
        grid_spec=pltpu.PrefetchScalarGridSpec(
            num_scalar_prefetch=2,
            grid=(n // bm,),
            in_specs=[pl.BlockSpec(memory_space=pl.ANY)] * 4,
            out_specs=pl.BlockSpec((bm * s_per, LANES), lambda i, be, tok: (i, 0)),
            scratch_shapes=[pltpu.VMEM((2, bm * s_per, LANES), F32), pltpu.SemaphoreType.DMA((2,))]
            + _ffn_weight_scratch(d, dff, w1.dtype),
        ),
        out_shape=jax.ShapeDtypeStruct((n * s_per, LANES), F32),
        compiler_params=_cparams(("arbitrary",), 56),
        name="moe_experts",
    )(block_exp, slot_tok, hn, w1, w3, w2)


def _combine_kernel(d0_ref, d1_ref, yb_hbm, x_ref, r_ref, g_ref, o_ref, r0_ref, r1_ref, sem, *, s_per, normalize):
    bm = o_ref.shape[0]
    i = pl.program_id(0)

    def issue(step, slot):
        base = step * bm

        def body(r, c):
            _row_copy(yb_hbm, r0_ref.at[slot], sem.at[slot], d0_ref[base + r], r, s_per).start()
            _row_copy(yb_hbm, r1_ref.at[slot], sem.at[slot], d1_ref[base + r], r, s_per).start()
            return c

        lax.fori_loop(0, bm, body, 0, unroll=8)

    def drain(slot):
        def body(r, c):
            _row_copy(yb_hbm, r0_ref.at[slot], sem.at[slot], 0, r, s_per).wait()
            _row_copy(yb_hbm, r1_ref.at[slot], sem.at[slot], 0, r, s_per).wait()
            return c

        lax.fori_loop(0, bm, body, 0, unroll=8)

    @pl.when(i == 0)
    def _():
        issue(0, 0)

    for slot in range(2):
        @pl.when(i % 2 == slot)
        def _():
            @pl.when(i + 1 < pl.num_programs(0))
            def _():
                issue(i + 1, 1 - slot)

            drain(slot)
            route = r_ref[...]
            out = (x_ref[...] + route[:, 2:3] * _from_row_tiles(r0_ref.at[slot], s_per)
                   + route[:, 3:4] * _from_row_tiles(r1_ref.at[slot], s_per))
            o_ref[...] = _rms(out, g_ref[...]) if normalize else out


def _combine(d0, d1, yb, x, route, g_final, normalize, bm):
    t, d = x.shape
    s_per = d // LANES
    return pl.pallas_call(
        functools.partial(_combine_kernel, s_per=s_per, normalize=normalize),
        grid_spec=pltpu.PrefetchScalarGridSpec(
            num_scalar_prefetch=2,
            grid=(t // bm,),
            in_specs=[
                pl.BlockSpec(memory_space=pl.ANY),
                pl.BlockSpec((bm, d), lambda i, a, b: (i, 0)),
                pl.BlockSpec((bm, LANES), lambda i, a, b: (i, 0)),
                pl.BlockSpec((1, d), lambda i, a, b: (0, 0)),
            ],
            out_specs=pl.BlockSpec((bm, d), lambda i, a, b: (i, 0)),
            scratch_shapes=[pltpu.VMEM((2, bm * s_per, LANES), F32), pltpu.VMEM((2, bm * s_per, LANES), F32),
                            pltpu.SemaphoreType.DMA((2,))],
        ),
        out_shape=jax.ShapeDtypeStruct((t, d), F32),
        compiler_params=_cparams(("arbitrary",), 32),
        name="moe_combine",
    )(d0, d1, yb, x, route, g_final)


def _dispatch_tables(idx, bm):
    t = idx.shape[0]
    flat_e = idx.reshape(-1)
    onehot = (flat_e[:, None] == jnp.arange(N_EXPERTS, dtype=I32)[None, :]).astype(I32)
    csum = jnp.cumsum(onehot, axis=0)
    rank = jnp.take_along_axis(csum, flat_e[:, None], axis=1)[:, 0] - 1
    counts = csum[-1]
    padded = (counts + bm - 1) // bm * bm
    pad_end = jnp.cumsum(padded)
    pad_start = pad_end - padded
    dest = (pad_start[flat_e] + rank).astype(I32)
    n_slots = t * TOP_K + N_EXPERTS * bm
    slot_tok = jnp.zeros((n_slots,), I32).at[dest].set(jnp.arange(t * TOP_K, dtype=I32) // TOP_K)
    block_start = jnp.arange(n_slots // bm, dtype=I32) * bm
    block_exp = jnp.minimum(jnp.searchsorted(pad_end, block_start, side='right'), N_EXPERTS - 1).astype(I32)
    block_exp = jnp.concatenate([block_exp, (pad_end[-1:] // bm).astype(I32)])
    return dest.reshape(t, TOP_K), slot_tok, block_exp


def _moe(x, g, w_router, w1, w3, w2, li, g_final, normalize, bm_tok, bm_slot):
    d = x.shape[1]
    wr_pad = jnp.zeros((d, LANES), F32).at[:, :N_EXPERTS].set(w_router)
    hn, route = _router(x, g, wr_pad, bm_tok)
    idx = route[:, :TOP_K].astype(I32)
    dest, slot_tok, block_exp = _dispatch_tables(idx, bm_slot)
    yb = _experts(block_exp, slot_tok, hn, w1, w3, w2, li, bm_slot)
    return _combine(dest[:, 0], dest[:, 1], yb, x, route, g_final, normalize, bm_tok)


def _norm_kernel(x_ref, g_ref, o_ref):
    o_ref[...] = _rms(x_ref[...], g_ref[...])


def _final_norm(x, g, bm):
    t, d = x.shape
    return pl.pallas_call(
        _norm_kernel,
        grid=(t // bm,),
        in_specs=[pl.BlockSpec((bm, d), lambda i: (i, 0)), pl.BlockSpec((1, d), lambda i: (0, 0))],
        out_specs=pl.BlockSpec((bm, d), lambda i: (i, 0)),
        out_shape=jax.ShapeDtypeStruct((t, d), F32),
        compiler_params=_cparams(("arbitrary",), 32),
        name="final_norm",
    )(x, g)


def _rope_tables(seq):
    half = ATTN_HEAD_DIM // 2
    inv_freq = 1.0 / (ROPE_THETA ** (jnp.arange(half, dtype=F32) / half))
    ang = jnp.arange(seq, dtype=F32)[:, None] * inv_freq[None, :]
    cos = jnp.tile(jnp.cos(ang), (1, 4))
    sin = jnp.sin(ang)
    return cos, jnp.concatenate([-sin, sin, -sin, sin], axis=1)


def _block(n, pref):
    return pref if n % pref == 0 else n


def kernel(x, norm_mix, w_in, attn_lambda, attn_subln, ssm_lam_re, ssm_lam_im, ssm_log_dt, ssm_b_re, ssm_b_im, ssm_c_re, ssm_c_im, ssm_d, w_glu, w_branch, w_out, norm_ffn, ffn_w1, ffn_w3, ffn_w2, moe_router, moe_w1, moe_w3, moe_w2, norm_final):
    bsz, seq, d = x.shape
    depth = w_in.shape[0]
    t = bsz * seq
    cos_t, sin_t = _rope_tables(seq)
    xf = x.reshape(t, d)
    bm = _block(seq, 1024)
    w_all = w_in.astype(BF16)
    a2, b2, c2, apow = jax.vmap(_s5_tables)(ssm_lam_re, ssm_lam_im, ssm_log_dt, ssm_b_re, ssm_b_im,
                                            ssm_c_re, ssm_c_im, ssm_d)
    wglu, wbr, wout = w_glu.astype(BF16), w_branch.astype(BF16), w_out.astype(BF16)
    for layer in range(depth):
        proj, u = _inproj(xf, norm_mix[layer][None], w_all, layer, cos_t, sin_t, seq, _block(seq, 512))

        lam_init = jnp.full((1,), 0.8 - 0.6 * math.exp(-0.3 * layer), F32)
        a = _attention(proj, lam_init, attn_lambda[layer], attn_subln[layer][None], bsz, seq,
                       _block(seq, 1024), _block(seq // 2, 512))

        y = _s5(u, a2[layer], b2[layer], c2[layer], apow[layer], bsz, _block(seq // SSM_CHUNK, 512))

        xf = _merge(a, y, proj, xf, wglu, wbr, wout, layer, _block(seq, 512))

        i = layer // 2
        if layer % 2 == 0:
            xf = _ffn(xf, norm_ffn[layer][None], ffn_w1, ffn_w3, ffn_w2, i, _block(seq, 512))
        else:
            last = layer == depth - 1
            xf = _moe(xf, norm_ffn[layer][None], moe_router[i], moe_w1, moe_w3, moe_w2, i,
                      norm_final[None], last, _block(seq, 256), 512)
    if depth % 2 == 1:
        xf = _final_norm(xf, norm_final[None], bm)
    return xf.reshape(bsz, seq, d)
```

```python
import functools
import math

import jax
import jax.numpy as jnp
from jax import lax
from jax.experimental import pallas as pl
from jax.experimental.pallas import tpu as pltpu

F32 = jnp.float32
BF16 = jnp.bfloat16
I32 = jnp.int32

RMS_EPS = 1e-6
ROPE_THETA = 10000.0
LOG2E = 1.4426950408889634

LANES = 128
SUBLANES = 8
VMEM_BYTES_V7X = 64 * 1024 * 1024

ATTN_HEADS = 4
ATTN_HEAD_DIM = 64
HEAD_SLAB = 2 * ATTN_HEAD_DIM
SSM_GROUP_CH = 16
SSM_STATE = 64
SSM_OCTET = LANES // SSM_GROUP_CH
SSM_CHUNK = 8
N_EXPERTS = 8
TOP_K = 2
SEG = 512


def _cparams(sem, vmem_mb):
    return pltpu.CompilerParams(dimension_semantics=sem, vmem_limit_bytes=vmem_mb * 1024 * 1024)


def _rms(x, g):
    return x * lax.rsqrt(jnp.mean(x * x, axis=-1, keepdims=True) + RMS_EPS) * g


def _sigmoid(x):
    return 0.5 * jnp.tanh(0.5 * x) + 0.5


def _inproj_kernel(x_ref, g_ref, w_ref, cos_ref, sin_ref, o_ref, u_ref, *, qscale):
    xn = _rms(x_ref[...], g_ref[...]).astype(BF16)
    cos = cos_ref[...]
    sin = sin_ref[...]

    half = ATTN_HEAD_DIM // 2
    first = (lax.broadcasted_iota(I32, cos.shape, 1) % ATTN_HEAD_DIM) < half

    def rot(a):
        parts = []
        for s in range(0, a.shape[1], HEAD_SLAB):
            blk = a[:, s:s + HEAD_SLAB]
            partner = jnp.where(first, pltpu.roll(blk, HEAD_SLAB - half, 1), pltpu.roll(blk, half, 1))
            parts.append(blk * cos + partner * sin)
        return jnp.concatenate(parts, axis=1)

    for j in range(w_ref.shape[1] // SEG):
        acc = jnp.dot(xn, w_ref[:, j * SEG:(j + 1) * SEG], preferred_element_type=F32)
        if j == 0:
            acc = rot(acc) * qscale
        elif j == 1:
            acc = rot(acc)
        elif j == 3:
            u_ref[...] = acc
        elif j >= 4:
            acc = _sigmoid(acc)
        o_ref[:, j * SEG:(j + 1) * SEG] = acc.astype(BF16)


def _inproj(x, g, w, layer, cos_t, sin_t, seq, bm):
    t, d = x.shape
    n = w.shape[2]
    qscale = ATTN_HEAD_DIM ** -0.5 * LOG2E
    nseq = seq // bm
    return pl.pallas_call(
        functools.partial(_inproj_kernel, qscale=qscale),
        grid=(t // bm,),
        in_specs=[
            pl.BlockSpec((bm, d), lambda i: (i, 0)),
            pl.BlockSpec((1, d), lambda i: (0, 0)),
            pl.BlockSpec((None, d, n), lambda i: (layer, 0, 0)),
            pl.BlockSpec((bm, HEAD_SLAB), lambda i: (i % nseq, 0)),
            pl.BlockSpec((bm, HEAD_SLAB), lambda i: (i % nseq, 0)),
        ],
        out_specs=[pl.BlockSpec((bm, n), lambda i: (i, 0)), pl.BlockSpec((bm, SEG), lambda i: (i, 0))],
        out_shape=[jax.ShapeDtypeStruct((t, n), BF16), jax.ShapeDtypeStruct((t, SEG), F32)],
        compiler_params=_cparams(("arbitrary",), 52),
        name="inproj",
    )(x, g, w, cos_t, sin_t)


ONES_ROWS = 16


def _attn_kernel(sc_ref, lp_ref, q_ref, k_ref, v_ref, g_ref, o_ref, vt_ref, acc_ref, m_ref,
                 s0_ref, s1_ref, p0_ref, p1_ref, x0_ref, x1_ref, *, bk):
    qi = pl.program_id(2)
    seq = k_ref.shape[0]
    bq = q_ref.shape[0]
    nkv = seq // bk
    assert nkv >= 2 and nkv % 2 == 0

    @pl.when(qi == 0)
    def _():
        for c in range(nkv):
            vt_ref[c, :HEAD_SLAB, :] = v_ref[c * bk:(c + 1) * bk, :].astype(F32).T.astype(BF16)
            vt_ref[c, HEAD_SLAB:, :] = jnp.ones((ONES_ROWS, bk), BF16)

    qt = q_ref[...].astype(F32).T
    row = lax.broadcasted_iota(I32, qt.shape, 0)
    half0 = row < ATTN_HEAD_DIM
    qst = jnp.concatenate([jnp.where(half0, qt, 0.0), jnp.where(half0, 0.0, qt)], axis=1).astype(BF16)

    acc_ref[...] = jnp.zeros_like(acc_ref)
    m_ref[...] = jnp.full_like(m_ref, -1e30)
    s_refs = (s0_ref, s1_ref)
    p_refs = (p0_ref, p1_ref)
    x_refs = (x0_ref, x1_ref)

    def qk(j, cur):
        off = pl.multiple_of(j * bk, bk)
        st = jnp.dot(k_ref[pl.ds(off, bk), :], qst, preferred_element_type=F32)
        s_refs[cur][...] = st
        x_refs[cur][...] = jnp.max(st, axis=0, keepdims=True)

    def pv(j, cur, alpha):
        acc_ref[...] = acc_ref[...] * alpha + jnp.dot(vt_ref[j], p_refs[cur][...],
                                                      preferred_element_type=F32)

    def softmax(cur):
        m_prev = m_ref[...]
        m_new = jnp.maximum(m_prev, x_refs[cur][...])
        p_refs[cur][...] = jnp.exp2((s_refs[cur][...] - m_new).astype(BF16))
        m_ref[...] = m_new
        return jnp.exp2(m_prev - m_new)

    def stage(j, cur, alpha):
        qk(j + 1, 1 - cur)
        pv(j - 1, 1 - cur, alpha)
        return softmax(cur)

    qk(0, 0)
    qk(1, 1)
    alpha = softmax(0)

    def pair(jj, alpha):
        j = 1 + 2 * jj
        return stage(j + 1, 0, stage(j, 1, alpha))

    alpha = lax.fori_loop(0, (nkv - 2) // 2, pair, alpha)
    pv(nkv - 2, 0, alpha)
    alpha = softmax(1)
    pv(nkv - 1, 1, alpha)

    lam_init = sc_ref[0]
    lp = lp_ref[...]
    lam = (jnp.exp(jnp.sum(lp[0:1] * lp[1:2], axis=-1, keepdims=True))
           - jnp.exp(jnp.sum(lp[2:3] * lp[3:4], axis=-1, keepdims=True)) + lam_init)
    acc = acc_ref[...]
    o = acc[:HEAD_SLAB] / acc[HEAD_SLAB:HEAD_SLAB + 1]
    a = o[:, :bq] - lam * o[:, bq:]
    a = a * lax.rsqrt(jnp.mean(a * a, axis=0, keepdims=True) + RMS_EPS) * (1.0 - lam_init)
    o_ref[...] = (a.T * g_ref[...]).astype(BF16)


def _attention(proj, lam_init, lam_p, subln, bsz, seq, bq, bk):
    t = proj.shape[0]
    nq = seq // bq
    koff = SEG // HEAD_SLAB
    return pl.pallas_call(
        functools.partial(_attn_kernel, bk=bk),
        grid=(bsz, ATTN_HEADS, nq),
        in_specs=[
            pl.BlockSpec(memory_space=pltpu.SMEM),
            pl.BlockSpec((4, ATTN_HEAD_DIM), lambda b, h, i: (0, 0)),
            pl.BlockSpec((bq, HEAD_SLAB), lambda b, h, i: (b * nq + i, h)),
            pl.BlockSpec((seq, HEAD_SLAB), lambda b, h, i: (b, koff + h)),
            pl.BlockSpec((seq, HEAD_SLAB), lambda b, h, i: (b, 2 * koff + h)),
            pl.BlockSpec((1, HEAD_SLAB), lambda b, h, i: (0, 0)),
        ],
        out_specs=pl.BlockSpec((bq, HEAD_SLAB), lambda b, h, i: (b * nq + i, h)),
        out_shape=jax.ShapeDtypeStruct((t, ATTN_HEADS * HEAD_SLAB), BF16),
        scratch_shapes=[
            pltpu.VMEM((seq // bk, HEAD_SLAB + ONES_ROWS, bk), BF16),
            pltpu.VMEM((HEAD_SLAB + ONES_ROWS, 2 * bq), F32),
            pltpu.VMEM((1, 2 * bq), F32),
            pltpu.VMEM((bk, 2 * bq), F32),
            pltpu.VMEM((bk, 2 * bq), F32),
            pltpu.VMEM((bk, 2 * bq), BF16),
            pltpu.VMEM((bk, 2 * bq), BF16),
            pltpu.VMEM((1, 2 * bq), F32),
            pltpu.VMEM((1, 2 * bq), F32),
        ],
        compiler_params=_cparams(("arbitrary", "arbitrary", "arbitrary"), 56),
        name="diff_attention",
    )(lam_init, lam_p, proj, proj, proj, subln)


SSM_TILE = SSM_OCTET * SSM_GROUP_CH
SSM_CW = SSM_CHUNK * SSM_TILE
SSM_SW = 4 * SSM_OCTET * SSM_STATE
SSM_QPER = SSM_SW // LANES
SCAN_SEGS = SUBLANES


def _s5_tables(lam_re, lam_im, log_dt, b_re, b_im, c_re, c_im, d_skip):
    hp = lax.Precision.HIGHEST
    ll = SSM_CHUNK
    g = lam_re.shape[1]
    no = g // SSM_OCTET
    lam = lax.complex(lam_re, lam_im)
    ldt = lam * jnp.exp(log_dt)[..., None]
    a = jnp.exp(ldt)
    bbar = ((a - 1.0) / lam)[..., None] * lax.complex(b_re, b_im)
    cmat = lax.complex(c_re, c_im)
    n = jnp.arange(ll + 1, dtype=F32)
    pw = jnp.exp(ldt[:, None] * n[None, :, None, None])

    def kern(d):
        return jnp.real(jnp.einsum('ghp,dgp,gpk->dghk', cmat[d], pw[d, :ll], bbar[d], precision=hp))

    kf, kb = kern(0), kern(1)
    tt = jnp.arange(ll)
    lag = tt[None, :] - tt[:, None]
    mf = jnp.where((lag >= 0)[:, :, None, None, None], kf[jnp.clip(lag, 0, ll - 1)], 0.0)
    mb = jnp.where((lag <= 0)[:, :, None, None, None], kb[jnp.clip(-lag, 0, ll - 1)], 0.0)
    skip = (jnp.eye(ll, dtype=F32)[:, :, None, None, None]
            * (d_skip.reshape(g, SSM_GROUP_CH)[None, None, :, :, None] * jnp.eye(SSM_GROUP_CH, dtype=F32)[None, None, None]))
    m = (mf + mb + skip).reshape(ll, ll, no, SSM_OCTET, SSM_GROUP_CH, SSM_GROUP_CH)
    a2 = m.transpose(2, 0, 3, 5, 1, 4).reshape(no, SSM_CW, ll * SSM_GROUP_CH)

    ein = jnp.stack([pw[0, ll - 1 - tt][..., None] * bbar[0][None],
                     pw[1, tt][..., None] * bbar[1][None]])
    ein = jnp.stack([jnp.real(ein), jnp.imag(ein)], axis=1)
    ein = ein.reshape(2, 2, ll, no, SSM_OCTET, SSM_STATE, SSM_GROUP_CH)
    b2 = ein.transpose(3, 2, 4, 6, 0, 1, 5).reshape(no, SSM_CW, 4 * SSM_STATE)

    eout = jnp.stack([cmat[0][None] * pw[0, tt + 1][:, :, None, :],
                      cmat[1][None] * pw[1, ll - tt][:, :, None, :]])
    eout = jnp.stack([jnp.real(eout), -jnp.imag(eout)], axis=1)
    eout = eout.reshape(2, 2, ll, no, SSM_OCTET, SSM_GROUP_CH, SSM_STATE)
    c2 = eout.transpose(3, 0, 1, 4, 6, 2, 5).reshape(no, SSM_SW, ll * SSM_GROUP_CH)

    al = pw[:, ll]
    apow = jnp.stack([jnp.real(al), jnp.imag(al)], axis=1)
    apow = apow.reshape(2, 2, no, SSM_OCTET * SSM_STATE).transpose(2, 0, 1, 3)
    return a2.astype(BF16), b2.astype(BF16), c2.astype(BF16), apow.astype(F32)


def _to_row_tiles(ref, x):
    n = x.shape[0]
    s_per = x.shape[1] // LANES
    for s in range(s_per):
        ref[pl.ds(s, n, stride=s_per), :] = x[:, s * LANES:(s + 1) * LANES]


def _from_row_tiles(ref, s_per):
    n = ref.shape[0] // s_per
    return jnp.concatenate([ref[pl.ds(s, n, stride=s_per), :] for s in range(s_per)], axis=1)


def _shr(idx, n):
    assert n & (n - 1) == 0
    return idx >> (n.bit_length() - 1)


def _group_of(idx, width):
    return _shr(idx, width) & (SSM_OCTET - 1)


def _spread(compact, rep_rows, rep_cols, inner, row_w, col_w):
    q = lax.broadcasted_iota(I32, (rep_rows, rep_cols), 0)
    c = lax.broadcasted_iota(I32, (rep_rows, rep_cols), 1)
    rep = ((_shr(q, inner) == _shr(c, inner * SSM_OCTET)) & ((q & (inner - 1)) == (c & (inner - 1)))).astype(BF16)
    full = jnp.dot(compact, rep, preferred_element_type=F32)
    r = lax.broadcasted_iota(I32, full.shape, 0)
    cc = lax.broadcasted_iota(I32, full.shape, 1)
    return jnp.where(_group_of(r, row_w) == _group_of(cc, col_w), full, 0.0).astype(BF16)


def _seg_rows(q, k, seglen):
    return pl.ds(q * SCAN_SEGS + k, seglen, stride=SSM_QPER * SCAN_SEGS)


def _s5_local_kernel(u_ref, a2_ref, b2_ref, yi_ref, st_ref, toe_ref, sin_ref, *, parts):
    ncb = yi_ref.shape[0]
    part = pl.program_id(2)
    segs = SCAN_SEGS // parts
    seglen = ncb // segs

    @pl.when((pl.program_id(1) == 0) & (part == 0))
    def _():
        toe_ref[...] = _spread(a2_ref[...], a2_ref.shape[1], SSM_CW, SSM_GROUP_CH, SSM_GROUP_CH, SSM_GROUP_CH)
        sin_ref[...] = _spread(b2_ref[...], b2_ref.shape[1], SSM_SW, SSM_STATE, SSM_GROUP_CH, SSM_STATE)

    u = jnp.concatenate([u_ref[pl.ds(j, ncb, stride=SSM_CHUNK), :] for j in range(SSM_CHUNK)],
                        axis=1).astype(BF16)
    yi_ref[...] = jnp.dot(u, toe_ref[...], preferred_element_type=F32)
    st = jnp.dot(u, sin_ref[...], preferred_element_type=F32)
    for pv in range(parts):
        @pl.when(part == pv)
        def _():
            for kk in range(segs):
                for q in range(SSM_QPER):
                    st_ref[_seg_rows(q, pv * segs + kk, seglen), :] = (
                        st[kk * seglen:(kk + 1) * seglen, q * LANES:(q + 1) * LANES])


def _cmul(ar, ai, xr, xi):
    return ar * xr - ai * xi, ar * xi + ai * xr


def _s5_scan_kernel(st_ref, ap_ref, x_ref):
    wt = SSM_QPER // 4
    seglen = st_ref.shape[0] // SSM_QPER
    assert seglen & (seglen - 1) == 0
    ap = ap_ref[...]

    def coef(d, r):
        return [jnp.broadcast_to(ap[d, r:r + 1, t * LANES:(t + 1) * LANES], (SCAN_SEGS, LANES)) for t in range(wt)]

    far, fai, bar, bai = coef(0, 0), coef(0, 1), coef(1, 0), coef(1, 1)

    def scan(init, store):
        def step(i, carry):
            fr, fi, br, bi = [list(c) for c in carry]
            rf = i * SSM_QPER
            rb = (seglen - 1 - i) * SSM_QPER
            for t in range(wt):
                if store:
                    x_ref[rf + t] = fr[t]
                    x_ref[rf + wt + t] = fi[t]
                    x_ref[rb + 2 * wt + t] = br[t]
                    x_ref[rb + 3 * wt + t] = bi[t]
                pr, pi = _cmul(far[t], fai[t], fr[t], fi[t])
                fr[t] = pr + st_ref[rf + t]
                fi[t] = pi + st_ref[rf + wt + t]
                pr, pi = _cmul(bar[t], bai[t], br[t], bi[t])
                br[t] = pr + st_ref[rb + 2 * wt + t]
                bi[t] = pi + st_ref[rb + 3 * wt + t]
            return tuple(fr), tuple(fi), tuple(br), tuple(bi)

        return lax.fori_loop(0, seglen, step, init)

    z = tuple(jnp.zeros((SCAN_SEGS, LANES), F32) for _ in range(wt))
    fr, fi, br, bi = scan((z, z, z, z), store=False)

    z1 = jnp.zeros((1, LANES), F32)
    fcr, fci, bcr, bci = [], [], [], []
    for t in range(wt):
        fsr, fsi, bsr, bsi = far[t][0:1], fai[t][0:1], bar[t][0:1], bai[t][0:1]
        for _ in range(seglen.bit_length() - 1):
            fsr, fsi = _cmul(fsr, fsi, fsr, fsi)
            bsr, bsi = _cmul(bsr, bsi, bsr, bsi)
        cr, ci = z1, z1
        rs, is_ = [], []
        for k in range(SCAN_SEGS):
            rs.append(cr)
            is_.append(ci)
            cr, ci = _cmul(fsr, fsi, cr, ci)
            cr, ci = cr + fr[t][k:k + 1], ci + fi[t][k:k + 1]
        fcr.append(jnp.concatenate(rs, axis=0))
        fci.append(jnp.concatenate(is_, axis=0))
        cr, ci = z1, z1
        rs, is_ = [None] * SCAN_SEGS, [None] * SCAN_SEGS
        for k in reversed(range(SCAN_SEGS)):
            rs[k] = cr
            is_[k] = ci
            cr, ci = _cmul(bsr, bsi, cr, ci)
            cr, ci = cr + br[t][k:k + 1], ci + bi[t][k:k + 1]
        bcr.append(jnp.concatenate(rs, axis=0))
        bci.append(jnp.concatenate(is_, axis=0))

    scan((tuple(fcr), tuple(fci), tuple(bcr), tuple(bci)), store=True)


def _s5_out_kernel(yi_ref, x_ref, c2_ref, y_ref, sout_ref, *, parts):
    ncb = yi_ref.shape[0]
    part = pl.program_id(2)
    segs = SCAN_SEGS // parts
    seglen = ncb // segs

    @pl.when((pl.program_id(1) == 0) & (part == 0))
    def _():
        sout_ref[...] = _spread(c2_ref[...], c2_ref.shape[1], SSM_CW, SSM_GROUP_CH, SSM_STATE, SSM_GROUP_CH)

    for pv in range(parts):
        @pl.when(part == pv)
        def _():
            x = jnp.concatenate(
                [jnp.concatenate([x_ref[_seg_rows(q, pv * segs + kk, seglen), :] for q in range(SSM_QPER)], axis=1)
                 for kk in range(segs)], axis=0).astype(BF16)
            y = yi_ref[...] + jnp.dot(x, sout_ref[...], preferred_element_type=F32)
            for t in range(SSM_CHUNK):
                y_ref[pl.ds(t, ncb, stride=SSM_CHUNK), :] = y[:, t * SSM_TILE:(t + 1) * SSM_TILE]


def _s5(u, a2, b2, c2, apow, bsz, ncb):
    t, width = u.shape
    no = width // SSM_TILE
    nc = t // SSM_CHUNK
    ncseq = nc // bsz
    parts = ncseq // ncb
    assert SCAN_SEGS % parts == 0
    rows = ncb * SSM_CHUNK
    slab = ncseq * SSM_QPER
    yi, st = pl.pallas_call(
        functools.partial(_s5_local_kernel, parts=parts),
        grid=(no, bsz, parts),
        in_specs=[
            pl.BlockSpec((rows, SSM_TILE), lambda o, b, p: (b * parts + p, o)),
            pl.BlockSpec((None,) + a2.shape[1:], lambda o, b, p: (o, 0, 0)),
            pl.BlockSpec((None,) + b2.shape[1:], lambda o, b, p: (o, 0, 0)),
        ],
        out_specs=[
            pl.BlockSpec((None, ncb, SSM_CW), lambda o, b, p: (o, b * parts + p, 0)),
            pl.BlockSpec((None, None, slab, LANES), lambda o, b, p: (o, b, 0, 0)),
        ],
        out_shape=[jax.ShapeDtypeStruct((no, nc, SSM_CW), F32),
                   jax.ShapeDtypeStruct((no, bsz, slab, LANES), F32)],
        scratch_shapes=[pltpu.VMEM((SSM_CW, SSM_CW), BF16), pltpu.VMEM((SSM_CW, SSM_SW), BF16)],
        compiler_params=_cparams(("arbitrary", "arbitrary", "arbitrary"), 52),
        name="s5_local",
    )(u, a2, b2)
    tiles = slab // SCAN_SEGS
    xin = pl.pallas_call(
        _s5_scan_kernel,
        grid=(no, bsz),
        in_specs=[
            pl.BlockSpec((None, None, tiles, SCAN_SEGS, LANES), lambda o, b: (o, b, 0, 0, 0)),
            pl.BlockSpec((None, 2, 2, SSM_SW // 4), lambda o, b: (o, 0, 0, 0)),
        ],
        out_specs=pl.BlockSpec((None, None, tiles, SCAN_SEGS, LANES), lambda o, b: (o, b, 0, 0, 0)),
        out_shape=jax.ShapeDtypeStruct((no, bsz, tiles, SCAN_SEGS, LANES), F32),
        compiler_params=_cparams(("arbitrary", "arbitrary"), 48),
        name="s5_scan",
    )(st.reshape(no, bsz, tiles, SCAN_SEGS, LANES), apow)
    return pl.pallas_call(
        functools.partial(_s5_out_kernel, parts=parts),
        grid=(no, bsz, parts),
        in_specs=[
            pl.BlockSpec((None, ncb, SSM_CW), lambda o, b, p: (o, b * parts + p, 0)),
            pl.BlockSpec((None, None, slab, LANES), lambda o, b, p: (o, b, 0, 0)),
            pl.BlockSpec((None,) + c2.shape[1:], lambda o, b, p: (o, 0, 0)),
        ],
        out_specs=pl.BlockSpec((rows, SSM_TILE), lambda o, b, p: (b * parts + p, o)),
        out_shape=jax.ShapeDtypeStruct((t, width), F32),
        scratch_shapes=[pltpu.VMEM((SSM_SW, SSM_CW), BF16)],
        compiler_params=_cparams(("arbitrary", "arbitrary", "arbitrary"), 52),
        name="s5_out",
    )(yi, xin.reshape(no, bsz, slab, LANES), c2)


def _merge_kernel(a_ref, y_ref, g0_ref, g1_ref, x_ref, wglu_ref, wb0_ref, wb1_ref, wout_ref, o_ref):
    y = y_ref[...]
    s = 0.5 * y * (1.0 + lax.erf(y * (2.0 ** -0.5)))
    glu = jnp.dot(s.astype(BF16), wglu_ref[...], preferred_element_type=F32)
    s = s * jax.nn.sigmoid(glu)
    pa = jnp.dot(a_ref[...], wb0_ref[...], preferred_element_type=F32)
    ps = jnp.dot(s.astype(BF16), wb1_ref[...], preferred_element_type=F32)
    merged = g0_ref[...].astype(F32) * pa + g1_ref[...].astype(F32) * ps
    o_ref[...] = x_ref[...] + jnp.dot(merged.astype(BF16), wout_ref[...], preferred_element_type=F32)


def _merge(a, y, proj, x, wglu, wbr, wout, layer, bm):
    t, d = x.shape
    aw = a.shape[1]
    g0 = 4 * SEG // d
    return pl.pallas_call(
        _merge_kernel,
        grid=(t // bm,),
        in_specs=[
            pl.BlockSpec((bm, aw), lambda i: (i, 0)),
            pl.BlockSpec((bm, aw), lambda i: (i, 0)),
            pl.BlockSpec((bm, d), lambda i: (i, g0)),
            pl.BlockSpec((bm, d), lambda i: (i, g0 + 1)),
            pl.BlockSpec((bm, d), lambda i: (i, 0)),
            pl.BlockSpec((None,) + wglu.shape[1:], lambda i: (layer, 0, 0)),
            pl.BlockSpec((None, None) + wbr.shape[2:], lambda i: (layer, 0, 0, 0)),
            pl.BlockSpec((None, None) + wbr.shape[2:], lambda i: (layer, 1, 0, 0)),
            pl.BlockSpec((None,) + wout.shape[1:], lambda i: (layer, 0, 0)),
        ],
        out_specs=pl.BlockSpec((bm, d), lambda i: (i, 0)),
        out_shape=jax.ShapeDtypeStruct((t, d), F32),
        compiler_params=_cparams(("arbitrary",), 48),
        name="merge",
    )(a, y, proj, proj, x, wglu, wbr, wbr, wout)


def _swiglu_partial(h, w1_ref, w3_ref, w2_ref, sub, side_work=None):
    ffc = w1_ref.shape[-1]
    acc = None
    for k, s in enumerate(range(0, ffc, sub)):
        e = min(s + sub, ffc)
        if side_work is not None:
            side_work(k, -(-ffc // sub))
        a = jnp.dot(h, w1_ref[:, s:e], preferred_element_type=F32)
        b = jnp.dot(h, w3_ref[:, s:e], preferred_element_type=F32)
        tt = (a * jax.nn.sigmoid(a) * b).astype(BF16)
        c = jnp.dot(tt, w2_ref[s:e, :], preferred_element_type=F32)
        acc = c if acc is None else acc + c
    return acc


FF_TILE = 2 * LANES
LOAD_DEPTH = 4


def _load_weights_bf16(w1_src, w3_src, w2_src, w1_ref, w3_ref, w2_ref, sta_ref, stb_ref, wsem):
    def stream(tiles, stage_ref):
        depth = stage_ref.shape[0]

        def copy(k):
            src, _ = tiles[k]
            return pltpu.make_async_copy(src, stage_ref.at[k % depth], wsem.at[k % depth])

        for k in range(min(depth, len(tiles))):
            copy(k).start()
        for k in range(len(tiles)):
            copy(k).wait()
            tiles[k][1](stage_ref[k % depth].astype(BF16))
            if k + depth < len(tiles):
                copy(k + depth).start()

    def row_sink(ref, c, rows):
        def put(v):
            ref[c * rows:(c + 1) * rows, :] = v
        return put

    ra, rb = sta_ref.shape[1], stb_ref.shape[1]
    up = []
    for c in range(w1_ref.shape[0] // ra):
        up.append((w1_src.at[pl.ds(c * ra, ra), :], row_sink(w1_ref, c, ra)))
        up.append((w3_src.at[pl.ds(c * ra, ra), :], row_sink(w3_ref, c, ra)))
    stream(up, sta_ref)
    stream([(w2_src.at[pl.ds(c * rb, rb), :], row_sink(w2_ref, c, rb)) for c in range(w2_ref.shape[0] // rb)],
           stb_ref)


def _ffn_weight_scratch(d, dff, dtype):
    return [pltpu.VMEM((d, dff), BF16), pltpu.VMEM((d, dff), BF16), pltpu.VMEM((dff, d), BF16),
            pltpu.VMEM((LOAD_DEPTH, LANES, dff), dtype), pltpu.VMEM((LOAD_DEPTH, FF_TILE, d), dtype),
            pltpu.SemaphoreType.DMA((LOAD_DEPTH,))]


def _ffn_kernel(x_ref, g_ref, w1_hbm, w3_hbm, w2_hbm, o_ref, w1_ref, w3_ref, w2_ref, sta_ref, stb_ref, wsem, *, li):
    @pl.when(pl.program_id(0) == 0)
    def _():
        _load_weights_bf16(w1_hbm.at[li], w3_hbm.at[li], w2_hbm.at[li], w1_ref, w3_ref, w2_ref,
                           sta_ref, stb_ref, wsem)

    x = x_ref[...]
    hn = _rms(x, g_ref[...]).astype(BF16)
    o_ref[...] = x + _swiglu_partial(hn, w1_ref, w3_ref, w2_ref, FF_TILE)


def _ffn(x, g, w1, w3, w2, li, bm):
    t, d = x.shape
    dff = w1.shape[2]
    assert dff % FF_TILE == 0
    return pl.pallas_call(
        functools.partial(_ffn_kernel, li=li),
        grid=(t // bm,),
        in_specs=[
            pl.BlockSpec((bm, d), lambda i: (i, 0)),
            pl.BlockSpec((1, d), lambda i: (0, 0)),
            pl.BlockSpec(memory_space=pl.ANY),
            pl.BlockSpec(memory_space=pl.ANY),
            pl.BlockSpec(memory_space=pl.ANY),
        ],
        out_specs=pl.BlockSpec((bm, d), lambda i: (i, 0)),
        out_shape=jax.ShapeDtypeStruct((t, d), F32),
        scratch_shapes=_ffn_weight_scratch(d, dff, w1.dtype),
        compiler_params=_cparams(("arbitrary",), 56),
        name="ffn_dense",
    )(x, g, w1, w3, w2)


def _router_kernel(x_ref, g_ref, wr_ref, hn_ref, r_ref):
    hn = _rms(x_ref[...], g_ref[...])
    _to_row_tiles(hn_ref, hn)
    logits = jnp.dot(hn, wr_ref[...], preferred_element_type=F32, precision=lax.Precision.HIGHEST)
    lane = lax.broadcasted_iota(I32, logits.shape, 1)
    neg = jnp.float32(-1e30)
    logits = jnp.where(lane < N_EXPERTS, logits, neg)
    m1 = jnp.max(logits, axis=-1, keepdims=True)
    i1 = jnp.min(jnp.where(logits == m1, lane, LANES), axis=-1, keepdims=True)
    rest = jnp.where(lane == i1, neg, logits)
    m2 = jnp.max(rest, axis=-1, keepdims=True)
    i2 = jnp.min(jnp.where(rest == m2, lane, LANES), axis=-1, keepdims=True)
    e2 = jnp.exp(m2 - m1)
    g1 = 1.0 / (1.0 + e2)
    g2 = e2 / (1.0 + e2)
    r_ref[...] = jnp.where(lane == 0, i1.astype(F32),
                           jnp.where(lane == 1, i2.astype(F32),
                                     jnp.where(lane == 2, g1, jnp.where(lane == 3, g2, 0.0))))


def _router(x, g, wr_pad, bm):
    t, d = x.shape
    s_per = d // LANES
    return pl.pallas_call(
        _router_kernel,
        grid=(t // bm,),
        in_specs=[
            pl.BlockSpec((bm, d), lambda i: (i, 0)),
            pl.BlockSpec((1, d), lambda i: (0, 0)),
            pl.BlockSpec((d, LANES), lambda i: (0, 0)),
        ],
        out_specs=[pl.BlockSpec((bm * s_per, LANES), lambda i: (i, 0)), pl.BlockSpec((bm, LANES), lambda i: (i, 0))],
        out_shape=[jax.ShapeDtypeStruct((t * s_per, LANES), F32), jax.ShapeDtypeStruct((t, LANES), F32)],
        compiler_params=_cparams(("arbitrary",), 40),
        name="moe_router",
    )(x, g, wr_pad)


def _row_copy(src_hbm, dst_ref, sem, src_row, dst_row, s_per):
    src = pl.multiple_of(src_row * s_per, s_per)
    dst = pl.multiple_of(dst_row * s_per, s_per)
    return pltpu.make_async_copy(src_hbm.at[pl.ds(src, s_per)], dst_ref.at[pl.ds(dst, s_per)], sem)


def _expert_kernel(be_ref, tok_ref, hn_hbm, w1_hbm, w3_hbm, w2_hbm, o_ref,
                   rows_ref, rsem, w1_ref, w3_ref, w2_ref, sta_ref, stb_ref, wsem, *, li, s_per):
    i = pl.program_id(0)
    bm = rows_ref.shape[1] // s_per

    def issue(step, slot):
        base = step * bm

        def body(r, c):
            _row_copy(hn_hbm, rows_ref.at[slot], rsem.at[slot], tok_ref[base + r], r, s_per).start()
            return c

        lax.fori_loop(0, bm, body, 0, unroll=8)

    def drain(slot):
        def body(r, c):
            _row_copy(hn_hbm, rows_ref.at[slot], rsem.at[slot], 0, r, s_per).wait()
            return c

        lax.fori_loop(0, bm, body, 0, unroll=8)

    @pl.when(i == 0)
    def _():
        issue(0, 0)

    nblk = pl.num_programs(0)
    active = i < be_ref[nblk]
    e = be_ref[i]
    changed = active & ((i == 0) | (e != be_ref[jnp.maximum(i - 1, 0)]))

    @pl.when(changed)
    def _():
        _load_weights_bf16(w1_hbm.at[li, e], w3_hbm.at[li, e], w2_hbm.at[li, e], w1_ref, w3_ref, w2_ref,
                           sta_ref, stb_ref, wsem)

    for slot in range(2):
        @pl.when((i % 2 == slot) & active)
        def _():
            drain(slot)
            xb = _from_row_tiles(rows_ref.at[slot], s_per).astype(BF16)

            def prefetch(k, n):
                per = -(-bm // max(n // 2, 1))
                for r in range(k * per, min(bm, (k + 1) * per)):
                    _row_copy(hn_hbm, rows_ref.at[1 - slot], rsem.at[1 - slot],
                              tok_ref[(i + 1) * bm + r], r, s_per).start()

            _to_row_tiles(o_ref, _swiglu_partial(xb, w1_ref, w3_ref, w2_ref, FF_TILE, prefetch))

        @pl.when((i % 2 == slot) & jnp.logical_not(active))
        def _():
            drain(slot)
            issue(i + 1, 1 - slot)
            o_ref[...] = jnp.zeros_like(o_ref)

        @pl.when((i % 2 == slot) & (i == nblk - 1))
        def _():
            drain(1 - slot)


def _experts(block_exp, slot_tok, hn, w1, w3, w2, li, bm):
    d = w1.shape[2]
    dff = w1.shape[3]
    s_per = d // LANES
    n = slot_tok.shape[0]
    assert dff % FF_TILE == 0
    slot_tok = jnp.concatenate([slot_tok, jnp.zeros((bm,), slot_tok.dtype)])
    return pl.pallas_call(
        functools.partial(_expert_kernel, li=li, s_per=s_per),
        grid_spec=pltpu.PrefetchScalarGridSpec(
            num_scalar_prefetch=2,
            grid=(n // bm,),
            in_specs=[pl.BlockSpec(memory_space=pl.ANY)] * 4,
            out_specs=pl.BlockSpec((bm * s_per, LANES), lambda i, be, tok: (i, 0)),
            scratch_shapes=[pltpu.VMEM((2, bm * s_per, LANES), F32), pltpu.SemaphoreType.DMA((2,))]
            + _ffn_weight_scratch(d, dff, w1.dtype),
        ),
        out_shape=jax.ShapeDtypeStruct((n * s_per, LANES), F32),
        compiler_params=_cparams(("arbitrary",), 56),
        name="moe_experts",
    )(block_exp, slot_tok, hn, w1, w3, w2)


def _combine_kernel(d0_ref, d1_ref, yb_hbm, x_ref, r_ref, g_ref, o_ref, r0_ref, r1_ref, sem, *, s_per, normalize):
    bm = o_ref.shape[0]
    i = pl.program_id(0)

    def issue(step, slot):
        base = step * bm

        def body(r, c):
            _row_copy(yb_hbm, r0_ref.at[slot], sem.at[slot], d0_ref[base + r], r, s_per).start()
            _row_copy(yb_hbm, r1_ref.at[slot], sem.at[slot], d1_ref[base + r], r, s_per).start()
            return c

        lax.fori_loop(0, bm, body, 0, unroll=8)

    def drain(slot):
        def body(r, c):
            _row_copy(yb_hbm, r0_ref.at[slot], sem.at[slot], 0, r, s_per).wait()
            _row_copy(yb_hbm, r1_ref.at[slot], sem.at[slot], 0, r, s_per).wait()
            return c

        lax.fori_loop(0, bm, body, 0, unroll=8)

    @pl.when(i == 0)
    def _():
        issue(0, 0)

    for slot in range(2):
        @pl.when(i % 2 == slot)
        def _():
            @pl.when(i + 1 < pl.num_programs(0))
            def _():
                issue(i + 1, 1 - slot)

            drain(slot)
            route = r_ref[...]
            out = (x_ref[...] + route[:, 2:3] * _from_row_tiles(r0_ref.at[slot], s_per)
                   + route[:, 3:4] * _from_row_tiles(r1_ref.at[slot], s_per))
            o_ref[...] = _rms(out, g_ref[...]) if normalize else out


def _combine(d0, d1, yb, x, route, g_final, normalize, bm):
    t, d = x.shape
    s_per = d // LANES
    return pl.pallas_call(
        functools.partial(_combine_kernel, s_per=s_per, normalize=normalize),
        grid_spec=pltpu.PrefetchScalarGridSpec(
            num_scalar_prefetch=2,
            grid=(t // bm,),
            in_specs=[
                pl.BlockSpec(memory_space=pl.ANY),
                pl.BlockSpec((bm, d), lambda i, a, b: (i, 0)),
                pl.BlockSpec((bm, LANES), lambda i, a, b: (i, 0)),
                pl.BlockSpec((1, d), lambda i, a, b: (0, 0)),
            ],
            out_specs=pl.BlockSpec((bm, d), lambda i, a, b: (i, 0)),
            scratch_shapes=[pltpu.VMEM((2, bm * s_per, LANES), F32), pltpu.VMEM((2, bm * s_per, LANES), F32),
                            pltpu.SemaphoreType.DMA((2,))],
        ),
        out_shape=jax.ShapeDtypeStruct((t, d), F32),
        compiler_params=_cparams(("arbitrary",), 32),
        name="moe_combine",
    )(d0, d1, yb, x, route, g_final)


def _dispatch_tables(idx, bm):
    t = idx.shape[0]
    flat_e = idx.reshape(-1)
    onehot = (flat_e[:, None] == jnp.arange(N_EXPERTS, dtype=I32)[None, :]).astype(I32)
    csum = jnp.cumsum(onehot, axis=0)
    rank = jnp.take_along_axis(csum, flat_e[:, None], axis=1)[:, 0] - 1
    counts = csum[-1]
    padded = (counts + bm - 1) // bm * bm
    pad_end = jnp.cumsum(padded)
    pad_start = pad_end - padded
    dest = (pad_start[flat_e] + rank).astype(I32)
    n_slots = t * TOP_K + N_EXPERTS * bm
    slot_tok = jnp.zeros((n_slots,), I32).at[dest].set(jnp.arange(t * TOP_K, dtype=I32) // TOP_K)
    block_start = jnp.arange(n_slots // bm, dtype=I32) * bm
    block_exp = jnp.minimum(jnp.searchsorted(pad_end, block_start, side='right'), N_EXPERTS - 1).astype(I32)
    block_exp = jnp.concatenate([block_exp, (pad_end[-1:] // bm).astype(I32)])
    return dest.reshape(t, TOP_K), slot_tok, block_exp


def _moe(x, g, w_router, w1, w3, w2, li, g_final, normalize, bm_tok, bm_slot):
    d = x.shape[1]
    wr_pad = jnp.zeros((d, LANES), F32).at[:, :N_EXPERTS].set(w_router)
    hn, route = _router(x, g, wr_pad, bm_tok)
    idx = route[:, :TOP_K].astype(I32)
    dest, slot_tok, block_exp = _dispatch_tables(idx, bm_slot)
    yb = _experts(block_exp, slot_tok, hn, w1, w3, w2, li, bm_slot)
    return _combine(dest[:, 0], dest[:, 1], yb, x, route, g_final, normalize, bm_tok)


def _norm_kernel(x_ref, g_ref, o_ref):
    o_ref[...] = _rms(x_ref[...], g_ref[...])


def _final_norm(x, g, bm):
    t, d = x.shape
    return pl.pallas_call(
        _norm_kernel,
        grid=(t // bm,),
        in_specs=[pl.BlockSpec((bm, d), lambda i: (i, 0)), pl.BlockSpec((1, d), lambda i: (0, 0))],
        out_specs=pl.BlockSpec((bm, d), lambda i: (i, 0)),
        out_shape=jax.ShapeDtypeStruct((t, d), F32),
        compiler_params=_cparams(("arbitrary",), 32),
        name="final_norm",
    )(x, g)


def _rope_tables(seq):
    half = ATTN_HEAD_DIM // 2
    inv_freq = 1.0 / (ROPE_THETA ** (jnp.arange(half, dtype=F32) / half))
    ang = jnp.arange(seq, dtype=F32)[:, None] * inv_freq[None, :]
    cos = jnp.tile(jnp.cos(ang), (1, 4))
    sin = jnp.sin(ang)
    return cos, jnp.concatenate([-sin, sin, -sin, sin], axis=1)


def _block(n, pref):
    return pref if n % pref == 0 else n


def kernel(x, norm_mix, w_in, attn_lambda, attn_subln, ssm_lam_re, ssm_lam_im, ssm_log_dt, ssm_b_re, ssm_b_im, ssm_c_re, ssm_c_im, ssm_d, w_glu, w_branch, w_out, norm_ffn, ffn_w1, ffn_w3, ffn_w2, moe_router, moe_w1, moe_w3, moe_w2, norm_final):
    bsz, seq, d = x.shape
    depth = w_in.shape[0]
    t = bsz * seq
    cos_t, sin_t = _rope_tables(seq)
    xf = x.reshape(t, d)
    bm = _block(seq, 1024)
    w_all = w_in.astype(BF16)
    a2, b2, c2, apow = jax.vmap(_s5_tables)(ssm_lam_re, ssm_lam_im, ssm_log_dt, ssm_b_re, ssm_b_im,
                                            ssm_c_re, ssm_c_im, ssm_d)
    wglu, wbr, wout = w_glu.astype(BF16), w_branch.astype(BF16), w_out.astype(BF16)
    for layer in range(depth):
        proj, u = _inproj(xf, norm_mix[layer][None], w_all, layer, cos_t, sin_t, seq, _block(seq, 512))

        lam_init = jnp.full((1,), 0.8 - 0.6 * math.exp(-0.3 * layer), F32)
        a = _attention(proj, lam_init, attn_lambda[layer], attn_subln[layer][None], bsz, seq,
                       _block(seq, 1024), _block(seq // 2, 512))

        y = _s5(u, a2[layer], b2[layer], c2[layer], apow[layer], bsz, _block(seq // SSM_CHUNK, 512))

        xf = _merge(a, y, proj, xf, wglu, wbr, wout, layer, _block(seq, 512))

        i = layer // 2
        if layer % 2 == 0:
            xf = _ffn(xf, norm_ffn[layer][None], ffn_w1, ffn_w3, ffn_w2, i, _block(seq, 512))
        else:
            last = layer == depth - 1
            xf = _moe(xf, norm_ffn[layer][None], moe_router[i], moe_w1, moe_w3, moe_w2, i,
                      norm_final[None], last, _block(seq, 256), 512)
    if depth % 2 == 1:
        xf = _final_norm(xf, norm_final[None], bm)
    return xf.reshape(bsz, seq, d)
```
